```python
import math
import jax, jax.numpy as jnp
from jax import lax
import numpy as np

D_MODEL = 2048
BATCH = 8
SEQ = 2048
DEPTH = 1

D_RNN = D_MODEL
LRU_BLOCKS = 16
LRU_BLOCK = D_RNN // LRU_BLOCKS
LRU_CONV = 4
LRU_C = 8.0
N_HEADS = 16
HEAD_DIM = 128
IDX_HEADS = 16
IDX_DIM = 64
TOPK_MAX = 256
Q_BLOCK = 128
REL_BUCKETS = 32
REL_MAX_DIST = 128
D_FF = 3 * D_MODEL
FFN_CONV = 3
LN_EPS = 1e-5
ALPHA = (2.0 * DEPTH) ** 0.25
BETA = (8.0 * DEPTH) ** -0.25

SPLITS = (D_RNN, D_RNN, N_HEADS * HEAD_DIM, HEAD_DIM, HEAD_DIM,
          IDX_HEADS * IDX_DIM, IDX_DIM, IDX_HEADS, D_MODEL, D_MODEL)
D_IN = sum(SPLITS)
OFFSETS = tuple(int(v) for v in np.cumsum(SPLITS)[:-1])

kernel_name = "hybrid_rglru_dsa_convffn_deepnorm"


def layer_norm(x, g, b):
    xf = x.astype(jnp.float32)
    mu = jnp.mean(xf, axis=-1, keepdims=True)
    var = jnp.mean(jnp.square(xf - mu), axis=-1, keepdims=True)
    y = (xf - mu) * lax.rsqrt(var + LN_EPS) * g.astype(jnp.float32) + b.astype(jnp.float32)
    return y.astype(x.dtype)


def causal_dwconv(x, w, b):
    width = w.shape[0]
    S = x.shape[1]
    xp = jnp.pad(x, ((0, 0), (width - 1, 0), (0, 0)))
    y = b + xp[:, 0:S] * w[0]
    for j in range(1, width):
        y = y + xp[:, j:j + S] * w[j]
    return y


def t5_bucket(rel):
    max_exact = REL_BUCKETS // 2
    nf = jnp.maximum(rel, 1).astype(jnp.float32)
    large = max_exact + (jnp.log(nf / max_exact) / math.log(REL_MAX_DIST / max_exact)
                         * (REL_BUCKETS - max_exact)).astype(jnp.int32)
    large = jnp.minimum(large, REL_BUCKETS - 1)
    return jnp.where(rel < max_exact, rel, large)


def rg_lru(x, gate_a_w, gate_a_b, gate_x_w, gate_x_b, lam):
    B, S, _ = x.shape
    xb = x.reshape(B, S, LRU_BLOCKS, LRU_BLOCK)
    r = jax.nn.sigmoid(jnp.einsum('bsnj,njk->bsnk', xb, gate_a_w).reshape(B, S, D_RNN) + gate_a_b)
    i = jax.nn.sigmoid(jnp.einsum('bsnj,njk->bsnk', xb, gate_x_w).reshape(B, S, D_RNN) + gate_x_b)
    log_a = -LRU_C * r.astype(jnp.float32) * jax.nn.softplus(-lam.astype(jnp.float32))
    a = jnp.exp(log_a)
    mult = jnp.sqrt(-jnp.expm1(2.0 * log_a))
    first = (jnp.arange(S) == 0)[None, :, None]
    mult = jnp.where(first, 1.0, mult)
    u = mult * (i * x).astype(jnp.float32)

    def combine(left, right):
        a1, b1 = left
        a2, b2 = right
        return a1 * a2, a2 * b1 + b2

    _, h = lax.associative_scan(combine, (a, u), axis=1)
    return h.astype(x.dtype)


def sparse_attention(q, k, v, qi, ki, wi, rel_bias, top_k):
    B, S, H, Dh = q.shape
    nb = S // Q_BLOCK
    qi32 = qi.astype(jnp.float32)
    ki32 = ki.astype(jnp.float32)
    wi32 = wi.astype(jnp.float32) * (IDX_HEADS ** -0.5)
    key_pos = jnp.arange(S)

    def to_blocks(t):
        return jnp.moveaxis(t.reshape((B, nb, Q_BLOCK) + t.shape[2:]), 1, 0)

    q_blocks = to_blocks(q)
    qi_blocks = to_blocks(qi32)
    wi_blocks = to_blocks(wi32)
    t_blocks = jnp.arange(S, dtype=jnp.int32).reshape(nb, Q_BLOCK)
    gather = jax.vmap(lambda tb, ib: tb[ib])

    def block_fn(args):
        q_b, qi_b, w_b, t_b = args
        dots = jnp.einsum('bthd,bsd->bths', qi_b, ki32) * (IDX_DIM ** -0.5)
        score = jnp.einsum('bths,bth->bts', jax.nn.relu(dots), w_b)
        causal = key_pos[None, :] <= t_b[:, None]
        score = jnp.where(causal[None], score, -jnp.inf)
        _, idx = lax.top_k(score, top_k)
        valid = idx <= t_b[None, :, None]
        k_sel = gather(k, idx)
        v_sel = gather(v, idx)
        logits = jnp.einsum('bthd,btkd->bhtk', q_b, k_sel).astype(jnp.float32) * (Dh ** -0.5)
        rel = t_b[None, :, None] - idx
        bias = rel_bias[t5_bucket(rel)].astype(jnp.float32)
        logits = logits + jnp.transpose(bias, (0, 3, 1, 2))
        logits = jnp.where(valid[:, None], logits, -jnp.inf)
        p = jax.nn.softmax(logits, axis=-1).astype(v.dtype)
        return jnp.einsum('bhtk,btkd->bthd', p, v_sel)

    out = lax.map(block_fn, (q_blocks, qi_blocks, wi_blocks, t_blocks))
    return jnp.moveaxis(out, 0, 1).reshape(B, S, H * Dh)


def setup_inputs(seed: int = 0) -> dict:
    key = jax.random.key(seed)
    ks = jax.random.split(key, 24)

    def nrm(k, shape, scale):
        return jax.random.normal(k, shape, jnp.float32) * scale

    u = jax.random.uniform(ks[9], (DEPTH, D_RNN), jnp.float32, minval=0.9 ** 2, maxval=0.999 ** 2)
    lru_lambda = -jnp.log(jnp.expm1(-0.5 * jnp.log(u)))
    return {
        "x": nrm(ks[0], (BATCH, SEQ, D_MODEL), 1.0),
        "w_in": nrm(ks[1], (DEPTH, D_MODEL, D_IN), D_MODEL ** -0.5),
        "lru_conv_w": nrm(ks[2], (DEPTH, LRU_CONV, D_RNN), LRU_CONV ** -0.5),
        "lru_conv_b": nrm(ks[3], (DEPTH, D_RNN), 0.01),
        "lru_gate_a_w": nrm(ks[4], (DEPTH, LRU_BLOCKS, LRU_BLOCK, LRU_BLOCK), LRU_BLOCK ** -0.5),
        "lru_gate_a_b": nrm(ks[5], (DEPTH, D_RNN), 0.01),
        "lru_gate_x_w": nrm(ks[6], (DEPTH, LRU_BLOCKS, LRU_BLOCK, LRU_BLOCK), LRU_BLOCK ** -0.5),
        "lru_gate_x_b": nrm(ks[7], (DEPTH, D_RNN), 0.01),
        "lru_lambda": lru_lambda,
        "idx_knorm_g": 1.0 + nrm(ks[10], (DEPTH, IDX_DIM), 0.01),
        "idx_knorm_b": nrm(ks[11], (DEPTH, IDX_DIM), 0.01),
        "rel_bias": nrm(ks[12], (REL_BUCKETS, N_HEADS), 0.1),
        "w_proj_lru": nrm(ks[13], (DEPTH, D_RNN, D_MODEL), D_RNN ** -0.5),
        "w_proj_attn": nrm(ks[14], (DEPTH, N_HEADS * HEAD_DIM, D_MODEL), (N_HEADS * HEAD_DIM) ** -0.5),
        "w_out": nrm(ks[15], (DEPTH, D_MODEL, D_MODEL), D_MODEL ** -0.5 * BETA),
        "ln1_g": 1.0 + nrm(ks[16], (DEPTH, D_MODEL), 0.01),
        "ln1_b": nrm(ks[17], (DEPTH, D_MODEL), 0.01),
        "ffn_w_up": nrm(ks[18], (DEPTH, D_MODEL, 2 * D_FF), D_MODEL ** -0.5),
        "ffn_conv_w": nrm(ks[19], (DEPTH, FFN_CONV, 2 * D_FF), FFN_CONV ** -0.5),
        "ffn_conv_b": nrm(ks[20], (DEPTH, 2 * D_FF), 0.01),
        "ffn_w_down": nrm(ks[21], (DEPTH, D_FF, D_MODEL), D_FF ** -0.5 * BETA),
        "ln2_g": 1.0 + nrm(ks[22], (DEPTH, D_MODEL), 0.01),
        "ln2_b": nrm(ks[23], (DEPTH, D_MODEL), 0.01),
    }


def reference(x, w_in, lru_conv_w, lru_conv_b, lru_gate_a_w, lru_gate_a_b, lru_gate_x_w, lru_gate_x_b,
              lru_lambda, idx_knorm_g, idx_knorm_b, rel_bias, w_proj_lru, w_proj_attn, w_out,
              ln1_g, ln1_b, ffn_w_up, ffn_conv_w, ffn_conv_b, ffn_w_down, ln2_g, ln2_b):
    B, S, _ = x.shape
    top_k = min(TOPK_MAX, S // 4)
    h = x
    for l in range(DEPTH):
        proj = h @ w_in[l]
        lru_x, lru_g, q, k, v, qi, ki, wi, g_lru, g_att = jnp.split(proj, OFFSETS, axis=-1)
        lru_x = causal_dwconv(lru_x, lru_conv_w[l], lru_conv_b[l])
        y_lru = jax.nn.gelu(lru_g, approximate=True) * rg_lru(
            lru_x, lru_gate_a_w[l], lru_gate_a_b[l], lru_gate_x_w[l], lru_gate_x_b[l], lru_lambda[l])
        q = q.reshape(B, S, N_HEADS, HEAD_DIM)
        qi = qi.reshape(B, S, IDX_HEADS, IDX_DIM)
        ki = layer_norm(ki, idx_knorm_g[l], idx_knorm_b[l])
        y_att = sparse_attention(q, k, v, qi, ki, wi, rel_bias, top_k)
        merged = (jax.nn.sigmoid(g_lru) * (y_lru @ w_proj_lru[l])
                  + jax.nn.sigmoid(g_att) * (y_att @ w_proj_attn[l]))
        h = layer_norm(ALPHA * h + merged @ w_out[l], ln1_g[l], ln1_b[l])
        up = causal_dwconv(h @ ffn_w_up[l], ffn_conv_w[l], ffn_conv_b[l])
        up_g, up_v = jnp.split(up, 2, axis=-1)
        ffn = (jax.nn.gelu(up_g, approximate=True) * up_v) @ ffn_w_down[l]
        h = layer_norm(ALPHA * h + ffn, ln2_g[l], ln2_b[l])
    return h
```

```python
import functools
import math

import jax
import jax.numpy as jnp
from jax import lax
from jax.experimental import pallas as pl
from jax.experimental.pallas import tpu as pltpu

N_HEADS = 16
HEAD_DIM = 128
IDX_HEADS = 16
IDX_DIM = 64
TOPK_MAX = 256
LRU_BLOCK = 128
LRU_C = 8.0
REL_BUCKETS = 32
REL_MAX_DIST = 128
LN_EPS = 1e-5
DEPTH = 1
ALPHA = (2.0 * DEPTH) ** 0.25

LANES = 128
SUBLANES = 8
VMEM_LIMIT_BYTES = 56 * 1024 * 1024

TQ = 128
CK = 256
BIAS_TILE_STEP = 128
N_BIAS_TILES = 3

INT_MIN = -(2 ** 31)
MASK_LOGIT = -2e30
F32 = jnp.float32
BF16 = jnp.bfloat16


def _params(n_axes):
    return pltpu.CompilerParams(dimension_semantics=("arbitrary",) * n_axes,
                                vmem_limit_bytes=VMEM_LIMIT_BYTES)


def _matmul_kernel(x_ref, w_ref, o_ref):
    o_ref[...] = jnp.dot(x_ref[...], w_ref[...], preferred_element_type=F32).astype(o_ref.dtype)


def _project_time_major(x_bsd, w, out_dtype, ts, tn):
    B, S, D = x_bsd.shape
    N = w.shape[1]
    nj = N // tn
    return pl.pallas_call(
        _matmul_kernel,
        grid=(B, S // ts, nj),
        in_specs=[pl.BlockSpec((None, ts, D), lambda b, i, j: (b, i, 0)),
                  pl.BlockSpec((D, tn), lambda b, i, j: (0, j))],
        out_specs=pl.BlockSpec((ts, tn), lambda b, i, j: (i, b * nj + j)),
        out_shape=jax.ShapeDtypeStruct((S, B * N), out_dtype),
        compiler_params=_params(3),
        name="in_proj",
    )(x_bsd, w)


def _gelu_tanh(x):
    return 0.5 * x * (1.0 + jnp.tanh(math.sqrt(2.0 / math.pi) * (x + 0.044715 * (x * x * x))))


def _lru_kernel(lx_ref, lg_ref, cw_ref, cb_ref, wa_ref, ba_ref, wx_ref, bx_ref, lam_ref, o_ref,
                xs, a_s, u_s, h_s, *, tc, cb_width, conv_w):
    ti = pl.program_id(1)
    halo = conv_w - 1

    @pl.when(ti == 0)
    def _():
        xs[0:halo] = jnp.zeros((halo, SUBLANES, cb_width), F32)
        h_s[...] = jnp.zeros((SUBLANES, cb_width), F32)

    xs[halo:halo + tc] = lx_ref[...]
    y = cb_ref[...] + cw_ref[0:1, :] * xs[0:tc]
    for j in range(1, conv_w):
        y = y + cw_ref[j:j + 1, :] * xs[j:j + tc]
    xs[0:halo] = xs[tc:tc + halo]

    y2 = y.reshape(tc * SUBLANES, cb_width)
    yb = y2.astype(BF16)
    r_parts, i_parts = [], []
    for n in range(cb_width // LRU_BLOCK):
        blk = yb[:, n * LRU_BLOCK:(n + 1) * LRU_BLOCK]
        r_parts.append(jnp.dot(blk, wa_ref[n], preferred_element_type=F32))
        i_parts.append(jnp.dot(blk, wx_ref[n], preferred_element_type=F32))
    r = jax.nn.sigmoid(jnp.concatenate(r_parts, axis=1) + ba_ref[...])
    g_in = jax.nn.sigmoid(jnp.concatenate(i_parts, axis=1) + bx_ref[...])

    z = -lam_ref[...]
    softplus = jnp.maximum(z, 0.0) + jnp.log1p(jnp.exp(-jnp.abs(z)))
    log_a = (-LRU_C) * r * softplus
    a = jnp.exp(log_a)
    mult = jnp.sqrt(-jnp.tanh(log_a) * (a * a + 1.0))
    u = mult * (g_in * y2)
    a_s[...] = a.reshape(tc, SUBLANES, cb_width)
    u_s[...] = u.reshape(tc, SUBLANES, cb_width)

    @pl.when(ti == 0)
    def _():
        u_s[0] = (g_in * y2)[0:SUBLANES]

    def step(t, h):
        h = a_s[t] * h + u_s[t]
        u_s[t] = h
        return h

    h_s[...] = lax.fori_loop(0, tc, step, h_s[...], unroll=8)
    out = _gelu_tanh(lg_ref[...]) * u_s[...]
    o_ref[...] = out.reshape(tc * SUBLANES, cb_width).astype(o_ref.dtype)


def _rg_lru(lxg, conv_w, conv_b, gate_a_w, gate_a_b, gate_x_w, gate_x_b, lam, S, B, tc, cbw):
    d_rnn = conv_w.shape[1]
    width = conv_w.shape[0]
    ncb = d_rnn // cbw
    nblk = cbw // LRU_BLOCK
    row = lambda v: v.reshape(1, d_rnn)
    kern = functools.partial(_lru_kernel, tc=tc, cb_width=cbw, conv_w=width)
    vec_spec = pl.BlockSpec((1, cbw), lambda n, t: (0, n))
    return pl.pallas_call(
        kern,
        grid=(ncb, S // tc),
        in_specs=[pl.BlockSpec((tc, B, cbw), lambda n, t: (t, 0, n)),
                  pl.BlockSpec((tc, B, cbw), lambda n, t: (t, 0, ncb + n)),
                  pl.BlockSpec((width, cbw), lambda n, t: (0, n)),
                  vec_spec,
                  pl.BlockSpec((nblk, LRU_BLOCK, LRU_BLOCK), lambda n, t: (n, 0, 0)),
                  vec_spec,
                  pl.BlockSpec((nblk, LRU_BLOCK, LRU_BLOCK), lambda n, t: (n, 0, 0)),
                  vec_spec,
                  vec_spec],
        out_specs=pl.BlockSpec((tc * B, cbw), lambda n, t: (t, n)),
        out_shape=jax.ShapeDtypeStruct((S * B, d_rnn), BF16),
        scratch_shapes=[pltpu.VMEM((tc + width - 1, B, cbw), F32),
                        pltpu.VMEM((tc, B, cbw), F32),
                        pltpu.VMEM((tc, B, cbw), F32),
                        pltpu.VMEM((B, cbw), F32)],
        compiler_params=_params(2),
        name="rg_lru",
    )(lxg, lxg, conv_w, row(conv_b), gate_a_w.astype(BF16), row(gate_a_b),
      gate_x_w.astype(BF16), row(gate_x_b), row(lam))


def _bias_tile_kernel(rb_ref, o_ref):
    d = pl.program_id(0)
    h = pl.program_id(1)
    rows = lax.broadcasted_iota(jnp.int32, (TQ, CK), 0)
    cols = lax.broadcasted_iota(jnp.int32, (TQ, CK), 1)
    rel = d * BIAS_TILE_STEP + rows - cols
    max_exact = REL_BUCKETS // 2
    nf = jnp.maximum(rel, 1).astype(F32)
    large = max_exact + (jnp.log(nf / max_exact) / math.log(REL_MAX_DIST / max_exact)
                         * (REL_BUCKETS - max_exact)).astype(jnp.int32)
    large = jnp.minimum(large, REL_BUCKETS - 1)
    bucket = jnp.where(rel < max_exact, rel, large)
    acc = jnp.zeros((TQ, CK), F32)
    for b in range(REL_BUCKETS):
        acc = jnp.where(bucket == b, rb_ref[b, h], acc)
    o_ref[...] = acc


def _bias_tiles(rel_bias):
    return pl.pallas_call(
        _bias_tile_kernel,
        grid=(N_BIAS_TILES, N_HEADS),
        in_specs=[pl.BlockSpec(memory_space=pltpu.SMEM)],
        out_specs=pl.BlockSpec((None, None, TQ, CK), lambda d, h: (d, h, 0, 0)),
        out_shape=jax.ShapeDtypeStruct((N_BIAS_TILES, N_HEADS, TQ, CK), F32),
        compiler_params=_params(2),
        name="t5_bias_tiles",
    )(rel_bias)


def _ordered_key(score):
    bits = pltpu.bitcast(score, jnp.int32)
    return jnp.where(bits >= 0, bits, bits ^ jnp.int32(0x7FFFFFFF))


def _attn_kernel(far_ref, qi_ref, kvw_ref, q_ref, bias_ref, kg_ref, kb_ref, o_ref,
                 kin, kbuf, vbuf, keys, qs, pbuf, m_s, l_s, acc_s, *, top_k):
    i = pl.program_id(1)
    k_off, v_off, kw_off = 0, HEAD_DIM, 2 * HEAD_DIM

    @pl.when(i == 0)
    def _():
        kraw = kvw_ref[:, kw_off:kw_off + IDX_DIM]
        mu = jnp.mean(kraw, axis=-1, keepdims=True)
        var = jnp.mean(jnp.square(kraw - mu), axis=-1, keepdims=True)
        kn = (kraw - mu) * lax.rsqrt(var + LN_EPS) * kg_ref[...] + kb_ref[...]
        kin[...] = kn.astype(BF16)
        kbuf[...] = kvw_ref[:, k_off:k_off + HEAD_DIM].astype(BF16)
        vbuf[...] = kvw_ref[:, v_off:v_off + HEAD_DIM].astype(BF16)

    q0 = pl.multiple_of(i * TQ, TQ)
    n_chunks = (q0 + TQ + CK - 1) // CK
    rows = q0 + lax.broadcasted_iota(jnp.int32, (TQ, CK), 0)
    cols = lax.broadcasted_iota(jnp.int32, (TQ, CK), 1)

    qib = qi_ref[...].astype(BF16)
    wi = kvw_ref[pl.ds(q0, TQ), kw_off + IDX_DIM:kw_off + IDX_DIM + IDX_HEADS]
    wi = wi * ((IDX_DIM ** -0.5) * (IDX_HEADS ** -0.5))

    def score_chunk(c, carry):
        c0 = pl.multiple_of(c * CK, CK)
        kc = kin[pl.ds(c0, CK), :]
        acc = jnp.zeros((TQ, CK), F32)
        for h in range(IDX_HEADS):
            d = lax.dot_general(qib[:, h * IDX_DIM:(h + 1) * IDX_DIM], kc,
                                (((1,), (1,)), ((), ())), preferred_element_type=F32)
            acc = acc + wi[:, h:h + 1] * jnp.maximum(d, 0.0)
        keys[c] = jnp.where(c0 + cols <= rows, _ordered_key(acc), jnp.int32(INT_MIN))
        return carry

    lax.fori_loop(0, n_chunks, score_chunk, 0)

    def bit_step(step, thr):
        cand = thr + lax.shift_left(jnp.int32(1), 31 - step)

        def count_chunk(c, cnt):
            return cnt + jnp.where(keys[c] >= cand, 1.0, 0.0)

        cnt = lax.fori_loop(0, n_chunks, count_chunk, jnp.zeros((TQ, CK), F32))
        total = jnp.sum(cnt, axis=1, keepdims=True)
        return jnp.where(total >= top_k, cand, thr)

    thr = lax.fori_loop(0, 32, bit_step, jnp.full((TQ, 1), INT_MIN, jnp.int32))
    thr = jnp.maximum(thr, jnp.int32(INT_MIN + 1))

    for h in range(N_HEADS):
        qs[h * TQ:(h + 1) * TQ, :] = q_ref[:, h * HEAD_DIM:(h + 1) * HEAD_DIM]
    m_s[...] = jnp.full(m_s.shape, 0.5 * MASK_LOGIT, F32)
    l_s[...] = jnp.zeros(l_s.shape, F32)
    acc_s[...] = jnp.zeros(acc_s.shape, F32)
    scale = HEAD_DIM ** -0.5

    def attn_chunk(c, carry):
        c0 = pl.multiple_of(c * CK, CK)
        s_all = lax.dot_general(qs[...], kbuf[pl.ds(c0, CK), :],
                                (((1,), (1,)), ((), ())), preferred_element_type=F32)
        sel = keys[c] >= thr
        delta = q0 - c0
        near = delta < N_BIAS_TILES * BIAS_TILE_STEP
        tile = jnp.clip(delta // BIAS_TILE_STEP, 0, N_BIAS_TILES - 1)
        for h in range(N_HEADS):
            bias = jnp.where(near, bias_ref[tile, h], far_ref[h])
            s = s_all[h * TQ:(h + 1) * TQ] * scale + bias
            s = jnp.where(sel, s, MASK_LOGIT)
            m_old = m_s[h]
            m_new = jnp.maximum(m_old, jnp.max(s, axis=1, keepdims=True))
            p = jnp.exp(s - jnp.concatenate([m_new] * (CK // LANES), axis=1))
            alpha = jnp.exp(m_old - m_new)
            l_s[h] = alpha * l_s[h] + jnp.sum(p, axis=1, keepdims=True)
            acc_s[h] = alpha * acc_s[h]
            m_s[h] = m_new
            pbuf[h * TQ:(h + 1) * TQ, :] = p.astype(BF16)
        pv = jnp.dot(pbuf[...], vbuf[pl.ds(c0, CK), :], preferred_element_type=F32)
        acc_s[...] = acc_s[...] + pv.reshape(N_HEADS, TQ, HEAD_DIM)
        return carry

    lax.fori_loop(0, n_chunks, attn_chunk, 0)
    for h in range(N_HEADS):
        o_ref[:, h * HEAD_DIM:(h + 1) * HEAD_DIM] = (acc_s[h] / l_s[h]).astype(o_ref.dtype)


def _sparse_attention(qi, kvw, q, bias_tiles, far_bias, knorm_g, knorm_b, S, B, top_k):
    kern = functools.partial(_attn_kernel, top_k=top_k)
    qi_w = IDX_HEADS * IDX_DIM
    q_w = N_HEADS * HEAD_DIM
    kvw_w = kvw.shape[1] // B
    return pl.pallas_call(
        kern,
        grid=(B, S // TQ),
        in_specs=[pl.BlockSpec(memory_space=pltpu.SMEM),
                  pl.BlockSpec((TQ, qi_w), lambda b, i: (i, b)),
                  pl.BlockSpec((S, kvw_w), lambda b, i: (0, b)),
                  pl.BlockSpec((TQ, q_w), lambda b, i: (i, b)),
                  pl.BlockSpec((N_BIAS_TILES, N_HEADS, TQ, CK), lambda b, i: (0, 0, 0, 0)),
                  pl.BlockSpec((1, IDX_DIM), lambda b, i: (0, 0)),
                  pl.BlockSpec((1, IDX_DIM), lambda b, i: (0, 0))],
        out_specs=pl.BlockSpec((TQ, q_w), lambda b, i: (i, b)),
        out_shape=jax.ShapeDtypeStruct((S, B * q_w), BF16),
        scratch_shapes=[pltpu.VMEM((S, IDX_DIM), BF16),
                        pltpu.VMEM((S, HEAD_DIM), BF16),
                        pltpu.VMEM((S, HEAD_DIM), BF16),
                        pltpu.VMEM((S // CK, TQ, CK), jnp.int32),
                        pltpu.VMEM((N_HEADS * TQ, HEAD_DIM), BF16),
                        pltpu.VMEM((N_HEADS * TQ, CK), BF16),
                        pltpu.VMEM((N_HEADS, TQ, LANES), F32),
                        pltpu.VMEM((N_HEADS, TQ, LANES), F32),
                        pltpu.VMEM((N_HEADS, TQ, HEAD_DIM), F32)],
        compiler_params=_params(2),
        name="dsa_attention",
    )(far_bias, qi, kvw, q, bias_tiles, knorm_g.reshape(1, IDX_DIM), knorm_b.reshape(1, IDX_DIM))


def _merge_kernel(yl_ref, ya_ref, wl_ref, wa_ref, gl_ref, ga_ref, o_ref):
    pl_ = jnp.dot(yl_ref[...], wl_ref[...], preferred_element_type=F32)
    pa_ = jnp.dot(ya_ref[...], wa_ref[...], preferred_element_type=F32)
    merged = jax.nn.sigmoid(gl_ref[...]) * pl_ + jax.nn.sigmoid(ga_ref[...]) * pa_
    o_ref[...] = merged.astype(o_ref.dtype)


def _gated_merge(y_lru, y_att, w_l, w_a, gates, tm, tn):
    M, D = y_lru.shape
    N = w_l.shape[1]
    nj = N // tn
    return pl.pallas_call(
        _merge_kernel,
        grid=(M // tm, nj),
        in_specs=[pl.BlockSpec((tm, D), lambda i, j: (i, 0)),
                  pl.BlockSpec((tm, D), lambda i, j: (i, 0)),
                  pl.BlockSpec((D, tn), lambda i, j: (0, j)),
                  pl.BlockSpec((D, tn), lambda i, j: (0, j)),
                  pl.BlockSpec((tm, tn), lambda i, j: (i, j)),
                  pl.BlockSpec((tm, tn), lambda i, j: (i, nj + j))],
        out_specs=pl.BlockSpec((tm, tn), lambda i, j: (i, j)),
        out_shape=jax.ShapeDtypeStruct((M, N), BF16),
        compiler_params=_params(2),
        name="gated_merge",
    )(y_lru, y_att, w_l, w_a, gates, gates)


def _layer_norm_rows(v, g, b):
    mu = jnp.mean(v, axis=-1, keepdims=True)
    var = jnp.mean(jnp.square(v - mu), axis=-1, keepdims=True)
    return (v - mu) * lax.rsqrt(var + LN_EPS) * g + b


def _outproj_ln_kernel(m_ref, x_ref, w_ref, g_ref, b_ref, o32_ref, o16_ref):
    proj = jnp.dot(m_ref[...], w_ref[...], preferred_element_type=F32)
    h = _layer_norm_rows(ALPHA * x_ref[...] + proj, g_ref[...], b_ref[...])
    o32_ref[...] = h
    o16_ref[...] = h.astype(BF16)


def _outproj_ln(merged, x, w_out, g, b, ts):
    B, S, D = x.shape
    return pl.pallas_call(
        _outproj_ln_kernel,
        grid=(B, S // ts),
        in_specs=[pl.BlockSpec((ts, D), lambda b_, i: (i, b_)),
                  pl.BlockSpec((None, ts, D), lambda b_, i: (b_, i, 0)),
                  pl.BlockSpec((D, D), lambda b_, i: (0, 0)),
                  pl.BlockSpec((1, D), lambda b_, i: (0, 0)),
                  pl.BlockSpec((1, D), lambda b_, i: (0, 0))],
        out_specs=[pl.BlockSpec((ts, D), lambda b_, i: (i, b_)),
                   pl.BlockSpec((ts, D), lambda b_, i: (i, b_))],
        out_shape=[jax.ShapeDtypeStruct((S, B * D), F32),
                   jax.ShapeDtypeStruct((S, B * D), BF16)],
        compiler_params=_params(2),
        name="outproj_ln1",
    )(merged, x, w_out, g.reshape(1, D), b.reshape(1, D))


def _ffn_up_kernel(halo_ref, h_ref, wg_ref, wv_ref, cwg_ref, cwv_ref, cbg_ref, cbv_ref, o_ref,
                   *, tm, conv_w, batch):
    i = pl.program_id(0)
    keep = jnp.where(i > 0, 1.0, 0.0).astype(BF16)
    lhs = jnp.concatenate([halo_ref[...] * keep, h_ref[...]], axis=0)

    def conv_half(w_ref, cw_ref, cb_ref):
        up = jnp.dot(lhs, w_ref[...], preferred_element_type=F32)
        y = cb_ref[...] + cw_ref[0:1, :] * up[0:tm]
        for j in range(1, conv_w):
            y = y + cw_ref[j:j + 1, :] * up[j * batch:j * batch + tm]
        return y

    yg = conv_half(wg_ref, cwg_ref, cbg_ref)
    yv = conv_half(wv_ref, cwv_ref, cbv_ref)
    o_ref[...] = (_gelu_tanh(yg) * yv).astype(o_ref.dtype)


def _ffn_up(h16, w_up, conv_w, conv_b, B, tm, tn):
    M, D = h16.shape
    d_ff = w_up.shape[1] // 2
    width = conv_w.shape[0]
    halo_rows = (width - 1) * B
    nj = d_ff // tn
    kern = functools.partial(_ffn_up_kernel, tm=tm, conv_w=width, batch=B)
    blocks_per_tile = tm // halo_rows
    cb2 = conv_b.reshape(1, 2 * d_ff)
    return pl.pallas_call(
        kern,
        grid=(M // tm, nj),
        in_specs=[pl.BlockSpec((halo_rows, D), lambda i, j: (jnp.maximum(i * blocks_per_tile - 1, 0), 0)),
                  pl.BlockSpec((tm, D), lambda i, j: (i, 0)),
                  pl.BlockSpec((D, tn), lambda i, j: (0, j)),
                  pl.BlockSpec((D, tn), lambda i, j: (0, nj + j)),
                  pl.BlockSpec((width, tn), lambda i, j: (0, j)),
                  pl.BlockSpec((width, tn), lambda i, j: (0, nj + j)),
                  pl.BlockSpec((1, tn), lambda i, j: (0, j)),
                  pl.BlockSpec((1, tn), lambda i, j: (0, nj + j))],
        out_specs=pl.BlockSpec((tm, tn), lambda i, j: (i, j)),
        out_shape=jax.ShapeDtypeStruct((M, d_ff), BF16),
        compiler_params=_params(2),
        name="ffn_up_conv_geglu",
    )(h16, h16, w_up, w_up, conv_w, conv_w, cb2, cb2)


def _ffn_down_kernel(a_ref, h_ref, w_ref, g_ref, b_ref, o_ref, acc_ref):
    k = pl.program_id(2)

    @pl.when(k == 0)
    def _():
        acc_ref[...] = jnp.zeros(acc_ref.shape, F32)

    acc_ref[...] += jnp.dot(a_ref[...], w_ref[...], preferred_element_type=F32)

    @pl.when(k == pl.num_programs(2) - 1)
    def _():
        o_ref[...] = _layer_norm_rows(ALPHA * h_ref[...] + acc_ref[...], g_ref[...], b_ref[...])


def _ffn_down_ln(act, h32, w_down, g, b, B, S, ts, tk):
    d_ff, D = w_down.shape
    nk = d_ff // tk
    return pl.pallas_call(
        _ffn_down_kernel,
        grid=(B, S // ts, nk),
        in_specs=[pl.BlockSpec((ts, tk), lambda b_, i, k: (i, b_ * nk + k)),
                  pl.BlockSpec((ts, D), lambda b_, i, k: (i, b_)),
                  pl.BlockSpec((tk, D), lambda b_, i, k: (k, 0)),
                  pl.BlockSpec((1, D), lambda b_, i, k: (0, 0)),
                  pl.BlockSpec((1, D), lambda b_, i, k: (0, 0))],
        out_specs=pl.BlockSpec((None, ts, D), lambda b_, i, k: (b_, i, 0)),
        out_shape=jax.ShapeDtypeStruct((B, S, D), F32),
        scratch_shapes=[pltpu.VMEM((ts, D), F32)],
        compiler_params=_params(3),
        name="ffn_down_ln2",
    )(act, h32, w_down, g.reshape(1, D), b.reshape(1, D))


def _far_bucket():
    max_exact = REL_BUCKETS // 2
    large = max_exact + int(math.log(REL_MAX_DIST / max_exact) / math.log(REL_MAX_DIST / max_exact)
                            * (REL_BUCKETS - max_exact))
    return min(large, REL_BUCKETS - 1)


def kernel(x, w_in, lru_conv_w, lru_conv_b, lru_gate_a_w, lru_gate_a_b, lru_gate_x_w, lru_gate_x_b,
           lru_lambda, idx_knorm_g, idx_knorm_b, rel_bias, w_proj_lru, w_proj_attn, w_out,
           ln1_g, ln1_b, ffn_w_up, ffn_conv_w, ffn_conv_b, ffn_w_down, ln2_g, ln2_b):
    B, S, D = x.shape
    assert B == SUBLANES and S % CK == 0 and w_in.shape[0] == DEPTH
    top_k = min(TOPK_MAX, S // 4)
    d_rnn = lru_conv_w.shape[-1]
    q_w = N_HEADS * HEAD_DIM
    qi_w = IDX_HEADS * IDX_DIM
    splits = (d_rnn, d_rnn, q_w, HEAD_DIM, HEAD_DIM, qi_w, IDX_DIM, IDX_HEADS, D, D)
    offs = [0]
    for s_ in splits:
        offs.append(offs[-1] + s_)

    l = 0
    w = w_in[l].astype(BF16)
    x16 = x.astype(BF16)
    ts_in = min(S, 1024)
    w_lru = w[:, offs[0]:offs[2]]
    w_q = w[:, offs[2]:offs[3]]
    pad = LANES - IDX_DIM - IDX_HEADS
    w_kvw = jnp.concatenate([w[:, offs[3]:offs[5]], w[:, offs[6]:offs[8]],
                             jnp.zeros((D, pad), BF16)], axis=1)
    w_qi = w[:, offs[5]:offs[6]]
    w_g = w[:, offs[8]:offs[10]]

    lxg = _project_time_major(x16, w_lru, F32, ts_in, 1024).reshape(S, B, 2 * d_rnn)
    q = _project_time_major(x16, w_q, BF16, ts_in, 1024)
    kvw = _project_time_major(x16, w_kvw, F32, ts_in, w_kvw.shape[1])
    qi = _project_time_major(x16, w_qi, F32, ts_in, 1024)
    gates = _project_time_major(x16, w_g, F32, ts_in, 1024).reshape(S * B, 2 * D)

    y_lru = _rg_lru(lxg, lru_conv_w[l], lru_conv_b[l], lru_gate_a_w[l], lru_gate_a_b[l],
                    lru_gate_x_w[l], lru_gate_x_b[l], lru_lambda[l], S, B, tc=min(S, 256), cbw=256)

    tiles = _bias_tiles(rel_bias)
    far_bias = rel_bias[_far_bucket()]
    y_att = _sparse_attention(qi, kvw, q, tiles, far_bias, idx_knorm_g[l], idx_knorm_b[l],
                              S, B, top_k).reshape(S * B, q_w)

    merged = _gated_merge(y_lru, y_att, w_proj_lru[l].astype(BF16), w_proj_attn[l].astype(BF16),
                          gates, tm=512, tn=1024)
    h32, h16 = _outproj_ln(merged.reshape(S, B * D), x, w_out[l].astype(BF16),
                           ln1_g[l], ln1_b[l], ts=min(S, 512))
    act = _ffn_up(h16.reshape(S * B, D), ffn_w_up[l].astype(BF16), ffn_conv_w[l], ffn_conv_b[l],
                  B, tm=1024, tn=512)
    d_ff = ffn_w_down.shape[1]
    out = _ffn_down_ln(act.reshape(S, B * d_ff), h32, ffn_w_down[l].astype(BF16),
                       ln2_g[l], ln2_b[l], B, S, ts=min(S, 512), tk=2048)
    return out
```

```python
import functools
import math

import jax
import jax.numpy as jnp
from jax import lax
from jax.experimental import pallas as pl
from jax.experimental.pallas import tpu as pltpu

N_HEADS = 16
HEAD_DIM = 128
IDX_HEADS = 16
IDX_DIM = 64
TOPK_MAX = 256
LRU_BLOCK = 128
LRU_C = 8.0
REL_BUCKETS = 32
REL_MAX_DIST = 128
LN_EPS = 1e-5
DEPTH = 1
ALPHA = (2.0 * DEPTH) ** 0.25

LANES = 128
SUBLANES = 8
VMEM_LIMIT_BYTES = 56 * 1024 * 1024

TQ = 128
CK = 256
BIAS_TILE_STEP = 128
N_BIAS_TILES = 3

INT_MIN = -(2 ** 31)
MASK_LOGIT = -2e30
LOG2E = math.log2(math.e)
F32 = jnp.float32
BF16 = jnp.bfloat16


def _params(n_axes):
    return pltpu.CompilerParams(dimension_semantics=("arbitrary",) * n_axes,
                                vmem_limit_bytes=VMEM_LIMIT_BYTES)


def _matmul_kernel(x_ref, w_ref, o_ref):
    o_ref[...] = jnp.dot(x_ref[...], w_ref[...], preferred_element_type=F32).astype(o_ref.dtype)


def _project(x2d, w, out_dtype, tm, tn):
    M, K = x2d.shape
    N = w.shape[1]
    return pl.pallas_call(
        _matmul_kernel,
        grid=(M // tm, N // tn),
        in_specs=[pl.BlockSpec((tm, K), lambda i, j: (i, 0)),
                  pl.BlockSpec((K, tn), lambda i, j: (0, j))],
        out_specs=pl.BlockSpec((tm, tn), lambda i, j: (i, j)),
        out_shape=jax.ShapeDtypeStruct((M, N), out_dtype),
        compiler_params=_params(2),
        name="in_proj",
    )(x2d, w)


def _gelu_tanh(x):
    return 0.5 * x * (1.0 + jnp.tanh(math.sqrt(2.0 / math.pi) * (x + 0.044715 * (x * x * x))))


def _lru_kernel(lx_ref, lg_ref, cw_ref, cb_ref, wa_ref, ba_ref, wx_ref, bx_ref, lam_ref, o_ref,
                xs, a_s, u_s, h_s, *, tc, cb_width, conv_w):
    ti = pl.program_id(1)
    halo = conv_w - 1
    n_slab = cb_width // LANES
    rows = tc * SUBLANES
    slab = lambda v, s: v[:, s * LANES:(s + 1) * LANES]

    @pl.when(ti == 0)
    def _():
        xs[0:halo] = jnp.zeros((halo, SUBLANES, cb_width), F32)
        h_s[...] = jnp.zeros(h_s.shape, F32)

    xs[halo:halo + tc] = lx_ref[...]
    y = cb_ref[...] + cw_ref[0:1, :] * xs[0:tc]
    for j in range(1, conv_w):
        y = y + cw_ref[j:j + 1, :] * xs[j:j + tc]
    xs[0:halo] = xs[tc:tc + halo]

    y2 = y.reshape(rows, cb_width)
    yb = y2.astype(BF16)
    r_parts, i_parts = [], []
    for n in range(cb_width // LRU_BLOCK):
        blk = yb[:, n * LRU_BLOCK:(n + 1) * LRU_BLOCK]
        r_parts.append(jnp.dot(blk, wa_ref[n], preferred_element_type=F32))
        i_parts.append(jnp.dot(blk, wx_ref[n], preferred_element_type=F32))
    r = jax.nn.sigmoid(jnp.concatenate(r_parts, axis=1) + ba_ref[...])
    g_in = jax.nn.sigmoid(jnp.concatenate(i_parts, axis=1) + bx_ref[...])

    z = -lam_ref[...]
    softplus = jnp.maximum(z, 0.0) + jnp.log1p(jnp.exp(-jnp.abs(z)))
    log_a = (-LRU_C) * r * softplus
    a = jnp.exp(log_a)
    mult = jnp.sqrt(-jnp.tanh(log_a) * (a * a + 1.0))
    gx = g_in * y2
    u = mult * gx
    for s in range(n_slab):
        a_s[s] = slab(a, s)
        u_s[s] = slab(u, s)

    @pl.when(ti == 0)
    def _():
        for s in range(n_slab):
            u_s[s, 0:SUBLANES, :] = slab(gx, s)[0:SUBLANES]

    def step(t, h):
        r0 = pl.multiple_of(t * SUBLANES, SUBLANES)
        new = []
        for s in range(n_slab):
            hs = a_s[s, pl.ds(r0, SUBLANES), :] * h[s] + u_s[s, pl.ds(r0, SUBLANES), :]
            u_s[s, pl.ds(r0, SUBLANES), :] = hs
            new.append(hs)
        return tuple(new)

    h_fin = lax.fori_loop(0, tc, step, tuple(h_s[s] for s in range(n_slab)), unroll=8)
    gate = _gelu_tanh(lg_ref[...]).reshape(rows, cb_width)
    for s in range(n_slab):
        h_s[s] = h_fin[s]
        u_s[s] = slab(gate, s) * u_s[s]
    for b in range(SUBLANES):
        for s in range(n_slab):
            o_ref[b, :, s * LANES:(s + 1) * LANES] = (
                u_s[s, pl.ds(b, tc, stride=SUBLANES), :].astype(o_ref.dtype))


def _rg_lru(lxg, conv_w, conv_b, gate_a_w, gate_a_b, gate_x_w, gate_x_b, lam, S, B, tc, cbw):
    d_rnn = conv_w.shape[1]
    width = conv_w.shape[0]
    ncb = d_rnn // cbw
    nblk = cbw // LRU_BLOCK
    n_slab = cbw // LANES
    row = lambda v: v.reshape(1, d_rnn)
    kern = functools.partial(_lru_kernel, tc=tc, cb_width=cbw, conv_w=width)
    vec_spec = pl.BlockSpec((1, cbw), lambda n, t: (0, n))
    return pl.pallas_call(
        kern,
        grid=(ncb, S // tc),
        in_specs=[pl.BlockSpec((tc, B, cbw), lambda n, t: (t, 0, n)),
                  pl.BlockSpec((tc, B, cbw), lambda n, t: (t, 0, ncb + n)),
                  pl.BlockSpec((width, cbw), lambda n, t: (0, n)),
                  vec_spec,
                  pl.BlockSpec((nblk, LRU_BLOCK, LRU_BLOCK), lambda n, t: (n, 0, 0)),
                  vec_spec,
                  pl.BlockSpec((nblk, LRU_BLOCK, LRU_BLOCK), lambda n, t: (n, 0, 0)),
                  vec_spec,
                  vec_spec],
        out_specs=pl.BlockSpec((B, tc, cbw), lambda n, t: (0, t, n)),
        out_shape=jax.ShapeDtypeStruct((B, S, d_rnn), BF16),
        scratch_shapes=[pltpu.VMEM((tc + width - 1, B, cbw), F32),
                        pltpu.VMEM((n_slab, tc * B, LANES), F32),
                        pltpu.VMEM((n_slab, tc * B, LANES), F32),
                        pltpu.VMEM((n_slab, B, LANES), F32)],
        compiler_params=_params(2),
        name="rg_lru",
    )(lxg, lxg, conv_w, row(conv_b), gate_a_w.astype(BF16), row(gate_a_b),
      gate_x_w.astype(BF16), row(gate_x_b), row(lam))


def _bias_tile_kernel(rb_ref, o_ref):
    d = pl.program_id(0)
    h = pl.program_id(1)
    rows = lax.broadcasted_iota(jnp.int32, (TQ, CK), 0)
    cols = lax.broadcasted_iota(jnp.int32, (TQ, CK), 1)
    rel = d * BIAS_TILE_STEP + rows - cols
    max_exact = REL_BUCKETS // 2
    nf = jnp.maximum(rel, 1).astype(F32)
    large = max_exact + (jnp.log(nf / max_exact) / math.log(REL_MAX_DIST / max_exact)
                         * (REL_BUCKETS - max_exact)).astype(jnp.int32)
    large = jnp.minimum(large, REL_BUCKETS - 1)
    bucket = jnp.where(rel < max_exact, rel, large)
    acc = jnp.zeros((TQ, CK), F32)
    for b in range(REL_BUCKETS):
        acc = jnp.where(bucket == b, rb_ref[b, h], acc)
    o_ref[...] = acc * LOG2E


def _bias_tiles(rel_bias):
    return pl.pallas_call(
        _bias_tile_kernel,
        grid=(N_BIAS_TILES, N_HEADS),
        in_specs=[pl.BlockSpec(memory_space=pltpu.SMEM)],
        out_specs=pl.BlockSpec((None, None, TQ, CK), lambda d, h: (d, h, 0, 0)),
        out_shape=jax.ShapeDtypeStruct((N_BIAS_TILES, N_HEADS, TQ, CK), F32),
        compiler_params=_params(2),
        name="t5_bias_tiles",
    )(rel_bias)


def _ordered_key(score):
    bits = pltpu.bitcast(score, jnp.int32)
    return jnp.where(bits >= 0, bits, bits ^ jnp.int32(0x7FFFFFFF))


def _attn_kernel(far_ref, qi_ref, kvw_ref, q_ref, bias_ref, kg_ref, kb_ref, o_ref,
                 kin_t, k_t, v_aug, keys, qis, wbuf, qs, pbuf, m_s, accl_s, *, top_k, seq):
    i = pl.program_id(1)
    k_off, v_off, kw_off = 0, HEAD_DIM, 2 * HEAD_DIM
    n_all = seq // CK

    @pl.when(i == 0)
    def _():
        kraw = kvw_ref[:, kw_off:kw_off + IDX_DIM]
        mu = jnp.mean(kraw, axis=-1, keepdims=True)
        var = jnp.mean(jnp.square(kraw - mu), axis=-1, keepdims=True)
        kn = (kraw - mu) * lax.rsqrt(var + LN_EPS) * kg_ref[...] + kb_ref[...]
        kn = jnp.concatenate([kn, jnp.zeros((seq, LANES - IDX_DIM), F32)], axis=1)
        for c in range(n_all):
            kin_t[c] = kn[c * CK:(c + 1) * CK, :].T[0:IDX_DIM].astype(BF16)
            k_t[c] = kvw_ref[c * CK:(c + 1) * CK, k_off:k_off + HEAD_DIM].T.astype(BF16)
        v_aug[:, 0:HEAD_DIM] = kvw_ref[:, v_off:v_off + HEAD_DIM].astype(BF16)
        v_aug[:, HEAD_DIM:2 * HEAD_DIM] = jnp.ones((seq, HEAD_DIM), BF16)

    q0 = pl.multiple_of(i * TQ, TQ)
    n_chunks = (q0 + TQ + CK - 1) // CK
    rows = q0 + lax.broadcasted_iota(jnp.int32, (TQ, CK), 0)
    cols = lax.broadcasted_iota(jnp.int32, (TQ, CK), 1)

    qib = qi_ref[...].astype(BF16)
    wi = kvw_ref[pl.ds(q0, TQ), kw_off + IDX_DIM:kw_off + IDX_DIM + IDX_HEADS]
    wi = wi * ((IDX_DIM ** -0.5) * (IDX_HEADS ** -0.5))
    for h in range(IDX_HEADS):
        qis[h * TQ:(h + 1) * TQ, :] = qib[:, h * IDX_DIM:(h + 1) * IDX_DIM]
        wbuf[h] = jnp.broadcast_to(wi[:, h:h + 1], (TQ, LANES))

    def score_chunk(c, carry):
        d_all = jnp.dot(qis[...], kin_t[c], preferred_element_type=F32)
        acc = jnp.zeros((TQ, CK), F32)
        for h in range(IDX_HEADS):
            w_h = wbuf[h]
            acc = acc + jnp.concatenate([w_h] * (CK // LANES), axis=1) * jnp.maximum(
                d_all[h * TQ:(h + 1) * TQ], 0.0)
        keys[c] = jnp.where(c * CK + cols <= rows, _ordered_key(acc), jnp.int32(INT_MIN))
        return carry

    lax.fori_loop(0, n_chunks, score_chunk, 0)

    def bit_step(step, thr):
        cand = thr + lax.shift_left(jnp.int32(1), 31 - step)

        def count_chunk(c, cnt):
            return cnt + jnp.where(keys[c] >= cand, 1.0, 0.0)

        cnt = lax.fori_loop(0, n_chunks, count_chunk, jnp.zeros((TQ, CK), F32))
        total = jnp.sum(cnt, axis=1, keepdims=True)
        return jnp.where(total >= top_k, cand, thr)

    thr = lax.fori_loop(0, 32, bit_step, jnp.full((TQ, 1), INT_MIN, jnp.int32))
    thr = jnp.maximum(thr, jnp.int32(INT_MIN + 1))

    for h in range(N_HEADS):
        qs[h * TQ:(h + 1) * TQ, :] = q_ref[:, h * HEAD_DIM:(h + 1) * HEAD_DIM]
    m_s[...] = jnp.full(m_s.shape, 0.5 * MASK_LOGIT, F32)
    accl_s[...] = jnp.zeros(accl_s.shape, F32)
    scale2 = (HEAD_DIM ** -0.5) * LOG2E

    def attn_chunk(c, carry):
        s_all = jnp.dot(qs[...], k_t[c], preferred_element_type=F32)
        sel = keys[c] >= thr
        delta = q0 - c * CK
        near = delta < N_BIAS_TILES * BIAS_TILE_STEP
        tile = jnp.clip(delta // BIAS_TILE_STEP, 0, N_BIAS_TILES - 1)
        for h in range(N_HEADS):
            bias = jnp.where(near, bias_ref[tile, h], far_ref[h] * LOG2E)
            s = s_all[h * TQ:(h + 1) * TQ] * scale2 + bias
            s = jnp.where(sel, s, MASK_LOGIT)
            m_old = m_s[h]
            m_new = jnp.maximum(m_old, jnp.max(s, axis=1, keepdims=True))
            p = jnp.exp2(s - jnp.concatenate([m_new] * (CK // LANES), axis=1))
            alpha = jnp.exp2(m_old - m_new)
            accl_s[h] = jnp.concatenate([alpha, alpha], axis=1) * accl_s[h]
            m_s[h] = m_new
            pbuf[h * TQ:(h + 1) * TQ, :] = p.astype(BF16)
        c0 = pl.multiple_of(c * CK, CK)
        pv = jnp.dot(pbuf[...], v_aug[pl.ds(c0, CK), :], preferred_element_type=F32)
        accl_s[...] = accl_s[...] + pv.reshape(N_HEADS, TQ, 2 * HEAD_DIM)
        return carry

    lax.fori_loop(0, n_chunks, attn_chunk, 0)
    for h in range(N_HEADS):
        al = accl_s[h]
        o_ref[:, h * HEAD_DIM:(h + 1) * HEAD_DIM] = (
            al[:, 0:HEAD_DIM] / al[:, HEAD_DIM:2 * HEAD_DIM]).astype(o_ref.dtype)


def _sparse_attention(qi, kvw, q, bias_tiles, far_bias, knorm_g, knorm_b, top_k):
    B, S, q_w = q.shape
    qi_w = qi.shape[2]
    kvw_w = kvw.shape[2]
    kern = functools.partial(_attn_kernel, top_k=top_k, seq=S)
    return pl.pallas_call(
        kern,
        grid=(B, S // TQ),
        in_specs=[pl.BlockSpec(memory_space=pltpu.SMEM),
                  pl.BlockSpec((None, TQ, qi_w), lambda b, i: (b, i, 0)),
                  pl.BlockSpec((None, S, kvw_w), lambda b, i: (b, 0, 0)),
                  pl.BlockSpec((None, TQ, q_w), lambda b, i: (b, i, 0)),
                  pl.BlockSpec((N_BIAS_TILES, N_HEADS, TQ, CK), lambda b, i: (0, 0, 0, 0)),
                  pl.BlockSpec((1, IDX_DIM), lambda b, i: (0, 0)),
                  pl.BlockSpec((1, IDX_DIM), lambda b, i: (0, 0))],
        out_specs=pl.BlockSpec((None, TQ, q_w), lambda b, i: (b, i, 0)),
        out_shape=jax.ShapeDtypeStruct((B, S, q_w), BF16),
        scratch_shapes=[pltpu.VMEM((S // CK, IDX_DIM, CK), BF16),
                        pltpu.VMEM((S // CK, HEAD_DIM, CK), BF16),
                        pltpu.VMEM((S, 2 * HEAD_DIM), BF16),
                        pltpu.VMEM((S // CK, TQ, CK), jnp.int32),
                        pltpu.VMEM((IDX_HEADS * TQ, IDX_DIM), BF16),
                        pltpu.VMEM((IDX_HEADS, TQ, LANES), F32),
                        pltpu.VMEM((N_HEADS * TQ, HEAD_DIM), BF16),
                        pltpu.VMEM((N_HEADS * TQ, CK), BF16),
                        pltpu.VMEM((N_HEADS, TQ, LANES), F32),
                        pltpu.VMEM((N_HEADS, TQ, 2 * HEAD_DIM), F32)],
        compiler_params=_params(2),
        name="dsa_attention",
    )(far_bias, qi, kvw, q, bias_tiles, knorm_g.reshape(1, IDX_DIM), knorm_b.reshape(1, IDX_DIM))


def _merge_kernel(yl_ref, ya_ref, wl_ref, wa_ref, gl_ref, ga_ref, o_ref):
    pl_ = jnp.dot(yl_ref[...], wl_ref[...], preferred_element_type=F32)
    pa_ = jnp.dot(ya_ref[...], wa_ref[...], preferred_element_type=F32)
    merged = jax.nn.sigmoid(gl_ref[...]) * pl_ + jax.nn.sigmoid(ga_ref[...]) * pa_
    o_ref[...] = merged.astype(o_ref.dtype)


def _gated_merge(y_lru, y_att, w_l, w_a, gates, tm, tn):
    M, D = y_lru.shape
    N = w_l.shape[1]
    nj = N // tn
    return pl.pallas_call(
        _merge_kernel,
        grid=(M // tm, nj),
        in_specs=[pl.BlockSpec((tm, D), lambda i, j: (i, 0)),
                  pl.BlockSpec((tm, D), lambda i, j: (i, 0)),
                  pl.BlockSpec((D, tn), lambda i, j: (0, j)),
                  pl.BlockSpec((D, tn), lambda i, j: (0, j)),
                  pl.BlockSpec((tm, tn), lambda i, j: (i, j)),
                  pl.BlockSpec((tm, tn), lambda i, j: (i, nj + j))],
        out_specs=pl.BlockSpec((tm, tn), lambda i, j: (i, j)),
        out_shape=jax.ShapeDtypeStruct((M, N), BF16),
        compiler_params=_params(2),
        name="gated_merge",
    )(y_lru, y_att, w_l, w_a, gates, gates)


def _layer_norm_rows(v, g, b):
    mu = jnp.mean(v, axis=-1, keepdims=True)
    var = jnp.mean(jnp.square(v - mu), axis=-1, keepdims=True)
    return (v - mu) * lax.rsqrt(var + LN_EPS) * g + b


def _outproj_ln_kernel(m_ref, x_ref, w_ref, g_ref, b_ref, o32_ref, o16_ref):
    proj = jnp.dot(m_ref[...], w_ref[...], preferred_element_type=F32)
    h = _layer_norm_rows(ALPHA * x_ref[...] + proj, g_ref[...], b_ref[...])
    o32_ref[...] = h
    o16_ref[...] = h.astype(BF16)


def _outproj_ln(merged, x2d, w_out, g, b, tm):
    M, D = x2d.shape
    return pl.pallas_call(
        _outproj_ln_kernel,
        grid=(M // tm,),
        in_specs=[pl.BlockSpec((tm, D), lambda i: (i, 0)),
                  pl.BlockSpec((tm, D), lambda i: (i, 0)),
                  pl.BlockSpec((D, D), lambda i: (0, 0)),
                  pl.BlockSpec((1, D), lambda i: (0, 0)),
                  pl.BlockSpec((1, D), lambda i: (0, 0))],
        out_specs=[pl.BlockSpec((tm, D), lambda i: (i, 0)),
                   pl.BlockSpec((tm, D), lambda i: (i, 0))],
        out_shape=[jax.ShapeDtypeStruct((M, D), F32),
                   jax.ShapeDtypeStruct((M, D), BF16)],
        compiler_params=_params(1),
        name="outproj_ln1",
    )(merged, x2d, w_out, g.reshape(1, D), b.reshape(1, D))


def _ffn_up_kernel(halo_ref, h_ref, wg_ref, wv_ref, cwg_ref, cwv_ref, cbg_ref, cbv_ref, o_ref,
                   *, tm, conv_w, tiles_per_seq):
    i = pl.program_id(0)
    keep = jnp.where(i % tiles_per_seq > 0, 1.0, 0.0).astype(BF16)
    lhs = jnp.concatenate([halo_ref[...] * keep, h_ref[...]], axis=0)
    base = SUBLANES - (conv_w - 1)

    def conv_half(w_ref, cw_ref, cb_ref):
        up = jnp.dot(lhs, w_ref[...], preferred_element_type=F32)
        y = cb_ref[...] + cw_ref[0:1, :] * up[base:base + tm]
        for j in range(1, conv_w):
            y = y + cw_ref[j:j + 1, :] * up[base + j:base + j + tm]
        return y

    yg = conv_half(wg_ref, cwg_ref, cbg_ref)
    yv = conv_half(wv_ref, cwv_ref, cbv_ref)
    o_ref[...] = (_gelu_tanh(yg) * yv).astype(o_ref.dtype)


def _ffn_up(h16, w_up, conv_w, conv_b, S, tm, tn):
    M, D = h16.shape
    d_ff = w_up.shape[1] // 2
    width = conv_w.shape[0]
    assert width - 1 <= SUBLANES and S % tm == 0
    nj = d_ff // tn
    kern = functools.partial(_ffn_up_kernel, tm=tm, conv_w=width, tiles_per_seq=S // tm)
    halo_blocks = tm // SUBLANES
    cb2 = conv_b.reshape(1, 2 * d_ff)
    return pl.pallas_call(
        kern,
        grid=(M // tm, nj),
        in_specs=[pl.BlockSpec((SUBLANES, D), lambda i, j: (jnp.maximum(i * halo_blocks - 1, 0), 0)),
                  pl.BlockSpec((tm, D), lambda i, j: (i, 0)),
                  pl.BlockSpec((D, tn), lambda i, j: (0, j)),
                  pl.BlockSpec((D, tn), lambda i, j: (0, nj + j)),
                  pl.BlockSpec((width, tn), lambda i, j: (0, j)),
                  pl.BlockSpec((width, tn), lambda i, j: (0, nj + j)),
                  pl.BlockSpec((1, tn), lambda i, j: (0, j)),
                  pl.BlockSpec((1, tn), lambda i, j: (0, nj + j))],
        out_specs=pl.BlockSpec((tm, tn), lambda i, j: (i, j)),
        out_shape=jax.ShapeDtypeStruct((M, d_ff), BF16),
        compiler_params=_params(2),
        name="ffn_up_conv_geglu",
    )(h16, h16, w_up, w_up, conv_w, conv_w, cb2, cb2)


def _ffn_down_kernel(a_ref, h_ref, w_ref, g_ref, b_ref, o_ref, acc_ref):
    k = pl.program_id(1)

    @pl.when(k == 0)
    def _():
        acc_ref[...] = jnp.zeros(acc_ref.shape, F32)

    acc_ref[...] += jnp.dot(a_ref[...], w_ref[...], preferred_element_type=F32)

    @pl.when(k == pl.num_programs(1) - 1)
    def _():
        o_ref[...] = _layer_norm_rows(ALPHA * h_ref[...] + acc_ref[...], g_ref[...], b_ref[...])


def _ffn_down_ln(act, h32, w_down, g, b, tm, tk):
    d_ff, D = w_down.shape
    M = act.shape[0]
    return pl.pallas_call(
        _ffn_down_kernel,
        grid=(M // tm, d_ff // tk),
        in_specs=[pl.BlockSpec((tm, tk), lambda i, k: (i, k)),
                  pl.BlockSpec((tm, D), lambda i, k: (i, 0)),
                  pl.BlockSpec((tk, D), lambda i, k: (k, 0)),
                  pl.BlockSpec((1, D), lambda i, k: (0, 0)),
                  pl.BlockSpec((1, D), lambda i, k: (0, 0))],
        out_specs=pl.BlockSpec((tm, D), lambda i, k: (i, 0)),
        out_shape=jax.ShapeDtypeStruct((M, D), F32),
        scratch_shapes=[pltpu.VMEM((tm, D), F32)],
        compiler_params=_params(2),
        name="ffn_down_ln2",
    )(act, h32, w_down, g.reshape(1, D), b.reshape(1, D))


def _far_bucket():
    max_exact = REL_BUCKETS // 2
    large = max_exact + int(math.log(REL_MAX_DIST / max_exact) / math.log(REL_MAX_DIST / max_exact)
                            * (REL_BUCKETS - max_exact))
    return min(large, REL_BUCKETS - 1)


def kernel(x, w_in, lru_conv_w, lru_conv_b, lru_gate_a_w, lru_gate_a_b, lru_gate_x_w, lru_gate_x_b,
           lru_lambda, idx_knorm_g, idx_knorm_b, rel_bias, w_proj_lru, w_proj_attn, w_out,
           ln1_g, ln1_b, ffn_w_up, ffn_conv_w, ffn_conv_b, ffn_w_down, ln2_g, ln2_b):
    B, S, D = x.shape
    assert B == SUBLANES and S % CK == 0 and w_in.shape[0] == DEPTH
    top_k = min(TOPK_MAX, S // 4)
    d_rnn = lru_conv_w.shape[-1]
    q_w = N_HEADS * HEAD_DIM
    qi_w = IDX_HEADS * IDX_DIM
    splits = (d_rnn, d_rnn, q_w, HEAD_DIM, HEAD_DIM, qi_w, IDX_DIM, IDX_HEADS, D, D)
    offs = [0]
    for s_ in splits:
        offs.append(offs[-1] + s_)
    M = B * S

    l = 0
    w = w_in[l].astype(BF16)
    x2d = x.reshape(M, D)
    x16 = x2d.astype(BF16)
    x16_tm = jnp.transpose(x, (1, 0, 2)).astype(BF16).reshape(M, D)
    w_lru = w[:, offs[0]:offs[2]]
    w_q = w[:, offs[2]:offs[3]]
    pad = LANES - IDX_DIM - IDX_HEADS
    w_kvw = jnp.concatenate([w[:, offs[3]:offs[5]], w[:, offs[6]:offs[8]],
                             jnp.zeros((D, pad), BF16)], axis=1)
    w_qi = w[:, offs[5]:offs[6]]
    w_g = w[:, offs[8]:offs[10]]

    tm_in = min(M, 1024)
    lxg = _project(x16_tm, w_lru, F32, tm_in, 1024).reshape(S, B, 2 * d_rnn)
    q = _project(x16, w_q, BF16, tm_in, 1024).reshape(B, S, q_w)
    kvw = _project(x16, w_kvw, F32, tm_in, w_kvw.shape[1]).reshape(B, S, w_kvw.shape[1])
    qi = _project(x16, w_qi, F32, tm_in, 1024).reshape(B, S, qi_w)
    gates = _project(x16, w_g, F32, tm_in, 1024)

    y_lru = _rg_lru(lxg, lru_conv_w[l], lru_conv_b[l], lru_gate_a_w[l], lru_gate_a_b[l],
                    lru_gate_x_w[l], lru_gate_x_b[l], lru_lambda[l], S, B, tc=min(S, 256), cbw=256)

    tiles = _bias_tiles(rel_bias)
    far_bias = rel_bias[_far_bucket()]
    y_att = _sparse_attention(qi, kvw, q, tiles, far_bias, idx_knorm_g[l], idx_knorm_b[l], top_k)

    merged = _gated_merge(y_lru.reshape(M, d_rnn), y_att.reshape(M, q_w),
                          w_proj_lru[l].astype(BF16), w_proj_attn[l].astype(BF16),
                          gates, tm=512, tn=1024)
    h32, h16 = _outproj_ln(merged, x2d, w_out[l].astype(BF16), ln1_g[l], ln1_b[l], tm=512)
    act = _ffn_up(h16, ffn_w_up[l].astype(BF16), ffn_conv_w[l], ffn_conv_b[l], S,
                  tm=min(S, 1024), tn=512)
    out = _ffn_down_ln(act, h32, ffn_w_down[l].astype(BF16), ln2_g[l], ln2_b[l], tm=512, tk=2048)
    return out.reshape(B, S, D)
```

```python
import functools
import math

import jax
import jax.numpy as jnp
from jax import lax
from jax.experimental import pallas as pl
from jax.experimental.pallas import tpu as pltpu

N_HEADS = 16
HEAD_DIM = 128
IDX_HEADS = 16
IDX_DIM = 64
TOPK_MAX = 256
LRU_BLOCK = 128
LRU_C = 8.0
REL_BUCKETS = 32
REL_MAX_DIST = 128
LN_EPS = 1e-5
DEPTH = 1
ALPHA = (2.0 * DEPTH) ** 0.25

LANES = 128
SUBLANES = 8
VMEM_LIMIT_BYTES = 56 * 1024 * 1024

TQ = 128
CK = 256
BIAS_TILE_STEP = 128
N_BIAS_TILES = 3

INT_MIN = -(2 ** 31)
MASK_LOGIT = -2e30
LOG2E = math.log2(math.e)
F32 = jnp.float32
BF16 = jnp.bfloat16


def _params(n_axes):
    return pltpu.CompilerParams(dimension_semantics=("arbitrary",) * n_axes,
                                vmem_limit_bytes=VMEM_LIMIT_BYTES)


def _matmul_kernel(x_ref, w_ref, o_ref, *, out_scale):
    acc = jnp.dot(x_ref[...], w_ref[...], preferred_element_type=F32)
    if out_scale != 1.0:
        acc = acc * out_scale
    o_ref[...] = acc.astype(o_ref.dtype)


def _project(x2d, w, out_dtype, tm, tn, out_scale=1.0):
    M, K = x2d.shape
    N = w.shape[1]
    return pl.pallas_call(
        functools.partial(_matmul_kernel, out_scale=out_scale),
        grid=(M // tm, N // tn),
        in_specs=[pl.BlockSpec((tm, K), lambda i, j: (i, 0)),
                  pl.BlockSpec((K, tn), lambda i, j: (0, j))],
        out_specs=pl.BlockSpec((tm, tn), lambda i, j: (i, j)),
        out_shape=jax.ShapeDtypeStruct((M, N), out_dtype),
        compiler_params=_params(2),
        name="in_proj",
    )(x2d, w)


def _gelu_tanh(x):
    return 0.5 * x * (1.0 + jnp.tanh(math.sqrt(2.0 / math.pi) * (x + 0.044715 * (x * x * x))))


def _sigmoid(x):
    return 0.5 * jnp.tanh(0.5 * x) + 0.5


def _lru_kernel(lx_ref, lg_ref, cw_ref, cb_ref, wa_ref, ba_ref, wx_ref, bx_ref, lam_ref, o_ref,
                xs, a_s, u_s, h_s, *, tc, cb_width, conv_w):
    ti = pl.program_id(1)
    halo = conv_w - 1
    n_slab = cb_width // LANES
    rows = tc * SUBLANES
    slab = lambda v, s: v[:, s * LANES:(s + 1) * LANES]

    @pl.when(ti == 0)
    def _():
        xs[0:halo] = jnp.zeros((halo, SUBLANES, cb_width), F32)
        h_s[...] = jnp.zeros(h_s.shape, F32)

    xs[halo:halo + tc] = lx_ref[...]
    y = cb_ref[...] + cw_ref[0:1, :] * xs[0:tc]
    for j in range(1, conv_w):
        y = y + cw_ref[j:j + 1, :] * xs[j:j + tc]
    xs[0:halo] = xs[tc:tc + halo]

    y2 = y.reshape(rows, cb_width)
    yb = y2.astype(BF16)
    r_parts, i_parts = [], []
    for n in range(cb_width // LRU_BLOCK):
        blk = yb[:, n * LRU_BLOCK:(n + 1) * LRU_BLOCK]
        r_parts.append(jnp.dot(blk, wa_ref[n], preferred_element_type=F32))
        i_parts.append(jnp.dot(blk, wx_ref[n], preferred_element_type=F32))
    r = _sigmoid(jnp.concatenate(r_parts, axis=1) + ba_ref[...])
    g_in = _sigmoid(jnp.concatenate(i_parts, axis=1) + bx_ref[...])

    z = -lam_ref[...]
    softplus = jnp.maximum(z, 0.0) + jnp.log1p(jnp.exp(-jnp.abs(z)))
    log_a = (-LRU_C) * r * softplus
    a = jnp.exp(log_a)
    mult = jnp.sqrt(-jnp.tanh(log_a) * (a * a + 1.0))
    gx = g_in * y2
    u = mult * gx
    for s in range(n_slab):
        a_s[s] = slab(a, s)
        u_s[s] = slab(u, s)

    @pl.when(ti == 0)
    def _():
        for s in range(n_slab):
            u_s[s, 0:SUBLANES, :] = slab(gx, s)[0:SUBLANES]

    def step(t, h):
        r0 = pl.multiple_of(t * SUBLANES, SUBLANES)
        new = []
        for s in range(n_slab):
            hs = a_s[s, pl.ds(r0, SUBLANES), :] * h[s] + u_s[s, pl.ds(r0, SUBLANES), :]
            u_s[s, pl.ds(r0, SUBLANES), :] = hs
            new.append(hs)
        return tuple(new)

    h_fin = lax.fori_loop(0, tc, step, tuple(h_s[s] for s in range(n_slab)), unroll=8)
    gate = _gelu_tanh(lg_ref[...]).reshape(rows, cb_width)
    for s in range(n_slab):
        h_s[s] = h_fin[s]
        u_s[s] = slab(gate, s) * u_s[s]
    for b in range(SUBLANES):
        for s in range(n_slab):
            o_ref[b, :, s * LANES:(s + 1) * LANES] = (
                u_s[s, pl.ds(b, tc, stride=SUBLANES), :].astype(o_ref.dtype))


def _rg_lru(lxg, conv_w, conv_b, gate_a_w, gate_a_b, gate_x_w, gate_x_b, lam, S, B, tc, cbw):
    d_rnn = conv_w.shape[1]
    width = conv_w.shape[0]
    ncb = d_rnn // cbw
    nblk = cbw // LRU_BLOCK
    n_slab = cbw // LANES
    row = lambda v: v.reshape(1, d_rnn)
    kern = functools.partial(_lru_kernel, tc=tc, cb_width=cbw, conv_w=width)
    vec_spec = pl.BlockSpec((1, cbw), lambda n, t: (0, n))
    return pl.pallas_call(
        kern,
        grid=(ncb, S // tc),
        in_specs=[pl.BlockSpec((tc, B, cbw), lambda n, t: (t, 0, n)),
                  pl.BlockSpec((tc, B, cbw), lambda n, t: (t, 0, ncb + n)),
                  pl.BlockSpec((width, cbw), lambda n, t: (0, n)),
                  vec_spec,
                  pl.BlockSpec((nblk, LRU_BLOCK, LRU_BLOCK), lambda n, t: (n, 0, 0)),
                  vec_spec,
                  pl.BlockSpec((nblk, LRU_BLOCK, LRU_BLOCK), lambda n, t: (n, 0, 0)),
                  vec_spec,
                  vec_spec],
        out_specs=pl.BlockSpec((B, tc, cbw), lambda n, t: (0, t, n)),
        out_shape=jax.ShapeDtypeStruct((B, S, d_rnn), BF16),
        scratch_shapes=[pltpu.VMEM((tc + width - 1, B, cbw), F32),
                        pltpu.VMEM((n_slab, tc * B, LANES), F32),
                        pltpu.VMEM((n_slab, tc * B, LANES), F32),
                        pltpu.VMEM((n_slab, B, LANES), F32)],
        compiler_params=_params(2),
        name="rg_lru",
    )(lxg, lxg, conv_w, row(conv_b), gate_a_w.astype(BF16), row(gate_a_b),
      gate_x_w.astype(BF16), row(gate_x_b), row(lam))


def _far_bucket():
    max_exact = REL_BUCKETS // 2
    large = max_exact + int(math.log(REL_MAX_DIST / max_exact) / math.log(REL_MAX_DIST / max_exact)
                            * (REL_BUCKETS - max_exact))
    return min(large, REL_BUCKETS - 1)


def _bias_tile_kernel(rb_ref, o_ref):
    d = pl.program_id(0)
    h = pl.program_id(1)
    rows = lax.broadcasted_iota(jnp.int32, (TQ, CK), 0)
    cols = lax.broadcasted_iota(jnp.int32, (TQ, CK), 1)
    rel = d * BIAS_TILE_STEP + rows - cols
    max_exact = REL_BUCKETS // 2
    nf = jnp.maximum(rel, 1).astype(F32)
    large = max_exact + (jnp.log(nf / max_exact) / math.log(REL_MAX_DIST / max_exact)
                         * (REL_BUCKETS - max_exact)).astype(jnp.int32)
    large = jnp.minimum(large, REL_BUCKETS - 1)
    bucket = jnp.where(rel < max_exact, rel, large)
    acc = jnp.zeros((TQ, CK), F32)
    for b in range(REL_BUCKETS):
        acc = jnp.where(bucket == b, rb_ref[b, h], acc)
    o_ref[...] = (acc - rb_ref[_far_bucket(), h]) * LOG2E


def _bias_tiles(rel_bias):
    return pl.pallas_call(
        _bias_tile_kernel,
        grid=(N_BIAS_TILES, N_HEADS),
        in_specs=[pl.BlockSpec(memory_space=pltpu.SMEM)],
        out_specs=pl.BlockSpec((None, None, TQ, CK), lambda d, h: (d, h, 0, 0)),
        out_shape=jax.ShapeDtypeStruct((N_BIAS_TILES, N_HEADS, TQ, CK), F32),
        compiler_params=_params(2),
        name="t5_bias_tiles",
    )(rel_bias)


I16_MIN = -(2 ** 15)
I16_MAX = 2 ** 15 - 1


def _ordered_key(score):
    bits = pltpu.bitcast(score, jnp.int32)
    return jnp.where(bits >= 0, bits, bits ^ jnp.int32(0x7FFFFFFF))


def _count_ge(arr, n_pairs, cand):
    cand16 = jnp.broadcast_to(cand, (TQ, LANES)).astype(jnp.int16)
    cand16 = jnp.concatenate([cand16] * (CK // LANES), axis=1)
    one, zero = jnp.int16(1), jnp.int16(0)

    def count_pair(p, cnt):
        cnt = cnt + jnp.where(arr[p, 0] >= cand16, one, zero)
        return cnt + jnp.where(arr[p, 1] >= cand16, one, zero)

    cnt = lax.fori_loop(0, n_pairs, count_pair, jnp.zeros((TQ, CK), jnp.int16))
    return jnp.sum(cnt.astype(jnp.int32).astype(F32), axis=1, keepdims=True)


def _kth_largest_i16(arr, n_pairs, need):
    def bit_step(step, thr):
        cand = thr + lax.shift_left(jnp.int32(1), 15 - step)
        return jnp.where(_count_ge(arr, n_pairs, cand) >= need, cand, thr)

    return lax.fori_loop(0, 16, bit_step, jnp.full((TQ, 1), I16_MIN, jnp.int32))


def _attn_kernel(far_ref, qi_ref, kvw_ref, q_ref, bias_ref, kg_ref, kb_ref, o_ref,
                 kin_t, k_t, v_aug, keys, hi_s, lo_s, qis, wbuf, qs, pbuf, m_s, accl_s,
                 *, top_k, seq):
    i = pl.program_id(1)
    k_off, v_off, kw_off = 0, HEAD_DIM, 2 * HEAD_DIM
    n_all = seq // CK
    lane = lax.broadcasted_iota(jnp.int32, (TQ, LANES), 1)

    @pl.when(i == 0)
    def _():
        kraw = kvw_ref[:, kw_off:kw_off + IDX_DIM]
        mu = jnp.mean(kraw, axis=-1, keepdims=True)
        var = jnp.mean(jnp.square(kraw - mu), axis=-1, keepdims=True)
        kn = (kraw - mu) * lax.rsqrt(var + LN_EPS) * kg_ref[...] + kb_ref[...]
        kn = jnp.concatenate([kn, jnp.zeros((seq, LANES - IDX_DIM), F32)], axis=1)
        sub = lax.broadcasted_iota(jnp.int32, (HEAD_DIM, CK), 0)
        ones_rows = jnp.where(sub < 2, 1.0, 0.0).astype(BF16)
        for c in range(n_all):
            kin_t[c] = kn[c * CK:(c + 1) * CK, :].T[0:IDX_DIM].astype(BF16)
            k_t[c, 0:HEAD_DIM, :] = kvw_ref[c * CK:(c + 1) * CK, k_off:k_off + HEAD_DIM].T.astype(BF16)
            k_t[c, HEAD_DIM:2 * HEAD_DIM, :] = ones_rows
        v_aug[:, 0:HEAD_DIM] = kvw_ref[:, v_off:v_off + HEAD_DIM].astype(BF16)
        v_aug[:, HEAD_DIM:2 * HEAD_DIM] = jnp.ones((seq, HEAD_DIM), BF16)
        for h in range(N_HEADS):
            fb = jnp.full((TQ, LANES), far_ref[h] * LOG2E, F32)
            fb_hi = fb.astype(BF16).astype(F32)
            qs[h * TQ:(h + 1) * TQ, HEAD_DIM:2 * HEAD_DIM] = jnp.where(
                lane == 0, fb_hi, jnp.where(lane == 1, fb - fb_hi, 0.0)).astype(BF16)

    q0 = pl.multiple_of(i * TQ, TQ)
    n_chunks = (q0 + TQ + CK - 1) // CK
    n_pairs = (n_chunks + 1) // 2
    rows = q0 + lax.broadcasted_iota(jnp.int32, (TQ, CK), 0)
    cols = lax.broadcasted_iota(jnp.int32, (TQ, CK), 1)

    qib = qi_ref[...].astype(BF16)
    wi = kvw_ref[pl.ds(q0, TQ), kw_off + IDX_DIM:kw_off + IDX_DIM + IDX_HEADS]
    wi = wi * ((IDX_DIM ** -0.5) * (IDX_HEADS ** -0.5))
    for h in range(IDX_HEADS):
        qis[h * TQ:(h + 1) * TQ, :] = qib[:, h * IDX_DIM:(h + 1) * IDX_DIM]
        wbuf[h] = jnp.broadcast_to(wi[:, h:h + 1], (TQ, LANES))

    def score_chunk(c, carry):
        d_all = jnp.dot(qis[...], kin_t[c], preferred_element_type=F32)
        acc = jnp.zeros((TQ, CK), F32)
        for h in range(IDX_HEADS):
            w_h = wbuf[h]
            acc = acc + jnp.concatenate([w_h] * (CK // LANES), axis=1) * jnp.maximum(
                d_all[h * TQ:(h + 1) * TQ], 0.0)
        key = jnp.where(c * CK + cols <= rows, _ordered_key(acc), jnp.int32(INT_MIN))
        keys[c] = key
        hi_s[c // 2, c % 2] = lax.shift_right_arithmetic(key, 16).astype(jnp.int16)
        lo_s[c // 2, c % 2] = ((key & 0xFFFF) + I16_MIN).astype(jnp.int16)
        return carry

    lax.fori_loop(0, n_chunks, score_chunk, 0)

    @pl.when(n_chunks % 2 == 1)
    def _():
        hi_s[n_chunks // 2, 1] = jnp.full((TQ, CK), I16_MIN, jnp.int16)
        lo_s[n_chunks // 2, 1] = jnp.full((TQ, CK), I16_MIN, jnp.int16)

    need = jnp.full((TQ, 1), float(top_k), F32)
    hi_k = _kth_largest_i16(hi_s, n_pairs, need)
    above = jnp.where(hi_k == I16_MAX, 0.0,
                      _count_ge(hi_s, n_pairs, jnp.minimum(hi_k + 1, I16_MAX)))
    hi_k16 = jnp.broadcast_to(hi_k, (TQ, LANES)).astype(jnp.int16)
    hi_k16 = jnp.concatenate([hi_k16] * (CK // LANES), axis=1)

    def keep_group(p, carry):
        for j in range(2):
            lo_s[p, j] = jnp.where(hi_s[p, j] == hi_k16, lo_s[p, j], jnp.int16(I16_MIN))
        return carry

    lax.fori_loop(0, n_pairs, keep_group, 0)
    lo_k = _kth_largest_i16(lo_s, n_pairs, need - above)
    thr = hi_k * 65536 + (lo_k - I16_MIN)
    thr = jnp.maximum(thr, jnp.int32(INT_MIN + 1))

    for h in range(N_HEADS):
        qs[h * TQ:(h + 1) * TQ, 0:HEAD_DIM] = q_ref[:, h * HEAD_DIM:(h + 1) * HEAD_DIM]
    m_s[...] = jnp.full(m_s.shape, 0.5 * MASK_LOGIT, F32)
    accl_s[...] = jnp.zeros(accl_s.shape, F32)

    def attn_chunk(c, carry, *, near):
        s_all = jnp.dot(qs[...], k_t[c], preferred_element_type=F32)
        sel = keys[c] >= thr
        tile = (q0 - c * CK) // BIAS_TILE_STEP
        for h in range(N_HEADS):
            s = s_all[h * TQ:(h + 1) * TQ]
            if near:
                s = s + bias_ref[tile, h]
            s = jnp.where(sel, s, MASK_LOGIT)
            m_old = m_s[h]
            m_new = jnp.maximum(m_old, jnp.max(s, axis=1, keepdims=True))
            p = jnp.exp2(s - jnp.concatenate([m_new] * (CK // LANES), axis=1))
            alpha = jnp.exp2(m_old - m_new)
            accl_s[h] = jnp.concatenate([alpha, alpha], axis=1) * accl_s[h]
            m_s[h] = m_new
            pbuf[h * TQ:(h + 1) * TQ, :] = p.astype(BF16)
        c0 = pl.multiple_of(c * CK, CK)
        pv = jnp.dot(pbuf[...], v_aug[pl.ds(c0, CK), :], preferred_element_type=F32)
        accl_s[...] = accl_s[...] + pv.reshape(N_HEADS, TQ, 2 * HEAD_DIM)
        return carry

    n_near = jnp.where((q0 // BIAS_TILE_STEP) % 2 == 0, 2, 1)
    n_far = jnp.maximum(n_chunks - n_near, 0)
    lax.fori_loop(0, n_far, functools.partial(attn_chunk, near=False), 0)
    lax.fori_loop(n_far, n_chunks, functools.partial(attn_chunk, near=True), 0)
    for h in range(N_HEADS):
        al = accl_s[h]
        o_ref[:, h * HEAD_DIM:(h + 1) * HEAD_DIM] = (
            al[:, 0:HEAD_DIM] / al[:, HEAD_DIM:2 * HEAD_DIM]).astype(o_ref.dtype)


def _sparse_attention(qi, kvw, q, bias_tiles, far_bias, knorm_g, knorm_b, top_k):
    B, S, q_w = q.shape
    qi_w = qi.shape[2]
    kvw_w = kvw.shape[2]
    n_all = S // CK
    assert n_all % 2 == 0
    kern = functools.partial(_attn_kernel, top_k=top_k, seq=S)
    return pl.pallas_call(
        kern,
        grid=(B, S // TQ),
        in_specs=[pl.BlockSpec(memory_space=pltpu.SMEM),
                  pl.BlockSpec((None, TQ, qi_w), lambda b, i: (b, i, 0)),
                  pl.BlockSpec((None, S, kvw_w), lambda b, i: (b, 0, 0)),
                  pl.BlockSpec((None, TQ, q_w), lambda b, i: (b, i, 0)),
                  pl.BlockSpec((N_BIAS_TILES, N_HEADS, TQ, CK), lambda b, i: (0, 0, 0, 0)),
                  pl.BlockSpec((1, IDX_DIM), lambda b, i: (0, 0)),
                  pl.BlockSpec((1, IDX_DIM), lambda b, i: (0, 0))],
        out_specs=pl.BlockSpec((None, TQ, q_w), lambda b, i: (b, i, 0)),
        out_shape=jax.ShapeDtypeStruct((B, S, q_w), BF16),
        scratch_shapes=[pltpu.VMEM((n_all, IDX_DIM, CK), BF16),
                        pltpu.VMEM((n_all, 2 * HEAD_DIM, CK), BF16),
                        pltpu.VMEM((S, 2 * HEAD_DIM), BF16),
                        pltpu.VMEM((n_all, TQ, CK), jnp.int32),
                        pltpu.VMEM((n_all // 2, 2, TQ, CK), jnp.int16),
                        pltpu.VMEM((n_all // 2, 2, TQ, CK), jnp.int16),
                        pltpu.VMEM((IDX_HEADS * TQ, IDX_DIM), BF16),
                        pltpu.VMEM((IDX_HEADS, TQ, LANES), F32),
                        pltpu.VMEM((N_HEADS * TQ, 2 * HEAD_DIM), BF16),
                        pltpu.VMEM((N_HEADS * TQ, CK), BF16),
                        pltpu.VMEM((N_HEADS, TQ, LANES), F32),
                        pltpu.VMEM((N_HEADS, TQ, 2 * HEAD_DIM), F32)],
        compiler_params=_params(2),
        name="dsa_attention",
    )(far_bias, qi, kvw, q, bias_tiles, knorm_g.reshape(1, IDX_DIM), knorm_b.reshape(1, IDX_DIM))


def _merge_kernel(yl_ref, ya_ref, wl_ref, wa_ref, gl_ref, ga_ref, o_ref):
    pl_ = jnp.dot(yl_ref[...], wl_ref[...], preferred_element_type=F32)
    pa_ = jnp.dot(ya_ref[...], wa_ref[...], preferred_element_type=F32)
    merged = _sigmoid(gl_ref[...]) * pl_ + _sigmoid(ga_ref[...]) * pa_
    o_ref[...] = merged.astype(o_ref.dtype)


def _gated_merge(y_lru, y_att, w_l, w_a, gates, tm, tn):
    M, D = y_lru.shape
    N = w_l.shape[1]
    nj = N // tn
    return pl.pallas_call(
        _merge_kernel,
        grid=(M // tm, nj),
        in_specs=[pl.BlockSpec((tm, D), lambda i, j: (i, 0)),
                  pl.BlockSpec((tm, D), lambda i, j: (i, 0)),
                  pl.BlockSpec((D, tn), lambda i, j: (0, j)),
                  pl.BlockSpec((D, tn), lambda i, j: (0, j)),
                  pl.BlockSpec((tm, tn), lambda i, j: (i, j)),
                  pl.BlockSpec((tm, tn), lambda i, j: (i, nj + j))],
        out_specs=pl.BlockSpec((tm, tn), lambda i, j: (i, j)),
        out_shape=jax.ShapeDtypeStruct((M, N), BF16),
        compiler_params=_params(2),
        name="gated_merge",
    )(y_lru, y_att, w_l, w_a, gates, gates)


def _layer_norm_rows(v, g, b):
    mu = jnp.mean(v, axis=-1, keepdims=True)
    var = jnp.mean(jnp.square(v - mu), axis=-1, keepdims=True)
    return (v - mu) * lax.rsqrt(var + LN_EPS) * g + b


def _outproj_ln_kernel(m_ref, x_ref, w_ref, g_ref, b_ref, o32_ref, o16_ref):
    proj = jnp.dot(m_ref[...], w_ref[...], preferred_element_type=F32)
    h = _layer_norm_rows(ALPHA * x_ref[...] + proj, g_ref[...], b_ref[...])
    o32_ref[...] = h
    o16_ref[...] = h.astype(BF16)


def _outproj_ln(merged, x2d, w_out, g, b, tm):
    M, D = x2d.shape
    return pl.pallas_call(
        _outproj_ln_kernel,
        grid=(M // tm,),
        in_specs=[pl.BlockSpec((tm, D), lambda i: (i, 0)),
                  pl.BlockSpec((tm, D), lambda i: (i, 0)),
                  pl.BlockSpec((D, D), lambda i: (0, 0)),
                  pl.BlockSpec((1, D), lambda i: (0, 0)),
                  pl.BlockSpec((1, D), lambda i: (0, 0))],
        out_specs=[pl.BlockSpec((tm, D), lambda i: (i, 0)),
                   pl.BlockSpec((tm, D), lambda i: (i, 0))],
        out_shape=[jax.ShapeDtypeStruct((M, D), F32),
                   jax.ShapeDtypeStruct((M, D), BF16)],
        compiler_params=_params(1),
        name="outproj_ln1",
    )(merged, x2d, w_out, g.reshape(1, D), b.reshape(1, D))


def _ffn_up_kernel(halo_ref, h_ref, wg_ref, wv_ref, cwg_ref, cwv_ref, cbg_ref, cbv_ref, o_ref,
                   *, tm, conv_w, tiles_per_seq):
    i = pl.program_id(0)
    keep = jnp.where(i % tiles_per_seq > 0, 1.0, 0.0).astype(BF16)
    lhs = jnp.concatenate([halo_ref[...] * keep, h_ref[...]], axis=0)
    base = SUBLANES - (conv_w - 1)

    def conv_half(w_ref, cw_ref, cb_ref):
        up = jnp.dot(lhs, w_ref[...], preferred_element_type=F32)
        y = cb_ref[...] + cw_ref[0:1, :] * up[base:base + tm]
        for j in range(1, conv_w):
            y = y + cw_ref[j:j + 1, :] * up[base + j:base + j + tm]
        return y

    yg = conv_half(wg_ref, cwg_ref, cbg_ref)
    yv = conv_half(wv_ref, cwv_ref, cbv_ref)
    o_ref[...] = (_gelu_tanh(yg) * yv).astype(o_ref.dtype)


def _ffn_up(h16, w_up, conv_w, conv_b, S, tm, tn):
    M, D = h16.shape
    d_ff = w_up.shape[1] // 2
    width = conv_w.shape[0]
    assert width - 1 <= SUBLANES and S % tm == 0
    nj = d_ff // tn
    kern = functools.partial(_ffn_up_kernel, tm=tm, conv_w=width, tiles_per_seq=S // tm)
    halo_blocks = tm // SUBLANES
    cb2 = conv_b.reshape(1, 2 * d_ff)
    return pl.pallas_call(
        kern,
        grid=(M // tm, nj),
        in_specs=[pl.BlockSpec((SUBLANES, D), lambda i, j: (jnp.maximum(i * halo_blocks - 1, 0), 0)),
                  pl.BlockSpec((tm, D), lambda i, j: (i, 0)),
                  pl.BlockSpec((D, tn), lambda i, j: (0, j)),
                  pl.BlockSpec((D, tn), lambda i, j: (0, nj + j)),
                  pl.BlockSpec((width, tn), lambda i, j: (0, j)),
                  pl.BlockSpec((width, tn), lambda i, j: (0, nj + j)),
                  pl.BlockSpec((1, tn), lambda i, j: (0, j)),
                  pl.BlockSpec((1, tn), lambda i, j: (0, nj + j))],
        out_specs=pl.BlockSpec((tm, tn), lambda i, j: (i, j)),
        out_shape=jax.ShapeDtypeStruct((M, d_ff), BF16),
        compiler_params=_params(2),
        name="ffn_up_conv_geglu",
    )(h16, h16, w_up, w_up, conv_w, conv_w, cb2, cb2)


def _ffn_down_kernel(a_ref, h_ref, w_ref, g_ref, b_ref, o_ref, acc_ref):
    k = pl.program_id(1)

    @pl.when(k == 0)
    def _():
        acc_ref[...] = jnp.zeros(acc_ref.shape, F32)

    acc_ref[...] += jnp.dot(a_ref[...], w_ref[...], preferred_element_type=F32)

    @pl.when(k == pl.num_programs(1) - 1)
    def _():
        o_ref[...] = _layer_norm_rows(ALPHA * h_ref[...] + acc_ref[...], g_ref[...], b_ref[...])


def _ffn_down_ln(act, h32, w_down, g, b, tm, tk):
    d_ff, D = w_down.shape
    M = act.shape[0]
    return pl.pallas_call(
        _ffn_down_kernel,
        grid=(M // tm, d_ff // tk),
        in_specs=[pl.BlockSpec((tm, tk), lambda i, k: (i, k)),
                  pl.BlockSpec((tm, D), lambda i, k: (i, 0)),
                  pl.BlockSpec((tk, D), lambda i, k: (k, 0)),
                  pl.BlockSpec((1, D), lambda i, k: (0, 0)),
                  pl.BlockSpec((1, D), lambda i, k: (0, 0))],
        out_specs=pl.BlockSpec((tm, D), lambda i, k: (i, 0)),
        out_shape=jax.ShapeDtypeStruct((M, D), F32),
        scratch_shapes=[pltpu.VMEM((tm, D), F32)],
        compiler_params=_params(2),
        name="ffn_down_ln2",
    )(act, h32, w_down, g.reshape(1, D), b.reshape(1, D))


def kernel(x, w_in, lru_conv_w, lru_conv_b, lru_gate_a_w, lru_gate_a_b, lru_gate_x_w, lru_gate_x_b,
           lru_lambda, idx_knorm_g, idx_knorm_b, rel_bias, w_proj_lru, w_proj_attn, w_out,
           ln1_g, ln1_b, ffn_w_up, ffn_conv_w, ffn_conv_b, ffn_w_down, ln2_g, ln2_b):
    B, S, D = x.shape
    assert B == SUBLANES and S % CK == 0 and w_in.shape[0] == DEPTH
    top_k = min(TOPK_MAX, S // 4)
    d_rnn = lru_conv_w.shape[-1]
    q_w = N_HEADS * HEAD_DIM
    qi_w = IDX_HEADS * IDX_DIM
    splits = (d_rnn, d_rnn, q_w, HEAD_DIM, HEAD_DIM, qi_w, IDX_DIM, IDX_HEADS, D, D)
    offs = [0]
    for s_ in splits:
        offs.append(offs[-1] + s_)
    M = B * S

    l = 0
    w = w_in[l].astype(BF16)
    x2d = x.reshape(M, D)
    x16 = x2d.astype(BF16)
    x16_tm = jnp.transpose(x, (1, 0, 2)).astype(BF16).reshape(M, D)
    w_lru = w[:, offs[0]:offs[2]]
    w_q = w[:, offs[2]:offs[3]]
    pad = LANES - IDX_DIM - IDX_HEADS
    w_kvw = jnp.concatenate([w[:, offs[3]:offs[5]], w[:, offs[6]:offs[8]],
                             jnp.zeros((D, pad), BF16)], axis=1)
    w_qi = w[:, offs[5]:offs[6]]
    w_g = w[:, offs[8]:offs[10]]

    tm_in = min(M, 1024)
    lxg = _project(x16_tm, w_lru, F32, tm_in, 1024).reshape(S, B, 2 * d_rnn)
    q = _project(x16, w_q, BF16, tm_in, 1024,
                 out_scale=(HEAD_DIM ** -0.5) * LOG2E).reshape(B, S, q_w)
    kvw = _project(x16, w_kvw, F32, tm_in, w_kvw.shape[1]).reshape(B, S, w_kvw.shape[1])
    qi = _project(x16, w_qi, F32, tm_in, 1024).reshape(B, S, qi_w)
    gates = _project(x16, w_g, F32, tm_in, 1024)

    y_lru = _rg_lru(lxg, lru_conv_w[l], lru_conv_b[l], lru_gate_a_w[l], lru_gate_a_b[l],
                    lru_gate_x_w[l], lru_gate_x_b[l], lru_lambda[l], S, B, tc=min(S, 256), cbw=256)

    tiles = _bias_tiles(rel_bias)
    far_bias = rel_bias[_far_bucket()]
    y_att = _sparse_attention(qi, kvw, q, tiles, far_bias, idx_knorm_g[l], idx_knorm_b[l], top_k)

    merged = _gated_merge(y_lru.reshape(M, d_rnn), y_att.reshape(M, q_w),
                          w_proj_lru[l].astype(BF16), w_proj_attn[l].astype(BF16),
                          gates, tm=512, tn=1024)
    h32, h16 = _outproj_ln(merged, x2d, w_out[l].astype(BF16), ln1_g[l], ln1_b[l], tm=512)
    act = _ffn_up(h16, ffn_w_up[l].astype(BF16), ffn_conv_w[l], ffn_conv_b[l], S,
                  tm=min(S, 1024), tn=512)
    out = _ffn_down_ln(act, h32, ffn_w_down[l].astype(BF16), ln2_g[l], ln2_b[l], tm=512, tk=2048)
    return out.reshape(B, S, D)
```

```python
import functools
import math

import jax
import jax.numpy as jnp
from jax import lax
from jax.experimental import pallas as pl
from jax.experimental.pallas import tpu as pltpu

N_HEADS = 16
HEAD_DIM = 128
IDX_HEADS = 16
IDX_DIM = 64
TOPK_MAX = 256
LRU_BLOCK = 128
LRU_C = 8.0
REL_BUCKETS = 32
REL_MAX_DIST = 128
LN_EPS = 1e-5
DEPTH = 1
ALPHA = (2.0 * DEPTH) ** 0.25

LANES = 128
SUBLANES = 8
MXU_COLS = 256
VMEM_LIMIT_BYTES = 56 * 1024 * 1024

TQ = 256
CK = 256
HEAD_GROUP = 4
BIAS_TILE_STEP = 256
N_BIAS_TILES = 2
assert TQ % BIAS_TILE_STEP == 0 and CK % BIAS_TILE_STEP == 0
assert N_BIAS_TILES * BIAS_TILE_STEP - (CK - 1) >= REL_MAX_DIST

INT_MIN = -(2 ** 31)
MASK_LOGIT = -2e30
LOG2E = math.log2(math.e)
F32 = jnp.float32
BF16 = jnp.bfloat16


def _params(n_axes):
    return pltpu.CompilerParams(dimension_semantics=("arbitrary",) * n_axes,
                                vmem_limit_bytes=VMEM_LIMIT_BYTES)


def _matmul_kernel(x_ref, w_ref, o_ref, *, out_scale):
    acc = jnp.dot(x_ref[...], w_ref[...], preferred_element_type=F32)
    if out_scale != 1.0:
        acc = acc * out_scale
    o_ref[...] = acc.astype(o_ref.dtype)


def _project(x2d, w, out_dtype, tm, tn, out_scale=1.0):
    M, K = x2d.shape
    N = w.shape[1]
    return pl.pallas_call(
        functools.partial(_matmul_kernel, out_scale=out_scale),
        grid=(M // tm, N // tn),
        in_specs=[pl.BlockSpec((tm, K), lambda i, j: (i, 0)),
                  pl.BlockSpec((K, tn), lambda i, j: (0, j))],
        out_specs=pl.BlockSpec((tm, tn), lambda i, j: (i, j)),
        out_shape=jax.ShapeDtypeStruct((M, N), out_dtype),
        compiler_params=_params(2),
        name="in_proj",
    )(x2d, w)


def _gelu_tanh(x):
    return 0.5 * x * (1.0 + jnp.tanh(math.sqrt(2.0 / math.pi) * (x + 0.044715 * (x * x * x))))


def _sigmoid(x):
    return 0.5 * jnp.tanh(0.5 * x) + 0.5


def _lru_kernel(lx_ref, lg_ref, cw_ref, cb_ref, wa_ref, ba_ref, wx_ref, bx_ref, lam_ref, o_ref,
                xs, a_s, u_s, h_s, *, tc, cb_width, conv_w):
    ti = pl.program_id(1)
    halo = conv_w - 1
    n_slab = cb_width // LANES
    rows = tc * SUBLANES
    slab = lambda v, s: v[:, s * LANES:(s + 1) * LANES]

    @pl.when(ti == 0)
    def _():
        xs[0:halo] = jnp.zeros((halo, SUBLANES, cb_width), F32)
        h_s[...] = jnp.zeros(h_s.shape, F32)

    xs[halo:halo + tc] = lx_ref[...]
    y = cb_ref[...] + cw_ref[0:1, :] * xs[0:tc]
    for j in range(1, conv_w):
        y = y + cw_ref[j:j + 1, :] * xs[j:j + tc]
    xs[0:halo] = xs[tc:tc + halo]

    y2 = y.reshape(rows, cb_width)
    yb = y2.astype(BF16)
    r_parts, i_parts = [], []
    for n in range(cb_width // LRU_BLOCK):
        blk = yb[:, n * LRU_BLOCK:(n + 1) * LRU_BLOCK]
        r_parts.append(jnp.dot(blk, wa_ref[n], preferred_element_type=F32))
        i_parts.append(jnp.dot(blk, wx_ref[n], preferred_element_type=F32))
    r = _sigmoid(jnp.concatenate(r_parts, axis=1) + ba_ref[...])
    g_in = _sigmoid(jnp.concatenate(i_parts, axis=1) + bx_ref[...])

    z = -lam_ref[...]
    softplus = jnp.maximum(z, 0.0) + jnp.log1p(jnp.exp(-jnp.abs(z)))
    log_a = (-LRU_C) * r * softplus
    a = jnp.exp(log_a)
    mult = jnp.sqrt(-jnp.tanh(log_a) * (a * a + 1.0))
    gx = g_in * y2
    u = mult * gx
    for s in range(n_slab):
        a_s[s] = slab(a, s)
        u_s[s] = slab(u, s)

    @pl.when(ti == 0)
    def _():
        for s in range(n_slab):
            u_s[s, 0:SUBLANES, :] = slab(gx, s)[0:SUBLANES]

    def step(t, h):
        r0 = pl.multiple_of(t * SUBLANES, SUBLANES)
        new = []
        for s in range(n_slab):
            hs = a_s[s, pl.ds(r0, SUBLANES), :] * h[s] + u_s[s, pl.ds(r0, SUBLANES), :]
            u_s[s, pl.ds(r0, SUBLANES), :] = hs
            new.append(hs)
        return tuple(new)

    h_fin = lax.fori_loop(0, tc, step, tuple(h_s[s] for s in range(n_slab)), unroll=8)
    gate = _gelu_tanh(lg_ref[...]).reshape(rows, cb_width)
    for s in range(n_slab):
        h_s[s] = h_fin[s]
        u_s[s] = slab(gate, s) * u_s[s]
    for b in range(SUBLANES):
        for s in range(n_slab):
            o_ref[b, :, s * LANES:(s + 1) * LANES] = (
                u_s[s, pl.ds(b, tc, stride=SUBLANES), :].astype(o_ref.dtype))


def _rg_lru(lxg, conv_w, conv_b, gate_a_w, gate_a_b, gate_x_w, gate_x_b, lam, S, B, tc, cbw):
    d_rnn = conv_w.shape[1]
    width = conv_w.shape[0]
    ncb = d_rnn // cbw
    nblk = cbw // LRU_BLOCK
    n_slab = cbw // LANES
    row = lambda v: v.reshape(1, d_rnn)
    kern = functools.partial(_lru_kernel, tc=tc, cb_width=cbw, conv_w=width)
    vec_spec = pl.BlockSpec((1, cbw), lambda n, t: (0, n))
    return pl.pallas_call(
        kern,
        grid=(ncb, S // tc),
        in_specs=[pl.BlockSpec((tc, B, cbw), lambda n, t: (t, 0, n)),
                  pl.BlockSpec((tc, B, cbw), lambda n, t: (t, 0, ncb + n)),
                  pl.BlockSpec((width, cbw), lambda n, t: (0, n)),
                  vec_spec,
                  pl.BlockSpec((nblk, LRU_BLOCK, LRU_BLOCK), lambda n, t: (n, 0, 0)),
                  vec_spec,
                  pl.BlockSpec((nblk, LRU_BLOCK, LRU_BLOCK), lambda n, t: (n, 0, 0)),
                  vec_spec,
                  vec_spec],
        out_specs=pl.BlockSpec((B, tc, cbw), lambda n, t: (0, t, n)),
        out_shape=jax.ShapeDtypeStruct((B, S, d_rnn), BF16),
        scratch_shapes=[pltpu.VMEM((tc + width - 1, B, cbw), F32),
                        pltpu.VMEM((n_slab, tc * B, LANES), F32),
                        pltpu.VMEM((n_slab, tc * B, LANES), F32),
                        pltpu.VMEM((n_slab, B, LANES), F32)],
        compiler_params=_params(2),
        name="rg_lru",
    )(lxg, lxg, conv_w, row(conv_b), gate_a_w.astype(BF16), row(gate_a_b),
      gate_x_w.astype(BF16), row(gate_x_b), row(lam))


def _far_bucket():
    max_exact = REL_BUCKETS // 2
    large = max_exact + int(math.log(REL_MAX_DIST / max_exact) / math.log(REL_MAX_DIST / max_exact)
                            * (REL_BUCKETS - max_exact))
    return min(large, REL_BUCKETS - 1)


def _bias_tile_kernel(rb_ref, o_ref):
    d = pl.program_id(0)
    h = pl.program_id(1)
    rows = lax.broadcasted_iota(jnp.int32, (TQ, CK), 0)
    cols = lax.broadcasted_iota(jnp.int32, (TQ, CK), 1)
    rel = d * BIAS_TILE_STEP + rows - cols
    max_exact = REL_BUCKETS // 2
    nf = jnp.maximum(rel, 1).astype(F32)
    large = max_exact + (jnp.log(nf / max_exact) / math.log(REL_MAX_DIST / max_exact)
                         * (REL_BUCKETS - max_exact)).astype(jnp.int32)
    large = jnp.minimum(large, REL_BUCKETS - 1)
    bucket = jnp.where(rel < max_exact, rel, large)
    acc = jnp.zeros((TQ, CK), F32)
    for b in range(REL_BUCKETS):
        acc = jnp.where(bucket == b, rb_ref[b, h], acc)
    o_ref[...] = (acc - rb_ref[_far_bucket(), h]) * LOG2E


def _bias_tiles(rel_bias):
    return pl.pallas_call(
        _bias_tile_kernel,
        grid=(N_BIAS_TILES, N_HEADS),
        in_specs=[pl.BlockSpec(memory_space=pltpu.SMEM)],
        out_specs=pl.BlockSpec((None, None, TQ, CK), lambda d, h: (d, h, 0, 0)),
        out_shape=jax.ShapeDtypeStruct((N_BIAS_TILES, N_HEADS, TQ, CK), F32),
        compiler_params=_params(2),
        name="t5_bias_tiles",
    )(rel_bias)


I16_MIN = -(2 ** 15)
I16_MAX = 2 ** 15 - 1


def _ordered_key(score):
    bits = pltpu.bitcast(score, jnp.int32)
    return jnp.where(bits >= 0, bits, bits ^ jnp.int32(0x7FFFFFFF))


def _count_ge(arr, n_pairs, cand):
    cand16 = jnp.broadcast_to(cand, (TQ, LANES)).astype(jnp.int16)
    cand16 = jnp.concatenate([cand16] * (CK // LANES), axis=1)
    one, zero = jnp.int16(1), jnp.int16(0)

    def count_pair(p, cnt):
        cnt = cnt + jnp.where(arr[p, 0] >= cand16, one, zero)
        return cnt + jnp.where(arr[p, 1] >= cand16, one, zero)

    cnt = lax.fori_loop(0, n_pairs, count_pair, jnp.zeros((TQ, CK), jnp.int16))
    return jnp.sum(cnt.astype(jnp.int32).astype(F32), axis=1, keepdims=True)


def _kth_largest_i16(arr, n_pairs, need):
    def bit_step(step, thr):
        cand = thr + lax.shift_left(jnp.int32(1), 15 - step)
        return jnp.where(_count_ge(arr, n_pairs, cand) >= need, cand, thr)

    return lax.fori_loop(0, 16, bit_step, jnp.full((TQ, 1), I16_MIN, jnp.int32))


def _attn_kernel(far_ref, qi_ref, kvw_ref, q_ref, bias_ref, kg_ref, kb_ref, o_ref,
                 kin_t, k_t, v_aug, keys, hi_s, lo_s, qis, wbuf, qs, pbuf, m_s, accl_s,
                 *, top_k, seq):
    i = pl.program_id(1)
    k_off, v_off, kw_off = 0, HEAD_DIM, 2 * HEAD_DIM
    n_all = seq // CK
    lane = lax.broadcasted_iota(jnp.int32, (TQ, LANES), 1)

    @pl.when(i == 0)
    def _():
        kraw = kvw_ref[:, kw_off:kw_off + IDX_DIM]
        mu = jnp.mean(kraw, axis=-1, keepdims=True)
        var = jnp.mean(jnp.square(kraw - mu), axis=-1, keepdims=True)
        kn = (kraw - mu) * lax.rsqrt(var + LN_EPS) * kg_ref[...] + kb_ref[...]
        kn = jnp.concatenate([kn, jnp.zeros((seq, LANES - IDX_DIM), F32)], axis=1)
        sub = lax.broadcasted_iota(jnp.int32, (HEAD_DIM, CK), 0)
        ones_rows = jnp.where(sub < 2, 1.0, 0.0).astype(BF16)
        for c in range(n_all):
            kin_t[c] = kn[c * CK:(c + 1) * CK, :].T[0:IDX_DIM].astype(BF16)
            k_t[c, 0:HEAD_DIM, :] = kvw_ref[c * CK:(c + 1) * CK, k_off:k_off + HEAD_DIM].T.astype(BF16)
            k_t[c, HEAD_DIM:2 * HEAD_DIM, :] = ones_rows
        v_aug[:, 0:HEAD_DIM] = kvw_ref[:, v_off:v_off + HEAD_DIM].astype(BF16)
        v_aug[:, HEAD_DIM:2 * HEAD_DIM] = jnp.ones((seq, HEAD_DIM), BF16)
        for h in range(N_HEADS):
            fb = jnp.full((TQ, LANES), far_ref[h] * LOG2E, F32)
            fb_hi = fb.astype(BF16).astype(F32)
            qs[h * TQ:(h + 1) * TQ, HEAD_DIM:2 * HEAD_DIM] = jnp.where(
                lane == 0, fb_hi, jnp.where(lane == 1, fb - fb_hi, 0.0)).astype(BF16)

    q0 = pl.multiple_of(i * TQ, TQ)
    n_chunks = (q0 + TQ + CK - 1) // CK
    n_pairs = (n_chunks + 1) // 2
    rows = q0 + lax.broadcasted_iota(jnp.int32, (TQ, CK), 0)
    cols = lax.broadcasted_iota(jnp.int32, (TQ, CK), 1)

    qib = qi_ref[...].astype(BF16)
    wi = kvw_ref[pl.ds(q0, TQ), kw_off + IDX_DIM:kw_off + IDX_DIM + IDX_HEADS]
    wi = wi * ((IDX_DIM ** -0.5) * (IDX_HEADS ** -0.5))
    for h in range(IDX_HEADS):
        qis[h * TQ:(h + 1) * TQ, :] = qib[:, h * IDX_DIM:(h + 1) * IDX_DIM]
        wbuf[h] = jnp.broadcast_to(wi[:, h:h + 1], (TQ, LANES))

    def score_chunk(c, carry):
        kc = kin_t[c]
        acc = jnp.zeros((TQ, CK), F32)
        for g in range(IDX_HEADS // HEAD_GROUP):
            g0 = g * HEAD_GROUP * TQ
            d_g = jnp.dot(qis[g0:g0 + HEAD_GROUP * TQ, :], kc, preferred_element_type=F32)
            for hh in range(HEAD_GROUP):
                w_h = wbuf[g * HEAD_GROUP + hh]
                acc = acc + jnp.concatenate([w_h] * (CK // LANES), axis=1) * jnp.maximum(
                    d_g[hh * TQ:(hh + 1) * TQ], 0.0)
        key = jnp.where(c * CK + cols <= rows, _ordered_key(acc), jnp.int32(INT_MIN))
        keys[c] = key
        hi_s[c // 2, c % 2] = lax.shift_right_arithmetic(key, 16).astype(jnp.int16)
        lo_s[c // 2, c % 2] = ((key & 0xFFFF) + I16_MIN).astype(jnp.int16)
        return carry

    lax.fori_loop(0, n_chunks, score_chunk, 0)

    @pl.when(n_chunks % 2 == 1)
    def _():
        hi_s[n_chunks // 2, 1] = jnp.full((TQ, CK), I16_MIN, jnp.int16)
        lo_s[n_chunks // 2, 1] = jnp.full((TQ, CK), I16_MIN, jnp.int16)

    need = jnp.full((TQ, 1), float(top_k), F32)
    hi_k = _kth_largest_i16(hi_s, n_pairs, need)
    above = jnp.where(hi_k == I16_MAX, 0.0,
                      _count_ge(hi_s, n_pairs, jnp.minimum(hi_k + 1, I16_MAX)))
    hi_k16 = jnp.broadcast_to(hi_k, (TQ, LANES)).astype(jnp.int16)
    hi_k16 = jnp.concatenate([hi_k16] * (CK // LANES), axis=1)

    def keep_group(p, carry):
        for j in range(2):
            lo_s[p, j] = jnp.where(hi_s[p, j] == hi_k16, lo_s[p, j], jnp.int16(I16_MIN))
        return carry

    lax.fori_loop(0, n_pairs, keep_group, 0)
    lo_k = _kth_largest_i16(lo_s, n_pairs, need - above)
    thr = hi_k * 65536 + (lo_k - I16_MIN)
    thr = jnp.maximum(thr, jnp.int32(INT_MIN + 1))

    for h in range(N_HEADS):
        qs[h * TQ:(h + 1) * TQ, 0:HEAD_DIM] = q_ref[:, h * HEAD_DIM:(h + 1) * HEAD_DIM]
    m_s[...] = jnp.full(m_s.shape, 0.5 * MASK_LOGIT, F32)
    accl_s[...] = jnp.zeros(accl_s.shape, F32)

    def attn_chunk(c, carry, *, near):
        kc = k_t[c]
        vc = v_aug[pl.ds(pl.multiple_of(c * CK, CK), CK), :]
        sel = keys[c] >= thr
        tile = (q0 - c * CK) // BIAS_TILE_STEP
        for g in range(N_HEADS // HEAD_GROUP):
            g0 = g * HEAD_GROUP * TQ
            s_g = jnp.dot(qs[g0:g0 + HEAD_GROUP * TQ, :], kc, preferred_element_type=F32)
            alphas = []
            for hh in range(HEAD_GROUP):
                h = g * HEAD_GROUP + hh
                s = s_g[hh * TQ:(hh + 1) * TQ]
                if near:
                    s = s + bias_ref[tile, h]
                s = jnp.where(sel, s, MASK_LOGIT)
                m_old = m_s[h]
                m_new = jnp.maximum(m_old, jnp.max(s, axis=1, keepdims=True))
                p = jnp.exp2(s - jnp.concatenate([m_new] * (CK // LANES), axis=1))
                alphas.append(jnp.exp2(m_old - m_new))
                m_s[h] = m_new
                pbuf[h * TQ:(h + 1) * TQ, :] = p.astype(BF16)
            pv = jnp.dot(pbuf[g0:g0 + HEAD_GROUP * TQ, :], vc, preferred_element_type=F32)
            for hh in range(HEAD_GROUP):
                h = g * HEAD_GROUP + hh
                accl_s[h] = (jnp.concatenate([alphas[hh], alphas[hh]], axis=1) * accl_s[h]
                             + pv[hh * TQ:(hh + 1) * TQ])
        return carry

    n_far = jnp.maximum((q0 - N_BIAS_TILES * BIAS_TILE_STEP) // CK + 1, 0)
    lax.fori_loop(0, n_far, functools.partial(attn_chunk, near=False), 0)
    lax.fori_loop(n_far, n_chunks, functools.partial(attn_chunk, near=True), 0)
    for h in range(N_HEADS):
        al = accl_s[h]
        o_ref[:, h * HEAD_DIM:(h + 1) * HEAD_DIM] = (
            al[:, 0:HEAD_DIM] / al[:, HEAD_DIM:2 * HEAD_DIM]).astype(o_ref.dtype)


def _sparse_attention(qi, kvw, q, bias_tiles, far_bias, knorm_g, knorm_b, top_k):
    B, S, q_w = q.shape
    qi_w = qi.shape[2]
    kvw_w = kvw.shape[2]
    n_all = S // CK
    assert n_all % 2 == 0
    kern = functools.partial(_attn_kernel, top_k=top_k, seq=S)
    return pl.pallas_call(
        kern,
        grid=(B, S // TQ),
        in_specs=[pl.BlockSpec(memory_space=pltpu.SMEM),
                  pl.BlockSpec((None, TQ, qi_w), lambda b, i: (b, i, 0)),
                  pl.BlockSpec((None, S, kvw_w), lambda b, i: (b, 0, 0)),
                  pl.BlockSpec((None, TQ, q_w), lambda b, i: (b, i, 0)),
                  pl.BlockSpec((N_BIAS_TILES, N_HEADS, TQ, CK), lambda b, i: (0, 0, 0, 0),
                               pipeline_mode=pl.Buffered(1)),
                  pl.BlockSpec((1, IDX_DIM), lambda b, i: (0, 0)),
                  pl.BlockSpec((1, IDX_DIM), lambda b, i: (0, 0))],
        out_specs=pl.BlockSpec((None, TQ, q_w), lambda b, i: (b, i, 0)),
        out_shape=jax.ShapeDtypeStruct((B, S, q_w), BF16),
        scratch_shapes=[pltpu.VMEM((n_all, IDX_DIM, CK), BF16),
                        pltpu.VMEM((n_all, 2 * HEAD_DIM, CK), BF16),
                        pltpu.VMEM((S, 2 * HEAD_DIM), BF16),
                        pltpu.VMEM((n_all, TQ, CK), jnp.int32),
                        pltpu.VMEM((n_all // 2, 2, TQ, CK), jnp.int16),
                        pltpu.VMEM((n_all // 2, 2, TQ, CK), jnp.int16),
                        pltpu.VMEM((IDX_HEADS * TQ, IDX_DIM), BF16),
                        pltpu.VMEM((IDX_HEADS, TQ, LANES), F32),
                        pltpu.VMEM((N_HEADS * TQ, 2 * HEAD_DIM), BF16),
                        pltpu.VMEM((N_HEADS * TQ, CK), BF16),
                        pltpu.VMEM((N_HEADS, TQ, LANES), F32),
                        pltpu.VMEM((N_HEADS, TQ, 2 * HEAD_DIM), F32)],
        compiler_params=_params(2),
        name="dsa_attention",
    )(far_bias, qi, kvw, q, bias_tiles, knorm_g.reshape(1, IDX_DIM), knorm_b.reshape(1, IDX_DIM))


def _merge_kernel(yl_ref, ya_ref, wl_ref, wa_ref, gl_ref, ga_ref, o_ref):
    pl_ = jnp.dot(yl_ref[...], wl_ref[...], preferred_element_type=F32)
    pa_ = jnp.dot(ya_ref[...], wa_ref[...], preferred_element_type=F32)
    merged = _sigmoid(gl_ref[...]) * pl_ + _sigmoid(ga_ref[...]) * pa_
    o_ref[...] = merged.astype(o_ref.dtype)


def _gated_merge(y_lru, y_att, w_l, w_a, gates, tm, tn):
    M, D = y_lru.shape
    N = w_l.shape[1]
    nj = N // tn
    return pl.pallas_call(
        _merge_kernel,
        grid=(M // tm, nj),
        in_specs=[pl.BlockSpec((tm, D), lambda i, j: (i, 0)),
                  pl.BlockSpec((tm, D), lambda i, j: (i, 0)),
                  pl.BlockSpec((D, tn), lambda i, j: (0, j)),
                  pl.BlockSpec((D, tn), lambda i, j: (0, j)),
                  pl.BlockSpec((tm, tn), lambda i, j: (i, j)),
                  pl.BlockSpec((tm, tn), lambda i, j: (i, nj + j))],
        out_specs=pl.BlockSpec((tm, tn), lambda i, j: (i, j)),
        out_shape=jax.ShapeDtypeStruct((M, N), BF16),
        compiler_params=_params(2),
        name="gated_merge",
    )(y_lru, y_att, w_l, w_a, gates, gates)


def _layer_norm_rows(v, g, b):
    mu = jnp.mean(v, axis=-1, keepdims=True)
    var = jnp.mean(jnp.square(v - mu), axis=-1, keepdims=True)
    return (v - mu) * lax.rsqrt(var + LN_EPS) * g + b


def _outproj_ln_kernel(m_ref, x_ref, w_ref, g_ref, b_ref, o32_ref, o16_ref):
    proj = jnp.dot(m_ref[...], w_ref[...], preferred_element_type=F32)
    h = _layer_norm_rows(ALPHA * x_ref[...] + proj, g_ref[...], b_ref[...])
    o32_ref[...] = h
    o16_ref[...] = h.astype(BF16)


def _outproj_ln(merged, x2d, w_out, g, b, tm):
    M, D = x2d.shape
    return pl.pallas_call(
        _outproj_ln_kernel,
        grid=(M // tm,),
        in_specs=[pl.BlockSpec((tm, D), lambda i: (i, 0)),
                  pl.BlockSpec((tm, D), lambda i: (i, 0)),
                  pl.BlockSpec((D, D), lambda i: (0, 0)),
                  pl.BlockSpec((1, D), lambda i: (0, 0)),
                  pl.BlockSpec((1, D), lambda i: (0, 0))],
        out_specs=[pl.BlockSpec((tm, D), lambda i: (i, 0)),
                   pl.BlockSpec((tm, D), lambda i: (i, 0))],
        out_shape=[jax.ShapeDtypeStruct((M, D), F32),
                   jax.ShapeDtypeStruct((M, D), BF16)],
        compiler_params=_params(1),
        name="outproj_ln1",
    )(merged, x2d, w_out, g.reshape(1, D), b.reshape(1, D))


def _ffn_up_kernel(halo_ref, h_ref, wg_ref, wv_ref, cwg_ref, cwv_ref, cbg_ref, cbv_ref, o_ref,
                   *, tm, conv_w, tiles_per_seq):
    i = pl.program_id(0)
    keep = jnp.where(i % tiles_per_seq > 0, 1.0, 0.0).astype(BF16)
    lhs = jnp.concatenate([halo_ref[...] * keep, h_ref[...]], axis=0)
    base = SUBLANES - (conv_w - 1)

    def conv_half(w_ref, cw_ref, cb_ref, cs):
        up = jnp.dot(lhs, w_ref[:, cs], preferred_element_type=F32)
        y = cb_ref[:, cs] + cw_ref[0:1, cs] * up[base:base + tm]
        for j in range(1, conv_w):
            y = y + cw_ref[j:j + 1, cs] * up[base + j:base + j + tm]
        return y

    for s in range(o_ref.shape[1] // MXU_COLS):
        cs = slice(s * MXU_COLS, (s + 1) * MXU_COLS)
        yg = conv_half(wg_ref, cwg_ref, cbg_ref, cs)
        yv = conv_half(wv_ref, cwv_ref, cbv_ref, cs)
        o_ref[:, cs] = (_gelu_tanh(yg) * yv).astype(o_ref.dtype)


def _ffn_up(h16, w_up, conv_w, conv_b, S, tm, tn):
    M, D = h16.shape
    d_ff = w_up.shape[1] // 2
    width = conv_w.shape[0]
    assert width - 1 <= SUBLANES and S % tm == 0
    nj = d_ff // tn
    kern = functools.partial(_ffn_up_kernel, tm=tm, conv_w=width, tiles_per_seq=S // tm)
    halo_blocks = tm // SUBLANES
    cb2 = conv_b.reshape(1, 2 * d_ff)
    return pl.pallas_call(
        kern,
        grid=(M // tm, nj),
        in_specs=[pl.BlockSpec((SUBLANES, D), lambda i, j: (jnp.maximum(i * halo_blocks - 1, 0), 0)),
                  pl.BlockSpec((tm, D), lambda i, j: (i, 0)),
                  pl.BlockSpec((D, tn), lambda i, j: (0, j)),
                  pl.BlockSpec((D, tn), lambda i, j: (0, nj + j)),
                  pl.BlockSpec((width, tn), lambda i, j: (0, j)),
                  pl.BlockSpec((width, tn), lambda i, j: (0, nj + j)),
                  pl.BlockSpec((1, tn), lambda i, j: (0, j)),
                  pl.BlockSpec((1, tn), lambda i, j: (0, nj + j))],
        out_specs=pl.BlockSpec((tm, tn), lambda i, j: (i, j)),
        out_shape=jax.ShapeDtypeStruct((M, d_ff), BF16),
        compiler_params=_params(2),
        name="ffn_up_conv_geglu",
    )(h16, h16, w_up, w_up, conv_w, conv_w, cb2, cb2)


def _ffn_down_kernel(a_ref, h_ref, w_ref, g_ref, b_ref, o_ref, acc_ref):
    k = pl.program_id(1)

    @pl.when(k == 0)
    def _():
        acc_ref[...] = jnp.zeros(acc_ref.shape, F32)

    acc_ref[...] += jnp.dot(a_ref[...], w_ref[...], preferred_element_type=F32)

    @pl.when(k == pl.num_programs(1) - 1)
    def _():
        o_ref[...] = _layer_norm_rows(ALPHA * h_ref[...] + acc_ref[...], g_ref[...], b_ref[...])


def _ffn_down_ln(act, h32, w_down, g, b, tm, tk):
    d_ff, D = w_down.shape
    M = act.shape[0]
    return pl.pallas_call(
        _ffn_down_kernel,
        grid=(M // tm, d_ff // tk),
        in_specs=[pl.BlockSpec((tm, tk), lambda i, k: (i, k)),
                  pl.BlockSpec((tm, D), lambda i, k: (i, 0)),
                  pl.BlockSpec((tk, D), lambda i, k: (k, 0)),
                  pl.BlockSpec((1, D), lambda i, k: (0, 0)),
                  pl.BlockSpec((1, D), lambda i, k: (0, 0))],
        out_specs=pl.BlockSpec((tm, D), lambda i, k: (i, 0)),
        out_shape=jax.ShapeDtypeStruct((M, D), F32),
        scratch_shapes=[pltpu.VMEM((tm, D), F32)],
        compiler_params=_params(2),
        name="ffn_down_ln2",
    )(act, h32, w_down, g.reshape(1, D), b.reshape(1, D))


def kernel(x, w_in, lru_conv_w, lru_conv_b, lru_gate_a_w, lru_gate_a_b, lru_gate_x_w, lru_gate_x_b,
           lru_lambda, idx_knorm_g, idx_knorm_b, rel_bias, w_proj_lru, w_proj_attn, w_out,
           ln1_g, ln1_b, ffn_w_up, ffn_conv_w, ffn_conv_b, ffn_w_down, ln2_g, ln2_b):
    B, S, D = x.shape
    assert B == SUBLANES and S % CK == 0 and w_in.shape[0] == DEPTH
    top_k = min(TOPK_MAX, S // 4)
    d_rnn = lru_conv_w.shape[-1]
    q_w = N_HEADS * HEAD_DIM
    qi_w = IDX_HEADS * IDX_DIM
    splits = (d_rnn, d_rnn, q_w, HEAD_DIM, HEAD_DIM, qi_w, IDX_DIM, IDX_HEADS, D, D)
    offs = [0]
    for s_ in splits:
        offs.append(offs[-1] + s_)
    M = B * S

    l = 0
    w = w_in[l].astype(BF16)
    x2d = x.reshape(M, D)
    x16 = x2d.astype(BF16)
    x16_tm = jnp.transpose(x, (1, 0, 2)).astype(BF16).reshape(M, D)
    w_lru = w[:, offs[0]:offs[2]]
    w_q = w[:, offs[2]:offs[3]]
    pad = LANES - IDX_DIM - IDX_HEADS
    w_kvw = jnp.concatenate([w[:, offs[3]:offs[5]], w[:, offs[6]:offs[8]],
                             jnp.zeros((D, pad), BF16)], axis=1)
    w_qi = w[:, offs[5]:offs[6]]
    w_g = w[:, offs[8]:offs[10]]

    tm_in = min(M, 1024)
    lxg = _project(x16_tm, w_lru, F32, tm_in, 1024).reshape(S, B, 2 * d_rnn)
    q = _project(x16, w_q, BF16, tm_in, 1024,
                 out_scale=(HEAD_DIM ** -0.5) * LOG2E).reshape(B, S, q_w)
    kvw = _project(x16, w_kvw, F32, tm_in, w_kvw.shape[1]).reshape(B, S, w_kvw.shape[1])
    qi = _project(x16, w_qi, F32, tm_in, 1024).reshape(B, S, qi_w)
    gates = _project(x16, w_g, F32, tm_in, 1024)

    y_lru = _rg_lru(lxg, lru_conv_w[l], lru_conv_b[l], lru_gate_a_w[l], lru_gate_a_b[l],
                    lru_gate_x_w[l], lru_gate_x_b[l], lru_lambda[l], S, B, tc=min(S, 256), cbw=256)

    tiles = _bias_tiles(rel_bias)
    far_bias = rel_bias[_far_bucket()]
    y_att = _sparse_attention(qi, kvw, q, tiles, far_bias, idx_knorm_g[l], idx_knorm_b[l], top_k)

    merged = _gated_merge(y_lru.reshape(M, d_rnn), y_att.reshape(M, q_w),
                          w_proj_lru[l].astype(BF16), w_proj_attn[l].astype(BF16),
                          gates, tm=512, tn=1024)
    h32, h16 = _outproj_ln(merged, x2d, w_out[l].astype(BF16), ln1_g[l], ln1_b[l], tm=512)
    act = _ffn_up(h16, ffn_w_up[l].astype(BF16), ffn_conv_w[l], ffn_conv_b[l], S,
                  tm=min(S, 1024), tn=1024)
    out = _ffn_down_ln(act, h32, ffn_w_down[l].astype(BF16), ln2_g[l], ln2_b[l], tm=512, tk=2048)
    return out.reshape(B, S, D)
```

```python
import functools
import math

import jax
import jax.numpy as jnp
from jax import lax
from jax.experimental import pallas as pl
from jax.experimental.pallas import tpu as pltpu

N_HEADS = 16
HEAD_DIM = 128
IDX_HEADS = 16
IDX_DIM = 64
TOPK_MAX = 256
LRU_BLOCK = 128
LRU_C = 8.0
REL_BUCKETS = 32
REL_MAX_DIST = 128
LN_EPS = 1e-5
DEPTH = 1
ALPHA = (2.0 * DEPTH) ** 0.25

LANES = 128
SUBLANES = 8
VMEM_LIMIT_BYTES = 56 * 1024 * 1024

TQ = 256
CK = 256
HEAD_GROUP = 4
BIAS_TILE_STEP = 256
N_BIAS_TILES = 2
assert TQ % BIAS_TILE_STEP == 0 and CK % BIAS_TILE_STEP == 0
assert N_BIAS_TILES * BIAS_TILE_STEP - (CK - 1) >= REL_MAX_DIST

INT_MIN = -(2 ** 31)
MASK_LOGIT = -2e30
LOG2E = math.log2(math.e)
F32 = jnp.float32
BF16 = jnp.bfloat16


def _params(n_axes):
    return pltpu.CompilerParams(dimension_semantics=("arbitrary",) * n_axes,
                                vmem_limit_bytes=VMEM_LIMIT_BYTES)


def _matmul_kernel(x_ref, w_ref, o_ref, *, out_scale):
    acc = jnp.dot(x_ref[...], w_ref[...], preferred_element_type=F32)
    if out_scale != 1.0:
        acc = acc * out_scale
    o_ref[...] = acc.astype(o_ref.dtype)


def _project(x2d, w, out_dtype, tm, tn, out_scale=1.0):
    M, K = x2d.shape
    N = w.shape[1]
    return pl.pallas_call(
        functools.partial(_matmul_kernel, out_scale=out_scale),
        grid=(M // tm, N // tn),
        in_specs=[pl.BlockSpec((tm, K), lambda i, j: (i, 0)),
                  pl.BlockSpec((K, tn), lambda i, j: (0, j))],
        out_specs=pl.BlockSpec((tm, tn), lambda i, j: (i, j)),
        out_shape=jax.ShapeDtypeStruct((M, N), out_dtype),
        compiler_params=_params(2),
        name="in_proj",
    )(x2d, w)


def _gelu_tanh(x):
    return 0.5 * x * (1.0 + jnp.tanh(math.sqrt(2.0 / math.pi) * (x + 0.044715 * (x * x * x))))


def _sigmoid(x):
    return 0.5 * jnp.tanh(0.5 * x) + 0.5


def _lru_kernel(lx_ref, lg_ref, cw_ref, cb_ref, wa_ref, ba_ref, wx_ref, bx_ref, lam_ref, o_ref,
                xs, a_s, u_s, h_s, *, tc, cb_width, conv_w):
    ti = pl.program_id(1)
    halo = conv_w - 1
    n_slab = cb_width // LANES
    rows = tc * SUBLANES
    slab = lambda v, s: v[:, s * LANES:(s + 1) * LANES]

    @pl.when(ti == 0)
    def _():
        xs[0:halo] = jnp.zeros((halo, SUBLANES, cb_width), F32)
        h_s[...] = jnp.zeros(h_s.shape, F32)

    xs[halo:halo + tc] = lx_ref[...]
    y = cb_ref[...] + cw_ref[0:1, :] * xs[0:tc]
    for j in range(1, conv_w):
        y = y + cw_ref[j:j + 1, :] * xs[j:j + tc]
    xs[0:halo] = xs[tc:tc + halo]

    y2 = y.reshape(rows, cb_width)
    yb = y2.astype(BF16)
    r_parts, i_parts = [], []
    for n in range(cb_width // LRU_BLOCK):
        blk = yb[:, n * LRU_BLOCK:(n + 1) * LRU_BLOCK]
        r_parts.append(jnp.dot(blk, wa_ref[n], preferred_element_type=F32))
        i_parts.append(jnp.dot(blk, wx_ref[n], preferred_element_type=F32))
    r = _sigmoid(jnp.concatenate(r_parts, axis=1) + ba_ref[...])
    g_in = _sigmoid(jnp.concatenate(i_parts, axis=1) + bx_ref[...])

    z = -lam_ref[...]
    softplus = jnp.maximum(z, 0.0) + jnp.log1p(jnp.exp(-jnp.abs(z)))
    log_a = (-LRU_C) * r * softplus
    a = jnp.exp(log_a)
    mult = jnp.sqrt(-jnp.tanh(log_a) * (a * a + 1.0))
    gx = g_in * y2
    u = mult * gx
    for s in range(n_slab):
        a_s[s] = slab(a, s)
        u_s[s] = slab(u, s)

    @pl.when(ti == 0)
    def _():
        for s in range(n_slab):
            u_s[s, 0:SUBLANES, :] = slab(gx, s)[0:SUBLANES]

    def step(t, h):
        r0 = pl.multiple_of(t * SUBLANES, SUBLANES)
        new = []
        for s in range(n_slab):
            hs = a_s[s, pl.ds(r0, SUBLANES), :] * h[s] + u_s[s, pl.ds(r0, SUBLANES), :]
            u_s[s, pl.ds(r0, SUBLANES), :] = hs
            new.append(hs)
        return tuple(new)

    h_fin = lax.fori_loop(0, tc, step, tuple(h_s[s] for s in range(n_slab)), unroll=8)
    gate = _gelu_tanh(lg_ref[...]).reshape(rows, cb_width)
    for s in range(n_slab):
        h_s[s] = h_fin[s]
        u_s[s] = slab(gate, s) * u_s[s]
    for b in range(SUBLANES):
        for s in range(n_slab):
            o_ref[b, :, s * LANES:(s + 1) * LANES] = (
                u_s[s, pl.ds(b, tc, stride=SUBLANES), :].astype(o_ref.dtype))


def _rg_lru(lxg, conv_w, conv_b, gate_a_w, gate_a_b, gate_x_w, gate_x_b, lam, S, B, tc, cbw):
    d_rnn = conv_w.shape[1]
    width = conv_w.shape[0]
    ncb = d_rnn // cbw
    nblk = cbw // LRU_BLOCK
    n_slab = cbw // LANES
    row = lambda v: v.reshape(1, d_rnn)
    kern = functools.partial(_lru_kernel, tc=tc, cb_width=cbw, conv_w=width)
    vec_spec = pl.BlockSpec((1, cbw), lambda n, t: (0, n))
    return pl.pallas_call(
        kern,
        grid=(ncb, S // tc),
        in_specs=[pl.BlockSpec((tc, B, cbw), lambda n, t: (t, 0, n)),
                  pl.BlockSpec((tc, B, cbw), lambda n, t: (t, 0, ncb + n)),
                  pl.BlockSpec((width, cbw), lambda n, t: (0, n)),
                  vec_spec,
                  pl.BlockSpec((nblk, LRU_BLOCK, LRU_BLOCK), lambda n, t: (n, 0, 0)),
                  vec_spec,
                  pl.BlockSpec((nblk, LRU_BLOCK, LRU_BLOCK), lambda n, t: (n, 0, 0)),
                  vec_spec,
                  vec_spec],
        out_specs=pl.BlockSpec((B, tc, cbw), lambda n, t: (0, t, n)),
        out_shape=jax.ShapeDtypeStruct((B, S, d_rnn), BF16),
        scratch_shapes=[pltpu.VMEM((tc + width - 1, B, cbw), F32),
                        pltpu.VMEM((n_slab, tc * B, LANES), F32),
                        pltpu.VMEM((n_slab, tc * B, LANES), F32),
                        pltpu.VMEM((n_slab, B, LANES), F32)],
        compiler_params=_params(2),
        name="rg_lru",
    )(lxg, lxg, conv_w, row(conv_b), gate_a_w.astype(BF16), row(gate_a_b),
      gate_x_w.astype(BF16), row(gate_x_b), row(lam))


def _far_bucket():
    max_exact = REL_BUCKETS // 2
    large = max_exact + int(math.log(REL_MAX_DIST / max_exact) / math.log(REL_MAX_DIST / max_exact)
                            * (REL_BUCKETS - max_exact))
    return min(large, REL_BUCKETS - 1)


def _bias_tile_kernel(rb_ref, o_ref):
    d = pl.program_id(0)
    h = pl.program_id(1)
    rows = lax.broadcasted_iota(jnp.int32, (TQ, CK), 0)
    cols = lax.broadcasted_iota(jnp.int32, (TQ, CK), 1)
    rel = d * BIAS_TILE_STEP + rows - cols
    max_exact = REL_BUCKETS // 2
    nf = jnp.maximum(rel, 1).astype(F32)
    large = max_exact + (jnp.log(nf / max_exact) / math.log(REL_MAX_DIST / max_exact)
                         * (REL_BUCKETS - max_exact)).astype(jnp.int32)
    large = jnp.minimum(large, REL_BUCKETS - 1)
    bucket = jnp.where(rel < max_exact, rel, large)
    acc = jnp.zeros((TQ, CK), F32)
    for b in range(REL_BUCKETS):
        acc = jnp.where(bucket == b, rb_ref[b, h], acc)
    o_ref[...] = (acc - rb_ref[_far_bucket(), h]) * LOG2E


def _bias_tiles(rel_bias):
    return pl.pallas_call(
        _bias_tile_kernel,
        grid=(N_BIAS_TILES, N_HEADS),
        in_specs=[pl.BlockSpec(memory_space=pltpu.SMEM)],
        out_specs=pl.BlockSpec((None, None, TQ, CK), lambda d, h: (d, h, 0, 0)),
        out_shape=jax.ShapeDtypeStruct((N_BIAS_TILES, N_HEADS, TQ, CK), F32),
        compiler_params=_params(2),
        name="t5_bias_tiles",
    )(rel_bias)


I16_MIN = -(2 ** 15)
I16_MAX = 2 ** 15 - 1


def _ordered_key(score):
    bits = pltpu.bitcast(score, jnp.int32)
    return jnp.where(bits >= 0, bits, bits ^ jnp.int32(0x7FFFFFFF))


def _count_ge(arr, n_pairs, cand):
    cand16 = jnp.broadcast_to(cand, (TQ, LANES)).astype(jnp.int16)
    cand16 = jnp.concatenate([cand16] * (CK // LANES), axis=1)
    one, zero = jnp.int16(1), jnp.int16(0)

    def count_pair(p, cnt):
        cnt = cnt + jnp.where(arr[p, 0] >= cand16, one, zero)
        return cnt + jnp.where(arr[p, 1] >= cand16, one, zero)

    cnt = lax.fori_loop(0, n_pairs, count_pair, jnp.zeros((TQ, CK), jnp.int16))
    return jnp.sum(cnt.astype(jnp.int32).astype(F32), axis=1, keepdims=True)


def _kth_largest_i16(arr, n_pairs, need):
    def bit_step(step, thr):
        cand = thr + lax.shift_left(jnp.int32(1), 15 - step)
        return jnp.where(_count_ge(arr, n_pairs, cand) >= need, cand, thr)

    return lax.fori_loop(0, 16, bit_step, jnp.full((TQ, 1), I16_MIN, jnp.int32))


def _attn_kernel(far_ref, qi_ref, kvw_ref, q_ref, bias_ref, kg_ref, kb_ref, o_ref,
                 kin_t, k_t, v_aug, keys, hi_s, lo_s, qis, wbuf, qs, pbuf, m_s, accl_s,
                 *, top_k, seq):
    i = pl.program_id(1)
    k_off, v_off, kw_off = 0, HEAD_DIM, 2 * HEAD_DIM
    n_all = seq // CK
    lane = lax.broadcasted_iota(jnp.int32, (TQ, LANES), 1)

    @pl.when(i == 0)
    def _():
        kraw = kvw_ref[:, kw_off:kw_off + IDX_DIM]
        mu = jnp.mean(kraw, axis=-1, keepdims=True)
        var = jnp.mean(jnp.square(kraw - mu), axis=-1, keepdims=True)
        kn = (kraw - mu) * lax.rsqrt(var + LN_EPS) * kg_ref[...] + kb_ref[...]
        kn = jnp.concatenate([kn, jnp.zeros((seq, LANES - IDX_DIM), F32)], axis=1)
        sub = lax.broadcasted_iota(jnp.int32, (HEAD_DIM, CK), 0)
        ones_rows = jnp.where(sub < 2, 1.0, 0.0).astype(BF16)
        for c in range(n_all):
            kin_t[c] = kn[c * CK:(c + 1) * CK, :].T[0:IDX_DIM].astype(BF16)
            k_t[c, 0:HEAD_DIM, :] = kvw_ref[c * CK:(c + 1) * CK, k_off:k_off + HEAD_DIM].T.astype(BF16)
            k_t[c, HEAD_DIM:2 * HEAD_DIM, :] = ones_rows
        v_aug[:, 0:HEAD_DIM] = kvw_ref[:, v_off:v_off + HEAD_DIM].astype(BF16)
        v_aug[:, HEAD_DIM:2 * HEAD_DIM] = jnp.ones((seq, HEAD_DIM), BF16)
        for h in range(N_HEADS):
            fb = jnp.full((TQ, LANES), far_ref[h] * LOG2E, F32)
            fb_hi = fb.astype(BF16).astype(F32)
            qs[h * TQ:(h + 1) * TQ, HEAD_DIM:2 * HEAD_DIM] = jnp.where(
                lane == 0, fb_hi, jnp.where(lane == 1, fb - fb_hi, 0.0)).astype(BF16)

    q0 = pl.multiple_of(i * TQ, TQ)
    n_chunks = (q0 + TQ + CK - 1) // CK
    n_pairs = (n_chunks + 1) // 2
    rows = q0 + lax.broadcasted_iota(jnp.int32, (TQ, CK), 0)
    cols = lax.broadcasted_iota(jnp.int32, (TQ, CK), 1)

    qib = qi_ref[...].astype(BF16)
    wi = kvw_ref[pl.ds(q0, TQ), kw_off + IDX_DIM:kw_off + IDX_DIM + IDX_HEADS]
    wi = wi * ((IDX_DIM ** -0.5) * (IDX_HEADS ** -0.5))
    for h in range(IDX_HEADS):
        qis[h * TQ:(h + 1) * TQ, :] = qib[:, h * IDX_DIM:(h + 1) * IDX_DIM]
        wbuf[h] = jnp.broadcast_to(wi[:, h:h + 1], (TQ, LANES))

    def score_chunk(c, carry):
        kc = kin_t[c]
        acc = jnp.zeros((TQ, CK), F32)
        for g in range(IDX_HEADS // HEAD_GROUP):
            g0 = g * HEAD_GROUP * TQ
            d_g = jnp.dot(qis[g0:g0 + HEAD_GROUP * TQ, :], kc, preferred_element_type=F32)
            for hh in range(HEAD_GROUP):
                w_h = wbuf[g * HEAD_GROUP + hh]
                acc = acc + jnp.concatenate([w_h] * (CK // LANES), axis=1) * jnp.maximum(
                    d_g[hh * TQ:(hh + 1) * TQ], 0.0)
        key = jnp.where(c * CK + cols <= rows, _ordered_key(acc), jnp.int32(INT_MIN))
        keys[c] = key
        hi_s[c // 2, c % 2] = lax.shift_right_arithmetic(key, 16).astype(jnp.int16)
        lo_s[c // 2, c % 2] = ((key & 0xFFFF) + I16_MIN).astype(jnp.int16)
        return carry

    lax.fori_loop(0, n_chunks, score_chunk, 0)

    @pl.when(n_chunks % 2 == 1)
    def _():
        hi_s[n_chunks // 2, 1] = jnp.full((TQ, CK), I16_MIN, jnp.int16)
        lo_s[n_chunks // 2, 1] = jnp.full((TQ, CK), I16_MIN, jnp.int16)

    need = jnp.full((TQ, 1), float(top_k), F32)
    hi_k = _kth_largest_i16(hi_s, n_pairs, need)
    above = jnp.where(hi_k == I16_MAX, 0.0,
                      _count_ge(hi_s, n_pairs, jnp.minimum(hi_k + 1, I16_MAX)))
    hi_k16 = jnp.broadcast_to(hi_k, (TQ, LANES)).astype(jnp.int16)
    hi_k16 = jnp.concatenate([hi_k16] * (CK // LANES), axis=1)

    def keep_group(p, carry):
        for j in range(2):
            lo_s[p, j] = jnp.where(hi_s[p, j] == hi_k16, lo_s[p, j], jnp.int16(I16_MIN))
        return carry

    lax.fori_loop(0, n_pairs, keep_group, 0)
    lo_k = _kth_largest_i16(lo_s, n_pairs, need - above)
    thr = hi_k * 65536 + (lo_k - I16_MIN)
    thr = jnp.maximum(thr, jnp.int32(INT_MIN + 1))

    for h in range(N_HEADS):
        qs[h * TQ:(h + 1) * TQ, 0:HEAD_DIM] = q_ref[:, h * HEAD_DIM:(h + 1) * HEAD_DIM]
    m_s[...] = jnp.full(m_s.shape, 0.5 * MASK_LOGIT, F32)
    accl_s[...] = jnp.zeros(accl_s.shape, F32)

    def attn_chunk(c, carry, *, near):
        kc = k_t[c]
        vc = v_aug[pl.ds(pl.multiple_of(c * CK, CK), CK), :]
        sel = keys[c] >= thr
        tile = (q0 - c * CK) // BIAS_TILE_STEP
        for g in range(N_HEADS // HEAD_GROUP):
            g0 = g * HEAD_GROUP * TQ
            s_g = jnp.dot(qs[g0:g0 + HEAD_GROUP * TQ, :], kc, preferred_element_type=F32)
            alphas = []
            for hh in range(HEAD_GROUP):
                h = g * HEAD_GROUP + hh
                s = s_g[hh * TQ:(hh + 1) * TQ]
                if near:
                    s = s + bias_ref[tile, h]
                s = jnp.where(sel, s, MASK_LOGIT)
                m_old = m_s[h]
                m_new = jnp.maximum(m_old, jnp.max(s, axis=1, keepdims=True))
                p = jnp.exp2(s - jnp.concatenate([m_new] * (CK // LANES), axis=1))
                alphas.append(jnp.exp2(m_old - m_new))
                m_s[h] = m_new
                pbuf[h * TQ:(h + 1) * TQ, :] = p.astype(BF16)
            pv = jnp.dot(pbuf[g0:g0 + HEAD_GROUP * TQ, :], vc, preferred_element_type=F32)
            for hh in range(HEAD_GROUP):
                h = g * HEAD_GROUP + hh
                accl_s[h] = (jnp.concatenate([alphas[hh], alphas[hh]], axis=1) * accl_s[h]
                             + pv[hh * TQ:(hh + 1) * TQ])
        return carry

    n_far = jnp.maximum((q0 - N_BIAS_TILES * BIAS_TILE_STEP) // CK + 1, 0)
    lax.fori_loop(0, n_far, functools.partial(attn_chunk, near=False), 0)
    lax.fori_loop(n_far, n_chunks, functools.partial(attn_chunk, near=True), 0)
    for h in range(N_HEADS):
        al = accl_s[h]
        o_ref[:, h * HEAD_DIM:(h + 1) * HEAD_DIM] = (
            al[:, 0:HEAD_DIM] / al[:, HEAD_DIM:2 * HEAD_DIM]).astype(o_ref.dtype)


def _sparse_attention(qi, kvw, q, bias_tiles, far_bias, knorm_g, knorm_b, top_k):
    B, S, q_w = q.shape
    qi_w = qi.shape[2]
    kvw_w = kvw.shape[2]
    n_all = S // CK
    assert n_all % 2 == 0
    kern = functools.partial(_attn_kernel, top_k=top_k, seq=S)
    return pl.pallas_call(
        kern,
        grid=(B, S // TQ),
        in_specs=[pl.BlockSpec(memory_space=pltpu.SMEM),
                  pl.BlockSpec((None, TQ, qi_w), lambda b, i: (b, i, 0)),
                  pl.BlockSpec((None, S, kvw_w), lambda b, i: (b, 0, 0)),
                  pl.BlockSpec((None, TQ, q_w), lambda b, i: (b, i, 0)),
                  pl.BlockSpec((N_BIAS_TILES, N_HEADS, TQ, CK), lambda b, i: (0, 0, 0, 0),
                               pipeline_mode=pl.Buffered(1)),
                  pl.BlockSpec((1, IDX_DIM), lambda b, i: (0, 0)),
                  pl.BlockSpec((1, IDX_DIM), lambda b, i: (0, 0))],
        out_specs=pl.BlockSpec((None, TQ, q_w), lambda b, i: (b, i, 0)),
        out_shape=jax.ShapeDtypeStruct((B, S, q_w), BF16),
        scratch_shapes=[pltpu.VMEM((n_all, IDX_DIM, CK), BF16),
                        pltpu.VMEM((n_all, 2 * HEAD_DIM, CK), BF16),
                        pltpu.VMEM((S, 2 * HEAD_DIM), BF16),
                        pltpu.VMEM((n_all, TQ, CK), jnp.int32),
                        pltpu.VMEM((n_all // 2, 2, TQ, CK), jnp.int16),
                        pltpu.VMEM((n_all // 2, 2, TQ, CK), jnp.int16),
                        pltpu.VMEM((IDX_HEADS * TQ, IDX_DIM), BF16),
                        pltpu.VMEM((IDX_HEADS, TQ, LANES), F32),
                        pltpu.VMEM((N_HEADS * TQ, 2 * HEAD_DIM), BF16),
                        pltpu.VMEM((N_HEADS * TQ, CK), BF16),
                        pltpu.VMEM((N_HEADS, TQ, LANES), F32),
                        pltpu.VMEM((N_HEADS, TQ, 2 * HEAD_DIM), F32)],
        compiler_params=_params(2),
        name="dsa_attention",
    )(far_bias, qi, kvw, q, bias_tiles, knorm_g.reshape(1, IDX_DIM), knorm_b.reshape(1, IDX_DIM))


def _merge_kernel(yl_ref, ya_ref, wl_ref, wa_ref, gl_ref, ga_ref, o_ref):
    pl_ = jnp.dot(yl_ref[...], wl_ref[...], preferred_element_type=F32)
    pa_ = jnp.dot(ya_ref[...], wa_ref[...], preferred_element_type=F32)
    merged = (_sigmoid(gl_ref[...].astype(F32)) * pl_
              + _sigmoid(ga_ref[...].astype(F32)) * pa_)
    o_ref[...] = merged.astype(o_ref.dtype)


def _gated_merge(y_lru, y_att, w_l, w_a, gates, tm, tn):
    M, D = y_lru.shape
    N = w_l.shape[1]
    nj = N // tn
    return pl.pallas_call(
        _merge_kernel,
        grid=(M // tm, nj),
        in_specs=[pl.BlockSpec((tm, D), lambda i, j: (i, 0)),
                  pl.BlockSpec((tm, D), lambda i, j: (i, 0)),
                  pl.BlockSpec((D, tn), lambda i, j: (0, j)),
                  pl.BlockSpec((D, tn), lambda i, j: (0, j)),
                  pl.BlockSpec((tm, tn), lambda i, j: (i, j)),
                  pl.BlockSpec((tm, tn), lambda i, j: (i, nj + j))],
        out_specs=pl.BlockSpec((tm, tn), lambda i, j: (i, j)),
        out_shape=jax.ShapeDtypeStruct((M, N), BF16),
        compiler_params=_params(2),
        name="gated_merge",
    )(y_lru, y_att, w_l, w_a, gates, gates)


def _layer_norm_rows(v, g, b):
    mu = jnp.mean(v, axis=-1, keepdims=True)
    var = jnp.mean(jnp.square(v - mu), axis=-1, keepdims=True)
    return (v - mu) * lax.rsqrt(var + LN_EPS) * g + b


def _outproj_ln_kernel(m_ref, x_ref, w_ref, g_ref, b_ref, o32_ref, o16_ref):
    proj = jnp.dot(m_ref[...], w_ref[...], preferred_element_type=F32)
    h = _layer_norm_rows(ALPHA * x_ref[...] + proj, g_ref[...], b_ref[...])
    o32_ref[...] = h
    o16_ref[...] = h.astype(BF16)


def _outproj_ln(merged, x2d, w_out, g, b, tm):
    M, D = x2d.shape
    return pl.pallas_call(
        _outproj_ln_kernel,
        grid=(M // tm,),
        in_specs=[pl.BlockSpec((tm, D), lambda i: (i, 0)),
                  pl.BlockSpec((tm, D), lambda i: (i, 0)),
                  pl.BlockSpec((D, D), lambda i: (0, 0)),
                  pl.BlockSpec((1, D), lambda i: (0, 0)),
                  pl.BlockSpec((1, D), lambda i: (0, 0))],
        out_specs=[pl.BlockSpec((tm, D), lambda i: (i, 0)),
                   pl.BlockSpec((tm, D), lambda i: (i, 0))],
        out_shape=[jax.ShapeDtypeStruct((M, D), F32),
                   jax.ShapeDtypeStruct((M, D), BF16)],
        compiler_params=_params(1),
        name="outproj_ln1",
    )(merged, x2d, w_out, g.reshape(1, D), b.reshape(1, D))


def _ffn_up_kernel(halo_ref, h_ref, wg_ref, wv_ref, cwg_ref, cwv_ref, cbg_ref, cbv_ref, o_ref,
                   *, tm, conv_w, tiles_per_seq):
    i = pl.program_id(0)
    keep = jnp.where(i % tiles_per_seq > 0, 1.0, 0.0).astype(BF16)
    lhs = jnp.concatenate([halo_ref[...] * keep, h_ref[...]], axis=0)
    base = SUBLANES - (conv_w - 1)

    def conv_half(w_ref, cw_ref, cb_ref):
        up = jnp.dot(lhs, w_ref[...], preferred_element_type=F32)
        y = cb_ref[...] + cw_ref[0:1, :] * up[base:base + tm]
        for j in range(1, conv_w):
            y = y + cw_ref[j:j + 1, :] * up[base + j:base + j + tm]
        return y

    yg = conv_half(wg_ref, cwg_ref, cbg_ref)
    yv = conv_half(wv_ref, cwv_ref, cbv_ref)
    o_ref[...] = (_gelu_tanh(yg) * yv).astype(o_ref.dtype)


def _ffn_up(h16, w_up, conv_w, conv_b, S, tm, tn):
    M, D = h16.shape
    d_ff = w_up.shape[1] // 2
    width = conv_w.shape[0]
    assert width - 1 <= SUBLANES and S % tm == 0
    nj = d_ff // tn
    kern = functools.partial(_ffn_up_kernel, tm=tm, conv_w=width, tiles_per_seq=S // tm)
    halo_blocks = tm // SUBLANES
    cb2 = conv_b.reshape(1, 2 * d_ff)
    return pl.pallas_call(
        kern,
        grid=(M // tm, nj),
        in_specs=[pl.BlockSpec((SUBLANES, D), lambda i, j: (jnp.maximum(i * halo_blocks - 1, 0), 0)),
                  pl.BlockSpec((tm, D), lambda i, j: (i, 0)),
                  pl.BlockSpec((D, tn), lambda i, j: (0, j)),
                  pl.BlockSpec((D, tn), lambda i, j: (0, nj + j)),
                  pl.BlockSpec((width, tn), lambda i, j: (0, j)),
                  pl.BlockSpec((width, tn), lambda i, j: (0, nj + j)),
                  pl.BlockSpec((1, tn), lambda i, j: (0, j)),
                  pl.BlockSpec((1, tn), lambda i, j: (0, nj + j))],
        out_specs=pl.BlockSpec((tm, tn), lambda i, j: (i, j)),
        out_shape=jax.ShapeDtypeStruct((M, d_ff), BF16),
        compiler_params=_params(2),
        name="ffn_up_conv_geglu",
    )(h16, h16, w_up, w_up, conv_w, conv_w, cb2, cb2)


def _ffn_down_kernel(a_ref, h_ref, w_ref, g_ref, b_ref, o_ref):
    k = pl.program_id(1)

    @pl.when(k == 0)
    def _():
        o_ref[...] = jnp.dot(a_ref[...], w_ref[...], preferred_element_type=F32)

    @pl.when(k > 0)
    def _():
        o_ref[...] += jnp.dot(a_ref[...], w_ref[...], preferred_element_type=F32)

    @pl.when(k == pl.num_programs(1) - 1)
    def _():
        o_ref[...] = _layer_norm_rows(ALPHA * h_ref[...] + o_ref[...], g_ref[...], b_ref[...])


def _ffn_down_ln(act, h32, w_down, g, b, tm, tk):
    d_ff, D = w_down.shape
    M = act.shape[0]
    return pl.pallas_call(
        _ffn_down_kernel,
        grid=(M // tm, d_ff // tk),
        in_specs=[pl.BlockSpec((tm, tk), lambda i, k: (i, k)),
                  pl.BlockSpec((tm, D), lambda i, k: (i, 0)),
                  pl.BlockSpec((tk, D), lambda i, k: (k, 0)),
                  pl.BlockSpec((1, D), lambda i, k: (0, 0)),
                  pl.BlockSpec((1, D), lambda i, k: (0, 0))],
        out_specs=pl.BlockSpec((tm, D), lambda i, k: (i, 0)),
        out_shape=jax.ShapeDtypeStruct((M, D), F32),
        compiler_params=_params(2),
        name="ffn_down_ln2",
    )(act, h32, w_down, g.reshape(1, D), b.reshape(1, D))


def kernel(x, w_in, lru_conv_w, lru_conv_b, lru_gate_a_w, lru_gate_a_b, lru_gate_x_w, lru_gate_x_b,
           lru_lambda, idx_knorm_g, idx_knorm_b, rel_bias, w_proj_lru, w_proj_attn, w_out,
           ln1_g, ln1_b, ffn_w_up, ffn_conv_w, ffn_conv_b, ffn_w_down, ln2_g, ln2_b):
    B, S, D = x.shape
    assert B == SUBLANES and S % CK == 0 and w_in.shape[0] == DEPTH
    top_k = min(TOPK_MAX, S // 4)
    d_rnn = lru_conv_w.shape[-1]
    q_w = N_HEADS * HEAD_DIM
    qi_w = IDX_HEADS * IDX_DIM
    splits = (d_rnn, d_rnn, q_w, HEAD_DIM, HEAD_DIM, qi_w, IDX_DIM, IDX_HEADS, D, D)
    offs = [0]
    for s_ in splits:
        offs.append(offs[-1] + s_)
    M = B * S

    l = 0
    w = w_in[l].astype(BF16)
    x2d = x.reshape(M, D)
    x16 = x2d.astype(BF16)
    x16_tm = jnp.transpose(x, (1, 0, 2)).astype(BF16).reshape(M, D)
    w_lru = w[:, offs[0]:offs[2]]
    w_q = w[:, offs[2]:offs[3]]
    pad = LANES - IDX_DIM - IDX_HEADS
    w_kvw = jnp.concatenate([w[:, offs[3]:offs[5]], w[:, offs[6]:offs[8]],
                             jnp.zeros((D, pad), BF16)], axis=1)
    w_qi = w[:, offs[5]:offs[6]]
    w_g = w[:, offs[8]:offs[10]]

    tm_in = min(M, 2048)
    lxg = _project(x16_tm, w_lru, F32, tm_in, 1024).reshape(S, B, 2 * d_rnn)
    q = _project(x16, w_q, BF16, tm_in, 1024,
                 out_scale=(HEAD_DIM ** -0.5) * LOG2E).reshape(B, S, q_w)
    kvw = _project(x16, w_kvw, F32, tm_in, w_kvw.shape[1]).reshape(B, S, w_kvw.shape[1])
    qi = _project(x16, w_qi, F32, tm_in, 1024).reshape(B, S, qi_w)
    gates = _project(x16, w_g, BF16, tm_in, 1024)

    y_lru = _rg_lru(lxg, lru_conv_w[l], lru_conv_b[l], lru_gate_a_w[l], lru_gate_a_b[l],
                    lru_gate_x_w[l], lru_gate_x_b[l], lru_lambda[l], S, B, tc=min(S, 256), cbw=256)

    tiles = _bias_tiles(rel_bias)
    far_bias = rel_bias[_far_bucket()]
    y_att = _sparse_attention(qi, kvw, q, tiles, far_bias, idx_knorm_g[l], idx_knorm_b[l], top_k)

    merged = _gated_merge(y_lru.reshape(M, d_rnn), y_att.reshape(M, q_w),
                          w_proj_lru[l].astype(BF16), w_proj_attn[l].astype(BF16),
                          gates, tm=min(M, 1024), tn=512)
    h32, h16 = _outproj_ln(merged, x2d, w_out[l].astype(BF16), ln1_g[l], ln1_b[l], tm=512)
    act = _ffn_up(h16, ffn_w_up[l].astype(BF16), ffn_conv_w[l], ffn_conv_b[l], S,
                  tm=min(S, 1024), tn=512)
    out = _ffn_down_ln(act, h32, ffn_w_down[l].astype(BF16), ln2_g[l], ln2_b[l],
                       tm=min(M, 1024), tk=1024)
    return out.reshape(B, S, D)
```

```python
import functools
import math

import jax
import jax.numpy as jnp
from jax import lax
from jax.experimental import pallas as pl
from jax.experimental.pallas import tpu as pltpu

N_HEADS = 16
HEAD_DIM = 128
IDX_HEADS = 16
IDX_DIM = 64
TOPK_MAX = 256
LRU_BLOCK = 128
LRU_C = 8.0
REL_BUCKETS = 32
REL_MAX_DIST = 128
LN_EPS = 1e-5
DEPTH = 1
ALPHA = (2.0 * DEPTH) ** 0.25

LANES = 128
SUBLANES = 8
VMEM_LIMIT_BYTES = 56 * 1024 * 1024

TQ = 256
CK = 256
HEAD_GROUP = 8
BIAS_TILE_STEP = 256
N_BIAS_TILES = 2
assert TQ % BIAS_TILE_STEP == 0 and CK % BIAS_TILE_STEP == 0
assert N_BIAS_TILES * BIAS_TILE_STEP - (CK - 1) >= REL_MAX_DIST

INT_MIN = -(2 ** 31)
MASK_LOGIT = -2e30
LOG2E = math.log2(math.e)
F32 = jnp.float32
BF16 = jnp.bfloat16


def _params(n_axes):
    return pltpu.CompilerParams(dimension_semantics=("arbitrary",) * n_axes,
                                vmem_limit_bytes=VMEM_LIMIT_BYTES)


def _matmul_kernel(x_ref, w_ref, o_ref, *, out_scale):
    acc = jnp.dot(x_ref[...], w_ref[...], preferred_element_type=F32)
    if out_scale != 1.0:
        acc = acc * out_scale
    o_ref[...] = acc.astype(o_ref.dtype)


def _project(x2d, w, out_dtype, tm, tn, out_scale=1.0):
    M, K = x2d.shape
    N = w.shape[1]
    return pl.pallas_call(
        functools.partial(_matmul_kernel, out_scale=out_scale),
        grid=(M // tm, N // tn),
        in_specs=[pl.BlockSpec((tm, K), lambda i, j: (i, 0)),
                  pl.BlockSpec((K, tn), lambda i, j: (0, j))],
        out_specs=pl.BlockSpec((tm, tn), lambda i, j: (i, j)),
        out_shape=jax.ShapeDtypeStruct((M, N), out_dtype),
        compiler_params=_params(2),
        name="in_proj",
    )(x2d, w)


def _gelu_tanh(x):
    return 0.5 * x * (1.0 + jnp.tanh(math.sqrt(2.0 / math.pi) * (x + 0.044715 * (x * x * x))))


def _sigmoid(x):
    return 0.5 * jnp.tanh(0.5 * x) + 0.5


def _lru_kernel(lx_ref, lg_ref, cw_ref, cb_ref, wa_ref, ba_ref, wx_ref, bx_ref, lam_ref, o_ref,
                xs, a_s, u_s, h_s, *, tc, cb_width, conv_w):
    ti = pl.program_id(1)
    halo_rows = (conv_w - 1) * SUBLANES
    n_slab = cb_width // LANES
    rows = tc * SUBLANES
    slab = lambda v, s: v[:, s * LANES:(s + 1) * LANES]

    @pl.when(ti == 0)
    def _():
        xs[:, 0:halo_rows, :] = jnp.zeros((n_slab, halo_rows, LANES), F32)
        h_s[...] = jnp.zeros(h_s.shape, F32)

    for b in range(SUBLANES):
        for s in range(n_slab):
            xs[s, pl.ds(halo_rows + b, tc, stride=SUBLANES), :] = lx_ref[b, :, s * LANES:(s + 1) * LANES]
    y_parts = []
    for s in range(n_slab):
        cs = slice(s * LANES, (s + 1) * LANES)
        y = cb_ref[:, cs] + cw_ref[0:1, cs] * xs[s, 0:rows, :]
        for j in range(1, conv_w):
            y = y + cw_ref[j:j + 1, cs] * xs[s, j * SUBLANES:j * SUBLANES + rows, :]
        y_parts.append(y)
        xs[s, 0:halo_rows, :] = xs[s, rows:rows + halo_rows, :]
    y2 = jnp.concatenate(y_parts, axis=1)
    yb = y2.astype(BF16)
    r_parts, i_parts = [], []
    for n in range(cb_width // LRU_BLOCK):
        blk = yb[:, n * LRU_BLOCK:(n + 1) * LRU_BLOCK]
        r_parts.append(jnp.dot(blk, wa_ref[n], preferred_element_type=F32))
        i_parts.append(jnp.dot(blk, wx_ref[n], preferred_element_type=F32))
    r = _sigmoid(jnp.concatenate(r_parts, axis=1) + ba_ref[...])
    g_in = _sigmoid(jnp.concatenate(i_parts, axis=1) + bx_ref[...])

    z = -lam_ref[...]
    softplus = jnp.maximum(z, 0.0) + jnp.log1p(jnp.exp(-jnp.abs(z)))
    log_a = (-LRU_C) * r * softplus
    a = jnp.exp(log_a)
    mult = jnp.sqrt(-jnp.tanh(log_a) * (a * a + 1.0))
    gx = g_in * y2
    u = mult * gx
    for s in range(n_slab):
        a_s[s] = slab(a, s)
        u_s[s] = slab(u, s)

    @pl.when(ti == 0)
    def _():
        for s in range(n_slab):
            u_s[s, 0:SUBLANES, :] = slab(gx, s)[0:SUBLANES]

    def step(t, h):
        r0 = pl.multiple_of(t * SUBLANES, SUBLANES)
        new = []
        for s in range(n_slab):
            hs = a_s[s, pl.ds(r0, SUBLANES), :] * h[s] + u_s[s, pl.ds(r0, SUBLANES), :]
            u_s[s, pl.ds(r0, SUBLANES), :] = hs
            new.append(hs)
        return tuple(new)

    h_fin = lax.fori_loop(0, tc, step, tuple(h_s[s] for s in range(n_slab)), unroll=8)
    for s in range(n_slab):
        h_s[s] = h_fin[s]
    for b in range(SUBLANES):
        for s in range(n_slab):
            cs = slice(s * LANES, (s + 1) * LANES)
            h_b = u_s[s, pl.ds(b, tc, stride=SUBLANES), :]
            o_ref[b, :, cs] = (_gelu_tanh(lg_ref[b, :, cs]) * h_b).astype(o_ref.dtype)


def _rg_lru(lxg, conv_w, conv_b, gate_a_w, gate_a_b, gate_x_w, gate_x_b, lam, S, B, tc, cbw):
    d_rnn = conv_w.shape[1]
    width = conv_w.shape[0]
    ncb = d_rnn // cbw
    nblk = cbw // LRU_BLOCK
    n_slab = cbw // LANES
    row = lambda v: v.reshape(1, d_rnn)
    kern = functools.partial(_lru_kernel, tc=tc, cb_width=cbw, conv_w=width)
    vec_spec = pl.BlockSpec((1, cbw), lambda n, t: (0, n))
    return pl.pallas_call(
        kern,
        grid=(ncb, S // tc),
        in_specs=[pl.BlockSpec((B, tc, cbw), lambda n, t: (0, t, n)),
                  pl.BlockSpec((B, tc, cbw), lambda n, t: (0, t, ncb + n)),
                  pl.BlockSpec((width, cbw), lambda n, t: (0, n)),
                  vec_spec,
                  pl.BlockSpec((nblk, LRU_BLOCK, LRU_BLOCK), lambda n, t: (n, 0, 0)),
                  vec_spec,
                  pl.BlockSpec((nblk, LRU_BLOCK, LRU_BLOCK), lambda n, t: (n, 0, 0)),
                  vec_spec,
                  vec_spec],
        out_specs=pl.BlockSpec((B, tc, cbw), lambda n, t: (0, t, n)),
        out_shape=jax.ShapeDtypeStruct((B, S, d_rnn), BF16),
        scratch_shapes=[pltpu.VMEM((n_slab, (tc + width - 1) * B, LANES), F32),
                        pltpu.VMEM((n_slab, tc * B, LANES), F32),
                        pltpu.VMEM((n_slab, tc * B, LANES), F32),
                        pltpu.VMEM((n_slab, B, LANES), F32)],
        compiler_params=_params(2),
        name="rg_lru",
    )(lxg, lxg, conv_w, row(conv_b), gate_a_w.astype(BF16), row(gate_a_b),
      gate_x_w.astype(BF16), row(gate_x_b), row(lam))


def _far_bucket():
    max_exact = REL_BUCKETS // 2
    large = max_exact + int(math.log(REL_MAX_DIST / max_exact) / math.log(REL_MAX_DIST / max_exact)
                            * (REL_BUCKETS - max_exact))
    return min(large, REL_BUCKETS - 1)


def _bias_tile_kernel(rb_ref, o_ref):
    d = pl.program_id(0)
    h = pl.program_id(1)
    rows = lax.broadcasted_iota(jnp.int32, (TQ, CK), 0)
    cols = lax.broadcasted_iota(jnp.int32, (TQ, CK), 1)
    rel = d * BIAS_TILE_STEP + rows - cols
    max_exact = REL_BUCKETS // 2
    nf = jnp.maximum(rel, 1).astype(F32)
    large = max_exact + (jnp.log(nf / max_exact) / math.log(REL_MAX_DIST / max_exact)
                         * (REL_BUCKETS - max_exact)).astype(jnp.int32)
    large = jnp.minimum(large, REL_BUCKETS - 1)
    bucket = jnp.where(rel < max_exact, rel, large)
    acc = jnp.zeros((TQ, CK), F32)
    for b in range(REL_BUCKETS):
        acc = jnp.where(bucket == b, rb_ref[b, h], acc)
    o_ref[...] = (acc - rb_ref[_far_bucket(), h]) * LOG2E


def _bias_tiles(rel_bias):
    return pl.pallas_call(
        _bias_tile_kernel,
        grid=(N_BIAS_TILES, N_HEADS),
        in_specs=[pl.BlockSpec(memory_space=pltpu.SMEM)],
        out_specs=pl.BlockSpec((None, None, TQ, CK), lambda d, h: (d, h, 0, 0)),
        out_shape=jax.ShapeDtypeStruct((N_BIAS_TILES, N_HEADS, TQ, CK), F32),
        compiler_params=_params(2),
        name="t5_bias_tiles",
    )(rel_bias)


I16_MIN = -(2 ** 15)
I16_MAX = 2 ** 15 - 1


def _ordered_key(score):
    bits = pltpu.bitcast(score, jnp.int32)
    return jnp.where(bits >= 0, bits, bits ^ jnp.int32(0x7FFFFFFF))


def _count_ge(arr, n_pairs, cand):
    cand16 = jnp.broadcast_to(cand, (TQ, LANES)).astype(jnp.int16)
    cand16 = jnp.concatenate([cand16] * (CK // LANES), axis=1)
    one, zero = jnp.int16(1), jnp.int16(0)

    def count_pair(p, cnt):
        cnt = cnt + jnp.where(arr[p, 0] >= cand16, one, zero)
        return cnt + jnp.where(arr[p, 1] >= cand16, one, zero)

    cnt = lax.fori_loop(0, n_pairs, count_pair, jnp.zeros((TQ, CK), jnp.int16))
    return jnp.sum(cnt.astype(jnp.int32).astype(F32), axis=1, keepdims=True)


def _kth_largest_i16(arr, n_pairs, need):
    def bit_step(step, thr):
        cand = thr + lax.shift_left(jnp.int32(1), 15 - step)
        return jnp.where(_count_ge(arr, n_pairs, cand) >= need, cand, thr)

    return lax.fori_loop(0, 16, bit_step, jnp.full((TQ, 1), I16_MIN, jnp.int32))


def _attn_kernel(far_ref, qi_ref, kvw_ref, q_ref, bias_ref, kg_ref, kb_ref, o_ref,
                 kin_t, k_t, v_aug, keys, hi_s, lo_s, qis, wbuf, qs, pbuf, m_s, accl_s,
                 *, top_k, seq):
    i = pl.program_id(1)
    k_off, v_off, kw_off = 0, HEAD_DIM, 2 * HEAD_DIM
    n_all = seq // CK
    lane = lax.broadcasted_iota(jnp.int32, (TQ, LANES), 1)

    @pl.when(i == 0)
    def _():
        kraw = kvw_ref[:, kw_off:kw_off + IDX_DIM]
        mu = jnp.mean(kraw, axis=-1, keepdims=True)
        var = jnp.mean(jnp.square(kraw - mu), axis=-1, keepdims=True)
        kn = (kraw - mu) * lax.rsqrt(var + LN_EPS) * kg_ref[...] + kb_ref[...]
        kn = jnp.concatenate([kn, jnp.zeros((seq, LANES - IDX_DIM), F32)], axis=1)
        sub = lax.broadcasted_iota(jnp.int32, (HEAD_DIM, CK), 0)
        ones_rows = jnp.where(sub < 2, 1.0, 0.0).astype(BF16)
        for c in range(n_all):
            kin_t[c] = kn[c * CK:(c + 1) * CK, :].T[0:IDX_DIM].astype(BF16)
            k_t[c, 0:HEAD_DIM, :] = kvw_ref[c * CK:(c + 1) * CK, k_off:k_off + HEAD_DIM].T.astype(BF16)
            k_t[c, HEAD_DIM:2 * HEAD_DIM, :] = ones_rows
        v_aug[:, 0:HEAD_DIM] = kvw_ref[:, v_off:v_off + HEAD_DIM].astype(BF16)
        v_aug[:, HEAD_DIM:2 * HEAD_DIM] = jnp.ones((seq, HEAD_DIM), BF16)
        for h in range(N_HEADS):
            fb = jnp.full((TQ, LANES), far_ref[h] * LOG2E, F32)
            fb_hi = fb.astype(BF16).astype(F32)
            qs[h * TQ:(h + 1) * TQ, HEAD_DIM:2 * HEAD_DIM] = jnp.where(
                lane == 0, fb_hi, jnp.where(lane == 1, fb - fb_hi, 0.0)).astype(BF16)

    q0 = pl.multiple_of(i * TQ, TQ)
    n_chunks = (q0 + TQ + CK - 1) // CK
    n_pairs = (n_chunks + 1) // 2
    rows = q0 + lax.broadcasted_iota(jnp.int32, (TQ, CK), 0)
    cols = lax.broadcasted_iota(jnp.int32, (TQ, CK), 1)

    qib = qi_ref[...].astype(BF16)
    wi = kvw_ref[pl.ds(q0, TQ), kw_off + IDX_DIM:kw_off + IDX_DIM + IDX_HEADS]
    wi = wi * ((IDX_DIM ** -0.5) * (IDX_HEADS ** -0.5))
    for h in range(IDX_HEADS):
        qis[h * TQ:(h + 1) * TQ, :] = qib[:, h * IDX_DIM:(h + 1) * IDX_DIM]
        wbuf[h] = jnp.broadcast_to(wi[:, h:h + 1], (TQ, LANES))

    def score_chunk(c, carry):
        kc = kin_t[c]
        acc = jnp.zeros((TQ, CK), F32)
        for g in range(IDX_HEADS // HEAD_GROUP):
            g0 = g * HEAD_GROUP * TQ
            d_g = jnp.dot(qis[g0:g0 + HEAD_GROUP * TQ, :], kc, preferred_element_type=F32)
            for hh in range(HEAD_GROUP):
                w_h = wbuf[g * HEAD_GROUP + hh]
                acc = acc + jnp.concatenate([w_h] * (CK // LANES), axis=1) * jnp.maximum(
                    d_g[hh * TQ:(hh + 1) * TQ], 0.0)
        key = jnp.where(c * CK + cols <= rows, _ordered_key(acc), jnp.int32(INT_MIN))
        keys[c] = key
        hi_s[c // 2, c % 2] = lax.shift_right_arithmetic(key, 16).astype(jnp.int16)
        lo_s[c // 2, c % 2] = ((key & 0xFFFF) + I16_MIN).astype(jnp.int16)
        return carry

    lax.fori_loop(0, n_chunks, score_chunk, 0)

    @pl.when(n_chunks % 2 == 1)
    def _():
        hi_s[n_chunks // 2, 1] = jnp.full((TQ, CK), I16_MIN, jnp.int16)
        lo_s[n_chunks // 2, 1] = jnp.full((TQ, CK), I16_MIN, jnp.int16)

    need = jnp.full((TQ, 1), float(top_k), F32)
    hi_k = _kth_largest_i16(hi_s, n_pairs, need)
    above = jnp.where(hi_k == I16_MAX, 0.0,
                      _count_ge(hi_s, n_pairs, jnp.minimum(hi_k + 1, I16_MAX)))
    hi_k16 = jnp.broadcast_to(hi_k, (TQ, LANES)).astype(jnp.int16)
    hi_k16 = jnp.concatenate([hi_k16] * (CK // LANES), axis=1)

    def keep_group(p, carry):
        for j in range(2):
            lo_s[p, j] = jnp.where(hi_s[p, j] == hi_k16, lo_s[p, j], jnp.int16(I16_MIN))
        return carry

    lax.fori_loop(0, n_pairs, keep_group, 0)
    lo_k = _kth_largest_i16(lo_s, n_pairs, need - above)
    thr = hi_k * 65536 + (lo_k - I16_MIN)
    thr = jnp.maximum(thr, jnp.int32(INT_MIN + 1))

    for h in range(N_HEADS):
        qs[h * TQ:(h + 1) * TQ, 0:HEAD_DIM] = q_ref[:, h * HEAD_DIM:(h + 1) * HEAD_DIM]
    m_s[...] = jnp.full(m_s.shape, 0.5 * MASK_LOGIT, F32)
    accl_s[...] = jnp.zeros(accl_s.shape, F32)

    def attn_chunk(c, carry, *, near):
        kc = k_t[c]
        vc = v_aug[pl.ds(pl.multiple_of(c * CK, CK), CK), :]
        sel = keys[c] >= thr
        tile = (q0 - c * CK) // BIAS_TILE_STEP
        for g in range(N_HEADS // HEAD_GROUP):
            g0 = g * HEAD_GROUP * TQ
            s_g = jnp.dot(qs[g0:g0 + HEAD_GROUP * TQ, :], kc, preferred_element_type=F32)
            alphas = []
            for hh in range(HEAD_GROUP):
                h = g * HEAD_GROUP + hh
                s = s_g[hh * TQ:(hh + 1) * TQ]
                if near:
                    s = s + bias_ref[tile, h]
                s = jnp.where(sel, s, MASK_LOGIT)
                m_old = m_s[h]
                m_new = jnp.maximum(m_old, jnp.max(s, axis=1, keepdims=True))
                p = jnp.exp2(s - jnp.concatenate([m_new] * (CK // LANES), axis=1))
                alphas.append(jnp.exp2(m_old - m_new))
                m_s[h] = m_new
                pbuf[h * TQ:(h + 1) * TQ, :] = p.astype(BF16)
            pv = jnp.dot(pbuf[g0:g0 + HEAD_GROUP * TQ, :], vc, preferred_element_type=F32)
            for hh in range(HEAD_GROUP):
                h = g * HEAD_GROUP + hh
                accl_s[h] = (jnp.concatenate([alphas[hh], alphas[hh]], axis=1) * accl_s[h]
                             + pv[hh * TQ:(hh + 1) * TQ])
        return carry

    n_far = jnp.maximum((q0 - N_BIAS_TILES * BIAS_TILE_STEP) // CK + 1, 0)
    lax.fori_loop(0, n_far, functools.partial(attn_chunk, near=False), 0)
    lax.fori_loop(n_far, n_chunks, functools.partial(attn_chunk, near=True), 0)
    for h in range(N_HEADS):
        al = accl_s[h]
        o_ref[:, h * HEAD_DIM:(h + 1) * HEAD_DIM] = (
            al[:, 0:HEAD_DIM] / al[:, HEAD_DIM:2 * HEAD_DIM]).astype(o_ref.dtype)


def _sparse_attention(qi, kvw, q, bias_tiles, far_bias, knorm_g, knorm_b, top_k):
    B, S, q_w = q.shape
    qi_w = qi.shape[2]
    kvw_w = kvw.shape[2]
    n_all = S // CK
    assert n_all % 2 == 0
    kern = functools.partial(_attn_kernel, top_k=top_k, seq=S)
    return pl.pallas_call(
        kern,
        grid=(B, S // TQ),
        in_specs=[pl.BlockSpec(memory_space=pltpu.SMEM),
                  pl.BlockSpec((None, TQ, qi_w), lambda b, i: (b, i, 0)),
                  pl.BlockSpec((None, S, kvw_w), lambda b, i: (b, 0, 0)),
                  pl.BlockSpec((None, TQ, q_w), lambda b, i: (b, i, 0)),
                  pl.BlockSpec((N_BIAS_TILES, N_HEADS, TQ, CK), lambda b, i: (0, 0, 0, 0),
                               pipeline_mode=pl.Buffered(1)),
                  pl.BlockSpec((1, IDX_DIM), lambda b, i: (0, 0)),
                  pl.BlockSpec((1, IDX_DIM), lambda b, i: (0, 0))],
        out_specs=pl.BlockSpec((None, TQ, q_w), lambda b, i: (b, i, 0)),
        out_shape=jax.ShapeDtypeStruct((B, S, q_w), BF16),
        scratch_shapes=[pltpu.VMEM((n_all, IDX_DIM, CK), BF16),
                        pltpu.VMEM((n_all, 2 * HEAD_DIM, CK), BF16),
                        pltpu.VMEM((S, 2 * HEAD_DIM), BF16),
                        pltpu.VMEM((n_all, TQ, CK), jnp.int32),
                        pltpu.VMEM((n_all // 2, 2, TQ, CK), jnp.int16),
                        pltpu.VMEM((n_all // 2, 2, TQ, CK), jnp.int16),
                        pltpu.VMEM((IDX_HEADS * TQ, IDX_DIM), BF16),
                        pltpu.VMEM((IDX_HEADS, TQ, LANES), F32),
                        pltpu.VMEM((N_HEADS * TQ, 2 * HEAD_DIM), BF16),
                        pltpu.VMEM((N_HEADS * TQ, CK), BF16),
                        pltpu.VMEM((N_HEADS, TQ, LANES), F32),
                        pltpu.VMEM((N_HEADS, TQ, 2 * HEAD_DIM), F32)],
        compiler_params=_params(2),
        name="dsa_attention",
    )(far_bias, qi, kvw, q, bias_tiles, knorm_g.reshape(1, IDX_DIM), knorm_b.reshape(1, IDX_DIM))


def _merge_kernel(yl_ref, ya_ref, wl_ref, wa_ref, gl_ref, ga_ref, o_ref):
    pl_ = jnp.dot(yl_ref[...], wl_ref[...], preferred_element_type=F32)
    pa_ = jnp.dot(ya_ref[...], wa_ref[...], preferred_element_type=F32)
    merged = (_sigmoid(gl_ref[...].astype(F32)) * pl_
              + _sigmoid(ga_ref[...].astype(F32)) * pa_)
    o_ref[...] = merged.astype(o_ref.dtype)


def _gated_merge(y_lru, y_att, w_l, w_a, gates, tm, tn):
    M, D = y_lru.shape
    N = w_l.shape[1]
    nj = N // tn
    return pl.pallas_call(
        _merge_kernel,
        grid=(M // tm, nj),
        in_specs=[pl.BlockSpec((tm, D), lambda i, j: (i, 0)),
                  pl.BlockSpec((tm, D), lambda i, j: (i, 0)),
                  pl.BlockSpec((D, tn), lambda i, j: (0, j)),
                  pl.BlockSpec((D, tn), lambda i, j: (0, j)),
                  pl.BlockSpec((tm, tn), lambda i, j: (i, j)),
                  pl.BlockSpec((tm, tn), lambda i, j: (i, nj + j))],
        out_specs=pl.BlockSpec((tm, tn), lambda i, j: (i, j)),
        out_shape=jax.ShapeDtypeStruct((M, N), BF16),
        compiler_params=_params(2),
        name="gated_merge",
    )(y_lru, y_att, w_l, w_a, gates, gates)


def _layer_norm_rows(v, g, b):
    mu = jnp.mean(v, axis=-1, keepdims=True)
    var = jnp.mean(jnp.square(v - mu), axis=-1, keepdims=True)
    return (v - mu) * lax.rsqrt(var + LN_EPS) * g + b


def _outproj_ln_kernel(m_ref, x_ref, w_ref, g_ref, b_ref, o32_ref, o16_ref):
    proj = jnp.dot(m_ref[...], w_ref[...], preferred_element_type=F32)
    h = _layer_norm_rows(ALPHA * x_ref[...] + proj, g_ref[...], b_ref[...])
    o32_ref[...] = h
    o16_ref[...] = h.astype(BF16)


def _outproj_ln(merged, x2d, w_out, g, b, tm):
    M, D = x2d.shape
    return pl.pallas_call(
        _outproj_ln_kernel,
        grid=(M // tm,),
        in_specs=[pl.BlockSpec((tm, D), lambda i: (i, 0)),
                  pl.BlockSpec((tm, D), lambda i: (i, 0)),
                  pl.BlockSpec((D, D), lambda i: (0, 0)),
                  pl.BlockSpec((1, D), lambda i: (0, 0)),
                  pl.BlockSpec((1, D), lambda i: (0, 0))],
        out_specs=[pl.BlockSpec((tm, D), lambda i: (i, 0)),
                   pl.BlockSpec((tm, D), lambda i: (i, 0))],
        out_shape=[jax.ShapeDtypeStruct((M, D), F32),
                   jax.ShapeDtypeStruct((M, D), BF16)],
        compiler_params=_params(1),
        name="outproj_ln1",
    )(merged, x2d, w_out, g.reshape(1, D), b.reshape(1, D))


def _ffn_up_kernel(halo_ref, h_ref, wg_ref, wv_ref, cwg_ref, cwv_ref, cbg_ref, cbv_ref, o_ref,
                   *, tm, conv_w, tiles_per_seq):
    i = pl.program_id(0)
    keep = jnp.where(i % tiles_per_seq > 0, 1.0, 0.0).astype(BF16)
    lhs = jnp.concatenate([halo_ref[...] * keep, h_ref[...]], axis=0)
    base = SUBLANES - (conv_w - 1)

    def conv_half(w_ref, cw_ref, cb_ref):
        up = jnp.dot(lhs, w_ref[...], preferred_element_type=F32)
        y = cb_ref[...] + cw_ref[0:1, :] * up[base:base + tm]
        for j in range(1, conv_w):
            y = y + cw_ref[j:j + 1, :] * up[base + j:base + j + tm]
        return y

    yg = conv_half(wg_ref, cwg_ref, cbg_ref)
    yv = conv_half(wv_ref, cwv_ref, cbv_ref)
    o_ref[...] = (_gelu_tanh(yg) * yv).astype(o_ref.dtype)


def _ffn_up(h16, w_up, conv_w, conv_b, S, tm, tn):
    M, D = h16.shape
    d_ff = w_up.shape[1] // 2
    width = conv_w.shape[0]
    assert width - 1 <= SUBLANES and S % tm == 0
    nj = d_ff // tn
    kern = functools.partial(_ffn_up_kernel, tm=tm, conv_w=width, tiles_per_seq=S // tm)
    halo_blocks = tm // SUBLANES
    cb2 = conv_b.reshape(1, 2 * d_ff)
    return pl.pallas_call(
        kern,
        grid=(M // tm, nj),
        in_specs=[pl.BlockSpec((SUBLANES, D), lambda i, j: (jnp.maximum(i * halo_blocks - 1, 0), 0)),
                  pl.BlockSpec((tm, D), lambda i, j: (i, 0)),
                  pl.BlockSpec((D, tn), lambda i, j: (0, j)),
                  pl.BlockSpec((D, tn), lambda i, j: (0, nj + j)),
                  pl.BlockSpec((width, tn), lambda i, j: (0, j)),
                  pl.BlockSpec((width, tn), lambda i, j: (0, nj + j)),
                  pl.BlockSpec((1, tn), lambda i, j: (0, j)),
                  pl.BlockSpec((1, tn), lambda i, j: (0, nj + j))],
        out_specs=pl.BlockSpec((tm, tn), lambda i, j: (i, j)),
        out_shape=jax.ShapeDtypeStruct((M, d_ff), BF16),
        compiler_params=_params(2),
        name="ffn_up_conv_geglu",
    )(h16, h16, w_up, w_up, conv_w, conv_w, cb2, cb2)


def _ffn_down_kernel(a_ref, h_ref, w_ref, g_ref, b_ref, o_ref):
    k = pl.program_id(1)

    @pl.when(k == 0)
    def _():
        o_ref[...] = jnp.dot(a_ref[...], w_ref[...], preferred_element_type=F32)

    @pl.when(k > 0)
    def _():
        o_ref[...] += jnp.dot(a_ref[...], w_ref[...], preferred_element_type=F32)

    @pl.when(k == pl.num_programs(1) - 1)
    def _():
        o_ref[...] = _layer_norm_rows(ALPHA * h_ref[...] + o_ref[...], g_ref[...], b_ref[...])


def _ffn_down_ln(act, h32, w_down, g, b, tm, tk):
    d_ff, D = w_down.shape
    M = act.shape[0]
    return pl.pallas_call(
        _ffn_down_kernel,
        grid=(M // tm, d_ff // tk),
        in_specs=[pl.BlockSpec((tm, tk), lambda i, k: (i, k)),
                  pl.BlockSpec((tm, D), lambda i, k: (i, 0)),
                  pl.BlockSpec((tk, D), lambda i, k: (k, 0)),
                  pl.BlockSpec((1, D), lambda i, k: (0, 0)),
                  pl.BlockSpec((1, D), lambda i, k: (0, 0))],
        out_specs=pl.BlockSpec((tm, D), lambda i, k: (i, 0)),
        out_shape=jax.ShapeDtypeStruct((M, D), F32),
        compiler_params=_params(2),
        name="ffn_down_ln2",
    )(act, h32, w_down, g.reshape(1, D), b.reshape(1, D))


def kernel(x, w_in, lru_conv_w, lru_conv_b, lru_gate_a_w, lru_gate_a_b, lru_gate_x_w, lru_gate_x_b,
           lru_lambda, idx_knorm_g, idx_knorm_b, rel_bias, w_proj_lru, w_proj_attn, w_out,
           ln1_g, ln1_b, ffn_w_up, ffn_conv_w, ffn_conv_b, ffn_w_down, ln2_g, ln2_b):
    B, S, D = x.shape
    assert B == SUBLANES and S % CK == 0 and w_in.shape[0] == DEPTH
    top_k = min(TOPK_MAX, S // 4)
    d_rnn = lru_conv_w.shape[-1]
    q_w = N_HEADS * HEAD_DIM
    qi_w = IDX_HEADS * IDX_DIM
    splits = (d_rnn, d_rnn, q_w, HEAD_DIM, HEAD_DIM, qi_w, IDX_DIM, IDX_HEADS, D, D)
    offs = [0]
    for s_ in splits:
        offs.append(offs[-1] + s_)
    M = B * S

    l = 0
    w = w_in[l].astype(BF16)
    x2d = x.reshape(M, D)
    x16 = x2d.astype(BF16)
    w_lru = w[:, offs[0]:offs[2]]
    w_q = w[:, offs[2]:offs[3]]
    pad = LANES - IDX_DIM - IDX_HEADS
    w_kvw = jnp.concatenate([w[:, offs[3]:offs[5]], w[:, offs[6]:offs[8]],
                             jnp.zeros((D, pad), BF16)], axis=1)
    w_qi = w[:, offs[5]:offs[6]]
    w_g = w[:, offs[8]:offs[10]]

    tm_in = min(M, 2048)
    lxg = _project(x16, w_lru, F32, tm_in, 1024).reshape(B, S, 2 * d_rnn)
    q = _project(x16, w_q, BF16, tm_in, 1024,
                 out_scale=(HEAD_DIM ** -0.5) * LOG2E).reshape(B, S, q_w)
    kvw = _project(x16, w_kvw, F32, tm_in, w_kvw.shape[1]).reshape(B, S, w_kvw.shape[1])
    qi = _project(x16, w_qi, F32, tm_in, 1024).reshape(B, S, qi_w)
    gates = _project(x16, w_g, BF16, tm_in, 1024)

    y_lru = _rg_lru(lxg, lru_conv_w[l], lru_conv_b[l], lru_gate_a_w[l], lru_gate_a_b[l],
                    lru_gate_x_w[l], lru_gate_x_b[l], lru_lambda[l], S, B, tc=min(S, 256), cbw=256)

    tiles = _bias_tiles(rel_bias)
    far_bias = rel_bias[_far_bucket()]
    y_att = _sparse_attention(qi, kvw, q, tiles, far_bias, idx_knorm_g[l], idx_knorm_b[l], top_k)

    merged = _gated_merge(y_lru.reshape(M, d_rnn), y_att.reshape(M, q_w),
                          w_proj_lru[l].astype(BF16), w_proj_attn[l].astype(BF16),
                          gates, tm=min(M, 1024), tn=512)
    h32, h16 = _outproj_ln(merged, x2d, w_out[l].astype(BF16), ln1_g[l], ln1_b[l], tm=512)
    act = _ffn_up(h16, ffn_w_up[l].astype(BF16), ffn_conv_w[l], ffn_conv_b[l], S,
                  tm=min(S, 1024), tn=512)
    out = _ffn_down_ln(act, h32, ffn_w_down[l].astype(BF16), ln2_g[l], ln2_b[l],
                       tm=min(M, 1024), tk=1024)
    return out.reshape(B, S, D)
```

```python
import functools
import math

import jax
import jax.numpy as jnp
from jax import lax
from jax.experimental import pallas as pl
from jax.experimental.pallas import tpu as pltpu

N_HEADS = 16
HEAD_DIM = 128
IDX_HEADS = 16
IDX_DIM = 64
TOPK_MAX = 256
LRU_BLOCK = 128
LRU_C = 8.0
REL_BUCKETS = 32
REL_MAX_DIST = 128
LN_EPS = 1e-5
DEPTH = 1
ALPHA = (2.0 * DEPTH) ** 0.25

LANES = 128
SUBLANES = 8
VMEM_LIMIT_BYTES = 56 * 1024 * 1024

TQ = 256
CK = 256
HEAD_GROUP = 8
BIAS_TILE_STEP = 256
N_BIAS_TILES = 2
assert TQ % BIAS_TILE_STEP == 0 and CK % BIAS_TILE_STEP == 0
assert N_BIAS_TILES * BIAS_TILE_STEP - (CK - 1) >= REL_MAX_DIST

INT_MIN = -(2 ** 31)
MASK_LOGIT = -2e30
LOG2E = math.log2(math.e)
F32 = jnp.float32
BF16 = jnp.bfloat16


def _params(n_axes):
    return pltpu.CompilerParams(dimension_semantics=("arbitrary",) * n_axes,
                                vmem_limit_bytes=VMEM_LIMIT_BYTES)


def _matmul_kernel(x_ref, w_ref, o_ref, *, out_scale):
    acc = jnp.dot(x_ref[...], w_ref[...], preferred_element_type=F32)
    if out_scale != 1.0:
        acc = acc * out_scale
    o_ref[...] = acc.astype(o_ref.dtype)


def _project(x2d, w, out_dtype, tm, tn, out_scale=1.0):
    M, K = x2d.shape
    N = w.shape[1]
    return pl.pallas_call(
        functools.partial(_matmul_kernel, out_scale=out_scale),
        grid=(M // tm, N // tn),
        in_specs=[pl.BlockSpec((tm, K), lambda i, j: (i, 0)),
                  pl.BlockSpec((K, tn), lambda i, j: (0, j))],
        out_specs=pl.BlockSpec((tm, tn), lambda i, j: (i, j)),
        out_shape=jax.ShapeDtypeStruct((M, N), out_dtype),
        compiler_params=_params(2),
        name="in_proj",
    )(x2d, w)


def _matmul_cast_kernel(x_ref, w_ref, o_ref, x16_ref):
    @pl.when(pl.program_id(1) == 0)
    def _():
        x16_ref[...] = x_ref[...].astype(BF16)

    o_ref[...] = jnp.dot(x16_ref[...], w_ref[...], preferred_element_type=F32).astype(o_ref.dtype)


def _project_and_cast(x2d, w, out_dtype, tm, tn):
    M, K = x2d.shape
    N = w.shape[1]
    return pl.pallas_call(
        _matmul_cast_kernel,
        grid=(M // tm, N // tn),
        in_specs=[pl.BlockSpec((tm, K), lambda i, j: (i, 0)),
                  pl.BlockSpec((K, tn), lambda i, j: (0, j))],
        out_specs=[pl.BlockSpec((tm, tn), lambda i, j: (i, j)),
                   pl.BlockSpec((tm, K), lambda i, j: (i, 0))],
        out_shape=[jax.ShapeDtypeStruct((M, N), out_dtype),
                   jax.ShapeDtypeStruct((M, K), BF16)],
        compiler_params=_params(2),
        name="in_proj_cast",
    )(x2d, w)


def _gelu_tanh(x):
    return 0.5 * x * (1.0 + jnp.tanh(math.sqrt(2.0 / math.pi) * (x + 0.044715 * (x * x * x))))


def _sigmoid(x):
    return 0.5 * jnp.tanh(0.5 * x) + 0.5


def _lru_kernel(lx_ref, lg_ref, cw_ref, cb_ref, wa_ref, ba_ref, wx_ref, bx_ref, lam_ref, o_ref,
                xs, a_s, u_s, h_s, *, tc, cb_width, conv_w):
    ti = pl.program_id(1)
    halo_rows = (conv_w - 1) * SUBLANES
    n_slab = cb_width // LANES
    rows = tc * SUBLANES
    slab = lambda v, s: v[:, s * LANES:(s + 1) * LANES]

    @pl.when(ti == 0)
    def _():
        xs[:, 0:halo_rows, :] = jnp.zeros((n_slab, halo_rows, LANES), F32)
        h_s[...] = jnp.zeros(h_s.shape, F32)

    for b in range(SUBLANES):
        for s in range(n_slab):
            xs[s, pl.ds(halo_rows + b, tc, stride=SUBLANES), :] = lx_ref[b, :, s * LANES:(s + 1) * LANES]
    y_parts = []
    for s in range(n_slab):
        cs = slice(s * LANES, (s + 1) * LANES)
        y = cb_ref[:, cs] + cw_ref[0:1, cs] * xs[s, 0:rows, :]
        for j in range(1, conv_w):
            y = y + cw_ref[j:j + 1, cs] * xs[s, j * SUBLANES:j * SUBLANES + rows, :]
        y_parts.append(y)
        xs[s, 0:halo_rows, :] = xs[s, rows:rows + halo_rows, :]
    y2 = jnp.concatenate(y_parts, axis=1)
    yb = y2.astype(BF16)
    r_parts, i_parts = [], []
    for n in range(cb_width // LRU_BLOCK):
        blk = yb[:, n * LRU_BLOCK:(n + 1) * LRU_BLOCK]
        r_parts.append(jnp.dot(blk, wa_ref[n], preferred_element_type=F32))
        i_parts.append(jnp.dot(blk, wx_ref[n], preferred_element_type=F32))
    r = _sigmoid(jnp.concatenate(r_parts, axis=1) + ba_ref[...])
    g_in = _sigmoid(jnp.concatenate(i_parts, axis=1) + bx_ref[...])

    z = -lam_ref[...]
    softplus = jnp.maximum(z, 0.0) + jnp.log1p(jnp.exp(-jnp.abs(z)))
    log_a = (-LRU_C) * r * softplus
    a = jnp.exp(log_a)
    mult = jnp.sqrt(-jnp.tanh(log_a) * (a * a + 1.0))
    gx = g_in * y2
    u = mult * gx
    for s in range(n_slab):
        a_s[s] = slab(a, s)
        u_s[s] = slab(u, s)

    @pl.when(ti == 0)
    def _():
        for s in range(n_slab):
            u_s[s, 0:SUBLANES, :] = slab(gx, s)[0:SUBLANES]

    def step(t, h):
        r0 = pl.multiple_of(t * SUBLANES, SUBLANES)
        new = []
        for s in range(n_slab):
            hs = a_s[s, pl.ds(r0, SUBLANES), :] * h[s] + u_s[s, pl.ds(r0, SUBLANES), :]
            u_s[s, pl.ds(r0, SUBLANES), :] = hs
            new.append(hs)
        return tuple(new)

    h_fin = lax.fori_loop(0, tc, step, tuple(h_s[s] for s in range(n_slab)), unroll=8)
    for s in range(n_slab):
        h_s[s] = h_fin[s]
    for b in range(SUBLANES):
        for s in range(n_slab):
            cs = slice(s * LANES, (s + 1) * LANES)
            h_b = u_s[s, pl.ds(b, tc, stride=SUBLANES), :]
            o_ref[b, :, cs] = (_gelu_tanh(lg_ref[b, :, cs]) * h_b).astype(o_ref.dtype)


def _rg_lru(lxg, conv_w, conv_b, gate_a_w, gate_a_b, gate_x_w, gate_x_b, lam, S, B, tc, cbw):
    d_rnn = conv_w.shape[1]
    width = conv_w.shape[0]
    ncb = d_rnn // cbw
    nblk = cbw // LRU_BLOCK
    n_slab = cbw // LANES
    row = lambda v: v.reshape(1, d_rnn)
    kern = functools.partial(_lru_kernel, tc=tc, cb_width=cbw, conv_w=width)
    vec_spec = pl.BlockSpec((1, cbw), lambda n, t: (0, n))
    return pl.pallas_call(
        kern,
        grid=(ncb, S // tc),
        in_specs=[pl.BlockSpec((B, tc, cbw), lambda n, t: (0, t, n)),
                  pl.BlockSpec((B, tc, cbw), lambda n, t: (0, t, ncb + n)),
                  pl.BlockSpec((width, cbw), lambda n, t: (0, n)),
                  vec_spec,
                  pl.BlockSpec((nblk, LRU_BLOCK, LRU_BLOCK), lambda n, t: (n, 0, 0)),
                  vec_spec,
                  pl.BlockSpec((nblk, LRU_BLOCK, LRU_BLOCK), lambda n, t: (n, 0, 0)),
                  vec_spec,
                  vec_spec],
        out_specs=pl.BlockSpec((B, tc, cbw), lambda n, t: (0, t, n)),
        out_shape=jax.ShapeDtypeStruct((B, S, d_rnn), BF16),
        scratch_shapes=[pltpu.VMEM((n_slab, (tc + width - 1) * B, LANES), F32),
                        pltpu.VMEM((n_slab, tc * B, LANES), F32),
                        pltpu.VMEM((n_slab, tc * B, LANES), F32),
                        pltpu.VMEM((n_slab, B, LANES), F32)],
        compiler_params=_params(2),
        name="rg_lru",
    )(lxg, lxg, conv_w, row(conv_b), gate_a_w.astype(BF16), row(gate_a_b),
      gate_x_w.astype(BF16), row(gate_x_b), row(lam))


I16_MIN = -(2 ** 15)
I16_MAX = 2 ** 15 - 1


def _far_bucket():
    max_exact = REL_BUCKETS // 2
    large = max_exact + int(math.log(REL_MAX_DIST / max_exact) / math.log(REL_MAX_DIST / max_exact)
                            * (REL_BUCKETS - max_exact))
    return min(large, REL_BUCKETS - 1)


def _fill_bias_tiles(rb_ref, bias_s):
    rows = lax.broadcasted_iota(jnp.int32, (TQ, CK), 0)
    cols = lax.broadcasted_iota(jnp.int32, (TQ, CK), 1)
    max_exact = REL_BUCKETS // 2

    def tile_body(n, carry):
        d = n // N_HEADS
        h = n % N_HEADS
        rel = d * BIAS_TILE_STEP + rows - cols
        nf = jnp.maximum(rel, 1).astype(F32)
        large = max_exact + (jnp.log(nf / max_exact) / math.log(REL_MAX_DIST / max_exact)
                             * (REL_BUCKETS - max_exact)).astype(jnp.int32)
        large = jnp.minimum(large, REL_BUCKETS - 1)
        bucket = jnp.where(rel < max_exact, rel, large)
        acc = jnp.zeros((TQ, CK), F32)
        for b in range(REL_BUCKETS):
            acc = jnp.where(bucket == b, rb_ref[b, h], acc)
        bias_s[d, h] = (acc - rb_ref[_far_bucket(), h]) * LOG2E
        return carry

    lax.fori_loop(0, N_BIAS_TILES * N_HEADS, tile_body, 0)


def _ordered_key(score):
    bits = pltpu.bitcast(score, jnp.int32)
    return jnp.where(bits >= 0, bits, bits ^ jnp.int32(0x7FFFFFFF))


def _count_ge(arr, n_pairs, cand):
    cand16 = jnp.broadcast_to(cand, (TQ, LANES)).astype(jnp.int16)
    cand16 = jnp.concatenate([cand16] * (CK // LANES), axis=1)
    one, zero = jnp.int16(1), jnp.int16(0)

    def count_pair(p, cnt):
        cnt = cnt + jnp.where(arr[p, 0] >= cand16, one, zero)
        return cnt + jnp.where(arr[p, 1] >= cand16, one, zero)

    cnt = lax.fori_loop(0, n_pairs, count_pair, jnp.zeros((TQ, CK), jnp.int16))
    return jnp.sum(cnt.astype(jnp.int32).astype(F32), axis=1, keepdims=True)


def _kth_largest_i16(arr, n_pairs, need):
    def bit_step(step, thr):
        cand = thr + lax.shift_left(jnp.int32(1), 15 - step)
        return jnp.where(_count_ge(arr, n_pairs, cand) >= need, cand, thr)

    return lax.fori_loop(0, 16, bit_step, jnp.full((TQ, 1), I16_MIN, jnp.int32))


def _attn_kernel(rb_ref, qi_ref, kvw_ref, q_ref, kg_ref, kb_ref, o_ref,
                 bias_s, kin_t, k_t, v_aug, keys, hi_s, lo_s, qis, wbuf, qs, pbuf, m_s, accl_s,
                 *, top_k, seq):
    i = pl.program_id(1)
    k_off, v_off, kw_off = 0, HEAD_DIM, 2 * HEAD_DIM
    n_all = seq // CK
    lane = lax.broadcasted_iota(jnp.int32, (TQ, LANES), 1)

    @pl.when((pl.program_id(0) == 0) & (i == 0))
    def _():
        _fill_bias_tiles(rb_ref, bias_s)

    @pl.when(i == 0)
    def _():
        kraw = kvw_ref[:, kw_off:kw_off + IDX_DIM]
        mu = jnp.mean(kraw, axis=-1, keepdims=True)
        var = jnp.mean(jnp.square(kraw - mu), axis=-1, keepdims=True)
        kn = (kraw - mu) * lax.rsqrt(var + LN_EPS) * kg_ref[...] + kb_ref[...]
        kn = jnp.concatenate([kn, jnp.zeros((seq, LANES - IDX_DIM), F32)], axis=1)
        sub = lax.broadcasted_iota(jnp.int32, (HEAD_DIM, CK), 0)
        ones_rows = jnp.where(sub < 2, 1.0, 0.0).astype(BF16)
        for c in range(n_all):
            kin_t[c] = kn[c * CK:(c + 1) * CK, :].T[0:IDX_DIM].astype(BF16)
            k_t[c, 0:HEAD_DIM, :] = kvw_ref[c * CK:(c + 1) * CK, k_off:k_off + HEAD_DIM].T.astype(BF16)
            k_t[c, HEAD_DIM:2 * HEAD_DIM, :] = ones_rows
        v_aug[:, 0:HEAD_DIM] = kvw_ref[:, v_off:v_off + HEAD_DIM].astype(BF16)
        v_aug[:, HEAD_DIM:2 * HEAD_DIM] = jnp.ones((seq, HEAD_DIM), BF16)
        for h in range(N_HEADS):
            fb = jnp.full((TQ, LANES), rb_ref[_far_bucket(), h] * LOG2E, F32)
            fb_hi = fb.astype(BF16).astype(F32)
            qs[h * TQ:(h + 1) * TQ, HEAD_DIM:2 * HEAD_DIM] = jnp.where(
                lane == 0, fb_hi, jnp.where(lane == 1, fb - fb_hi, 0.0)).astype(BF16)

    q0 = pl.multiple_of(i * TQ, TQ)
    n_chunks = (q0 + TQ + CK - 1) // CK
    n_pairs = (n_chunks + 1) // 2
    rows = q0 + lax.broadcasted_iota(jnp.int32, (TQ, CK), 0)
    cols = lax.broadcasted_iota(jnp.int32, (TQ, CK), 1)

    qib = qi_ref[...].astype(BF16)
    wi = kvw_ref[pl.ds(q0, TQ), kw_off + IDX_DIM:kw_off + IDX_DIM + IDX_HEADS]
    wi = wi * ((IDX_DIM ** -0.5) * (IDX_HEADS ** -0.5))
    for h in range(IDX_HEADS):
        qis[h * TQ:(h + 1) * TQ, :] = qib[:, h * IDX_DIM:(h + 1) * IDX_DIM]
        wbuf[h] = jnp.broadcast_to(wi[:, h:h + 1], (TQ, LANES))

    def score_chunk(c, carry):
        kc = kin_t[c]
        acc = jnp.zeros((TQ, CK), F32)
        for g in range(IDX_HEADS // HEAD_GROUP):
            g0 = g * HEAD_GROUP * TQ
            d_g = jnp.dot(qis[g0:g0 + HEAD_GROUP * TQ, :], kc, preferred_element_type=F32)
            for hh in range(HEAD_GROUP):
                w_h = wbuf[g * HEAD_GROUP + hh]
                acc = acc + jnp.concatenate([w_h] * (CK // LANES), axis=1) * jnp.maximum(
                    d_g[hh * TQ:(hh + 1) * TQ], 0.0)
        key = jnp.where(c * CK + cols <= rows, _ordered_key(acc), jnp.int32(INT_MIN))
        keys[c] = key
        hi_s[c // 2, c % 2] = lax.shift_right_arithmetic(key, 16).astype(jnp.int16)
        lo_s[c // 2, c % 2] = ((key & 0xFFFF) + I16_MIN).astype(jnp.int16)
        return carry

    lax.fori_loop(0, n_chunks, score_chunk, 0)

    @pl.when(n_chunks % 2 == 1)
    def _():
        hi_s[n_chunks // 2, 1] = jnp.full((TQ, CK), I16_MIN, jnp.int16)
        lo_s[n_chunks // 2, 1] = jnp.full((TQ, CK), I16_MIN, jnp.int16)

    need = jnp.full((TQ, 1), float(top_k), F32)
    hi_k = _kth_largest_i16(hi_s, n_pairs, need)
    above = jnp.where(hi_k == I16_MAX, 0.0,
                      _count_ge(hi_s, n_pairs, jnp.minimum(hi_k + 1, I16_MAX)))
    hi_k16 = jnp.broadcast_to(hi_k, (TQ, LANES)).astype(jnp.int16)
    hi_k16 = jnp.concatenate([hi_k16] * (CK // LANES), axis=1)

    def keep_group(p, carry):
        for j in range(2):
            lo_s[p, j] = jnp.where(hi_s[p, j] == hi_k16, lo_s[p, j], jnp.int16(I16_MIN))
        return carry

    lax.fori_loop(0, n_pairs, keep_group, 0)
    lo_k = _kth_largest_i16(lo_s, n_pairs, need - above)
    thr = hi_k * 65536 + (lo_k - I16_MIN)
    thr = jnp.maximum(thr, jnp.int32(INT_MIN + 1))

    for h in range(N_HEADS):
        qs[h * TQ:(h + 1) * TQ, 0:HEAD_DIM] = q_ref[:, h * HEAD_DIM:(h + 1) * HEAD_DIM]
    m_s[...] = jnp.full(m_s.shape, 0.5 * MASK_LOGIT, F32)
    accl_s[...] = jnp.zeros(accl_s.shape, F32)

    def attn_chunk(c, carry, *, near):
        kc = k_t[c]
        vc = v_aug[pl.ds(pl.multiple_of(c * CK, CK), CK), :]
        sel = keys[c] >= thr
        tile = (q0 - c * CK) // BIAS_TILE_STEP
        for g in range(N_HEADS // HEAD_GROUP):
            g0 = g * HEAD_GROUP * TQ
            s_g = jnp.dot(qs[g0:g0 + HEAD_GROUP * TQ, :], kc, preferred_element_type=F32)
            alphas = []
            for hh in range(HEAD_GROUP):
                h = g * HEAD_GROUP + hh
                s = s_g[hh * TQ:(hh + 1) * TQ]
                if near:
                    s = s + bias_s[tile, h]
                s = jnp.where(sel, s, MASK_LOGIT)
                m_old = m_s[h]
                m_new = jnp.maximum(m_old, jnp.max(s, axis=1, keepdims=True))
                p = jnp.exp2(s - jnp.concatenate([m_new] * (CK // LANES), axis=1))
                alphas.append(jnp.exp2(m_old - m_new))
                m_s[h] = m_new
                pbuf[h * TQ:(h + 1) * TQ, :] = p.astype(BF16)
            pv = jnp.dot(pbuf[g0:g0 + HEAD_GROUP * TQ, :], vc, preferred_element_type=F32)
            for hh in range(HEAD_GROUP):
                h = g * HEAD_GROUP + hh
                accl_s[h] = (jnp.concatenate([alphas[hh], alphas[hh]], axis=1) * accl_s[h]
                             + pv[hh * TQ:(hh + 1) * TQ])
        return carry

    n_far = jnp.maximum((q0 - N_BIAS_TILES * BIAS_TILE_STEP) // CK + 1, 0)
    lax.fori_loop(0, n_far, functools.partial(attn_chunk, near=False), 0)
    lax.fori_loop(n_far, n_chunks, functools.partial(attn_chunk, near=True), 0)
    for h in range(N_HEADS):
        al = accl_s[h]
        o_ref[:, h * HEAD_DIM:(h + 1) * HEAD_DIM] = (
            al[:, 0:HEAD_DIM] / al[:, HEAD_DIM:2 * HEAD_DIM]).astype(o_ref.dtype)


def _sparse_attention(qi, kvw, q, rel_bias, knorm_g, knorm_b, top_k):
    B, S, q_w = q.shape
    qi_w = qi.shape[2]
    kvw_w = kvw.shape[2]
    n_all = S // CK
    assert n_all % 2 == 0
    kern = functools.partial(_attn_kernel, top_k=top_k, seq=S)
    return pl.pallas_call(
        kern,
        grid=(B, S // TQ),
        in_specs=[pl.BlockSpec(memory_space=pltpu.SMEM),
                  pl.BlockSpec((None, TQ, qi_w), lambda b, i: (b, i, 0)),
                  pl.BlockSpec((None, S, kvw_w), lambda b, i: (b, 0, 0)),
                  pl.BlockSpec((None, TQ, q_w), lambda b, i: (b, i, 0)),
                  pl.BlockSpec((1, IDX_DIM), lambda b, i: (0, 0)),
                  pl.BlockSpec((1, IDX_DIM), lambda b, i: (0, 0))],
        out_specs=pl.BlockSpec((None, TQ, q_w), lambda b, i: (b, i, 0)),
        out_shape=jax.ShapeDtypeStruct((B, S, q_w), BF16),
        scratch_shapes=[pltpu.VMEM((N_BIAS_TILES, N_HEADS, TQ, CK), F32),
                        pltpu.VMEM((n_all, IDX_DIM, CK), BF16),
                        pltpu.VMEM((n_all, 2 * HEAD_DIM, CK), BF16),
                        pltpu.VMEM((S, 2 * HEAD_DIM), BF16),
                        pltpu.VMEM((n_all, TQ, CK), jnp.int32),
                        pltpu.VMEM((n_all // 2, 2, TQ, CK), jnp.int16),
                        pltpu.VMEM((n_all // 2, 2, TQ, CK), jnp.int16),
                        pltpu.VMEM((IDX_HEADS * TQ, IDX_DIM), BF16),
                        pltpu.VMEM((IDX_HEADS, TQ, LANES), F32),
                        pltpu.VMEM((N_HEADS * TQ, 2 * HEAD_DIM), BF16),
                        pltpu.VMEM((N_HEADS * TQ, CK), BF16),
                        pltpu.VMEM((N_HEADS, TQ, LANES), F32),
                        pltpu.VMEM((N_HEADS, TQ, 2 * HEAD_DIM), F32)],
        compiler_params=_params(2),
        name="dsa_attention",
    )(rel_bias, qi, kvw, q, knorm_g.reshape(1, IDX_DIM), knorm_b.reshape(1, IDX_DIM))


def _merge_kernel(yl_ref, ya_ref, wl_ref, wa_ref, gl_ref, ga_ref, o_ref):
    pl_ = jnp.dot(yl_ref[...], wl_ref[...], preferred_element_type=F32)
    pa_ = jnp.dot(ya_ref[...], wa_ref[...], preferred_element_type=F32)
    merged = (_sigmoid(gl_ref[...].astype(F32)) * pl_
              + _sigmoid(ga_ref[...].astype(F32)) * pa_)
    o_ref[...] = merged.astype(o_ref.dtype)


def _gated_merge(y_lru, y_att, w_l, w_a, gates, tm, tn):
    M, D = y_lru.shape
    N = w_l.shape[1]
    nj = N // tn
    return pl.pallas_call(
        _merge_kernel,
        grid=(M // tm, nj),
        in_specs=[pl.BlockSpec((tm, D), lambda i, j: (i, 0)),
                  pl.BlockSpec((tm, D), lambda i, j: (i, 0)),
                  pl.BlockSpec((D, tn), lambda i, j: (0, j)),
                  pl.BlockSpec((D, tn), lambda i, j: (0, j)),
                  pl.BlockSpec((tm, tn), lambda i, j: (i, j)),
                  pl.BlockSpec((tm, tn), lambda i, j: (i, nj + j))],
        out_specs=pl.BlockSpec((tm, tn), lambda i, j: (i, j)),
        out_shape=jax.ShapeDtypeStruct((M, N), BF16),
        compiler_params=_params(2),
        name="gated_merge",
    )(y_lru, y_att, w_l, w_a, gates, gates)


def _layer_norm_rows(v, g, b):
    mu = jnp.mean(v, axis=-1, keepdims=True)
    var = jnp.mean(jnp.square(v - mu), axis=-1, keepdims=True)
    return (v - mu) * lax.rsqrt(var + LN_EPS) * g + b


def _outproj_ln_kernel(m_ref, x_ref, w_ref, g_ref, b_ref, o32_ref, o16_ref):
    proj = jnp.dot(m_ref[...], w_ref[...], preferred_element_type=F32)
    h = _layer_norm_rows(ALPHA * x_ref[...] + proj, g_ref[...], b_ref[...])
    o32_ref[...] = h
    o16_ref[...] = h.astype(BF16)


def _outproj_ln(merged, x2d, w_out, g, b, tm):
    M, D = x2d.shape
    return pl.pallas_call(
        _outproj_ln_kernel,
        grid=(M // tm,),
        in_specs=[pl.BlockSpec((tm, D), lambda i: (i, 0)),
                  pl.BlockSpec((tm, D), lambda i: (i, 0)),
                  pl.BlockSpec((D, D), lambda i: (0, 0)),
                  pl.BlockSpec((1, D), lambda i: (0, 0)),
                  pl.BlockSpec((1, D), lambda i: (0, 0))],
        out_specs=[pl.BlockSpec((tm, D), lambda i: (i, 0)),
                   pl.BlockSpec((tm, D), lambda i: (i, 0))],
        out_shape=[jax.ShapeDtypeStruct((M, D), F32),
                   jax.ShapeDtypeStruct((M, D), BF16)],
        compiler_params=_params(1),
        name="outproj_ln1",
    )(merged, x2d, w_out, g.reshape(1, D), b.reshape(1, D))


def _ffn_up_kernel(halo_ref, h_ref, wg_ref, wv_ref, cwg_ref, cwv_ref, cbg_ref, cbv_ref, o_ref,
                   wg16, wv16, *, tm, conv_w, tiles_per_seq):
    i = pl.program_id(1)

    @pl.when(i == 0)
    def _():
        wg16[...] = wg_ref[...].astype(BF16)
        wv16[...] = wv_ref[...].astype(BF16)

    keep = jnp.where(i % tiles_per_seq > 0, 1.0, 0.0).astype(BF16)
    lhs = jnp.concatenate([halo_ref[...] * keep, h_ref[...]], axis=0)

    def conv_half(w_ref, cw_ref, cb_ref):
        up = jnp.dot(lhs, w_ref[...], preferred_element_type=F32)
        y = cb_ref[...] + cw_ref[conv_w - 1:conv_w, :] * up[SUBLANES:SUBLANES + tm]
        for j in range(conv_w - 1):
            back = pltpu.roll(up, conv_w - 1 - j, axis=0)
            y = y + cw_ref[j:j + 1, :] * back[SUBLANES:SUBLANES + tm]
        return y

    yg = conv_half(wg16, cwg_ref, cbg_ref)
    yv = conv_half(wv16, cwv_ref, cbv_ref)
    o_ref[...] = (_gelu_tanh(yg) * yv).astype(o_ref.dtype)


def _ffn_up(h16, w_up, conv_w, conv_b, S, tm, tn):
    M, D = h16.shape
    d_ff = w_up.shape[1] // 2
    width = conv_w.shape[0]
    assert width - 1 <= SUBLANES and S % tm == 0
    nj = d_ff // tn
    kern = functools.partial(_ffn_up_kernel, tm=tm, conv_w=width, tiles_per_seq=S // tm)
    halo_blocks = tm // SUBLANES
    cb2 = conv_b.reshape(1, 2 * d_ff)
    return pl.pallas_call(
        kern,
        grid=(nj, M // tm),
        in_specs=[pl.BlockSpec((SUBLANES, D), lambda j, i: (jnp.maximum(i * halo_blocks - 1, 0), 0)),
                  pl.BlockSpec((tm, D), lambda j, i: (i, 0)),
                  pl.BlockSpec((D, tn), lambda j, i: (0, j)),
                  pl.BlockSpec((D, tn), lambda j, i: (0, nj + j)),
                  pl.BlockSpec((width, tn), lambda j, i: (0, j)),
                  pl.BlockSpec((width, tn), lambda j, i: (0, nj + j)),
                  pl.BlockSpec((1, tn), lambda j, i: (0, j)),
                  pl.BlockSpec((1, tn), lambda j, i: (0, nj + j))],
        out_specs=pl.BlockSpec((tm, tn), lambda j, i: (i, j)),
        out_shape=jax.ShapeDtypeStruct((M, d_ff), BF16),
        scratch_shapes=[pltpu.VMEM((D, tn), BF16), pltpu.VMEM((D, tn), BF16)],
        compiler_params=_params(2),
        name="ffn_up_conv_geglu",
    )(h16, h16, w_up, w_up, conv_w, conv_w, cb2, cb2)


def _ffn_down_kernel(a_ref, h_ref, w_ref, g_ref, b_ref, o_ref):
    k = pl.program_id(1)

    @pl.when(k == 0)
    def _():
        o_ref[...] = jnp.dot(a_ref[...], w_ref[...], preferred_element_type=F32)

    @pl.when(k > 0)
    def _():
        o_ref[...] += jnp.dot(a_ref[...], w_ref[...], preferred_element_type=F32)

    @pl.when(k == pl.num_programs(1) - 1)
    def _():
        o_ref[...] = _layer_norm_rows(ALPHA * h_ref[...] + o_ref[...], g_ref[...], b_ref[...])


def _ffn_down_ln(act, h32, w_down, g, b, tm, tk):
    d_ff, D = w_down.shape
    M = act.shape[0]
    return pl.pallas_call(
        _ffn_down_kernel,
        grid=(M // tm, d_ff // tk),
        in_specs=[pl.BlockSpec((tm, tk), lambda i, k: (i, k)),
                  pl.BlockSpec((tm, D), lambda i, k: (i, 0)),
                  pl.BlockSpec((tk, D), lambda i, k: (k, 0)),
                  pl.BlockSpec((1, D), lambda i, k: (0, 0)),
                  pl.BlockSpec((1, D), lambda i, k: (0, 0))],
        out_specs=pl.BlockSpec((tm, D), lambda i, k: (i, 0)),
        out_shape=jax.ShapeDtypeStruct((M, D), F32),
        compiler_params=_params(2),
        name="ffn_down_ln2",
    )(act, h32, w_down, g.reshape(1, D), b.reshape(1, D))


def kernel(x, w_in, lru_conv_w, lru_conv_b, lru_gate_a_w, lru_gate_a_b, lru_gate_x_w, lru_gate_x_b,
           lru_lambda, idx_knorm_g, idx_knorm_b, rel_bias, w_proj_lru, w_proj_attn, w_out,
           ln1_g, ln1_b, ffn_w_up, ffn_conv_w, ffn_conv_b, ffn_w_down, ln2_g, ln2_b):
    B, S, D = x.shape
    assert B == SUBLANES and S % CK == 0 and w_in.shape[0] == DEPTH
    top_k = min(TOPK_MAX, S // 4)
    d_rnn = lru_conv_w.shape[-1]
    q_w = N_HEADS * HEAD_DIM
    qi_w = IDX_HEADS * IDX_DIM
    splits = (d_rnn, d_rnn, q_w, HEAD_DIM, HEAD_DIM, qi_w, IDX_DIM, IDX_HEADS, D, D)
    offs = [0]
    for s_ in splits:
        offs.append(offs[-1] + s_)
    M = B * S

    l = 0
    w = w_in[l].astype(BF16)
    x2d = x.reshape(M, D)
    w_lru = w[:, offs[0]:offs[2]]
    w_q = w[:, offs[2]:offs[3]]
    pad = LANES - IDX_DIM - IDX_HEADS
    w_kvw = jnp.concatenate([w[:, offs[3]:offs[5]], w[:, offs[6]:offs[8]],
                             jnp.zeros((D, pad), BF16)], axis=1)
    w_qi = w[:, offs[5]:offs[6]]
    w_g = w[:, offs[8]:offs[10]]

    tm_in = min(M, 2048)
    lxg, x16 = _project_and_cast(x2d, w_lru, F32, min(M, 1024), 1024)
    lxg = lxg.reshape(B, S, 2 * d_rnn)
    q = _project(x16, w_q, BF16, tm_in, 1024,
                 out_scale=(HEAD_DIM ** -0.5) * LOG2E).reshape(B, S, q_w)
    kvw = _project(x16, w_kvw, F32, tm_in, w_kvw.shape[1]).reshape(B, S, w_kvw.shape[1])
    qi = _project(x16, w_qi, F32, tm_in, 1024).reshape(B, S, qi_w)
    gates = _project(x16, w_g, BF16, tm_in, 1024)

    y_lru = _rg_lru(lxg, lru_conv_w[l], lru_conv_b[l], lru_gate_a_w[l], lru_gate_a_b[l],
                    lru_gate_x_w[l], lru_gate_x_b[l], lru_lambda[l], S, B, tc=min(S, 256), cbw=256)

    y_att = _sparse_attention(qi, kvw, q, rel_bias, idx_knorm_g[l], idx_knorm_b[l], top_k)

    merged = _gated_merge(y_lru.reshape(M, d_rnn), y_att.reshape(M, q_w),
                          w_proj_lru[l].astype(BF16), w_proj_attn[l].astype(BF16),
                          gates, tm=min(M, 1024), tn=512)
    h32, h16 = _outproj_ln(merged, x2d, w_out[l].astype(BF16), ln1_g[l], ln1_b[l], tm=512)
    act = _ffn_up(h16, ffn_w_up[l], ffn_conv_w[l], ffn_conv_b[l], S,
                  tm=min(S, 1024), tn=512)
    out = _ffn_down_ln(act, h32, ffn_w_down[l].astype(BF16), ln2_g[l], ln2_b[l],
                       tm=min(M, 1024), tk=1024)
    return out.reshape(B, S, D)
```

```python
import functools
import math

import jax
import jax.numpy as jnp
from jax import lax
from jax.experimental import pallas as pl
from jax.experimental.pallas import tpu as pltpu

N_HEADS = 16
HEAD_DIM = 128
IDX_HEADS = 16
IDX_DIM = 64
TOPK_MAX = 256
LRU_BLOCK = 128
LRU_C = 8.0
REL_BUCKETS = 32
REL_MAX_DIST = 128
LN_EPS = 1e-5
DEPTH = 1
ALPHA = (2.0 * DEPTH) ** 0.25

LANES = 128
SUBLANES = 8
VMEM_LIMIT_BYTES = 56 * 1024 * 1024

TQ = 256
CK = 256
HEAD_GROUP = 8
BIAS_TILE_STEP = 256
N_BIAS_TILES = 2
assert TQ % BIAS_TILE_STEP == 0 and CK % BIAS_TILE_STEP == 0
assert N_BIAS_TILES * BIAS_TILE_STEP - (CK - 1) >= REL_MAX_DIST

INT_MIN = -(2 ** 31)
MASK_LOGIT = -2e30
LOG2E = math.log2(math.e)
F32 = jnp.float32
BF16 = jnp.bfloat16


def _params(n_axes):
    return pltpu.CompilerParams(dimension_semantics=("arbitrary",) * n_axes,
                                vmem_limit_bytes=VMEM_LIMIT_BYTES)


def _matmul_kernel(x_ref, w_ref, o_ref, *, out_scale):
    acc = jnp.dot(x_ref[...], w_ref[...], preferred_element_type=F32)
    if out_scale != 1.0:
        acc = acc * out_scale
    o_ref[...] = acc.astype(o_ref.dtype)


def _project(x2d, w, out_dtype, tm, tn, out_scale=1.0):
    M, K = x2d.shape
    N = w.shape[1]
    return pl.pallas_call(
        functools.partial(_matmul_kernel, out_scale=out_scale),
        grid=(M // tm, N // tn),
        in_specs=[pl.BlockSpec((tm, K), lambda i, j: (i, 0)),
                  pl.BlockSpec((K, tn), lambda i, j: (0, j))],
        out_specs=pl.BlockSpec((tm, tn), lambda i, j: (i, j)),
        out_shape=jax.ShapeDtypeStruct((M, N), out_dtype),
        compiler_params=_params(2),
        name="in_proj",
    )(x2d, w)


def _matmul_cast_kernel(x_ref, w_ref, o_ref, x16_ref):
    @pl.when(pl.program_id(1) == 0)
    def _():
        x16_ref[...] = x_ref[...].astype(BF16)

    o_ref[...] = jnp.dot(x16_ref[...], w_ref[...], preferred_element_type=F32).astype(o_ref.dtype)


def _project_and_cast(x2d, w, out_dtype, tm, tn):
    M, K = x2d.shape
    N = w.shape[1]
    return pl.pallas_call(
        _matmul_cast_kernel,
        grid=(M // tm, N // tn),
        in_specs=[pl.BlockSpec((tm, K), lambda i, j: (i, 0)),
                  pl.BlockSpec((K, tn), lambda i, j: (0, j))],
        out_specs=[pl.BlockSpec((tm, tn), lambda i, j: (i, j)),
                   pl.BlockSpec((tm, K), lambda i, j: (i, 0))],
        out_shape=[jax.ShapeDtypeStruct((M, N), out_dtype),
                   jax.ShapeDtypeStruct((M, K), BF16)],
        compiler_params=_params(2),
        name="in_proj_cast",
    )(x2d, w)


def _gelu_tanh(x):
    c = math.sqrt(2.0 / math.pi)
    half = 0.5 * x
    return half + half * jnp.tanh(x * (c + (c * 0.044715) * (x * x)))


def _sigmoid(x):
    return 0.5 * jnp.tanh(0.5 * x) + 0.5


def _lru_kernel(lx_ref, lg_ref, cw_ref, cb_ref, wa_ref, ba_ref, wx_ref, bx_ref, lam_ref, o_ref,
                xs, a_s, u_s, h_s, *, tc, cb_width, conv_w):
    ti = pl.program_id(1)
    halo_rows = (conv_w - 1) * SUBLANES
    n_slab = cb_width // LANES
    rows = tc * SUBLANES
    slab = lambda v, s: v[:, s * LANES:(s + 1) * LANES]

    @pl.when(ti == 0)
    def _():
        xs[:, 0:halo_rows, :] = jnp.zeros((n_slab, halo_rows, LANES), F32)
        h_s[...] = jnp.zeros(h_s.shape, F32)

    for b in range(SUBLANES):
        for s in range(n_slab):
            xs[s, pl.ds(halo_rows + b, tc, stride=SUBLANES), :] = (
                lx_ref[b, :, s * LANES:(s + 1) * LANES].astype(F32))
    y_parts = []
    for s in range(n_slab):
        cs = slice(s * LANES, (s + 1) * LANES)
        y = cb_ref[:, cs] + cw_ref[0:1, cs] * xs[s, 0:rows, :]
        for j in range(1, conv_w):
            y = y + cw_ref[j:j + 1, cs] * xs[s, j * SUBLANES:j * SUBLANES + rows, :]
        y_parts.append(y)
        xs[s, 0:halo_rows, :] = xs[s, rows:rows + halo_rows, :]
    y2 = jnp.concatenate(y_parts, axis=1)
    yb = y2.astype(BF16)
    r_parts, i_parts = [], []
    for n in range(cb_width // LRU_BLOCK):
        blk = yb[:, n * LRU_BLOCK:(n + 1) * LRU_BLOCK]
        r_parts.append(jnp.dot(blk, wa_ref[n], preferred_element_type=F32))
        i_parts.append(jnp.dot(blk, wx_ref[n], preferred_element_type=F32))
    r = _sigmoid(jnp.concatenate(r_parts, axis=1) + ba_ref[...])
    g_in = _sigmoid(jnp.concatenate(i_parts, axis=1) + bx_ref[...])

    z = -lam_ref[...]
    softplus = jnp.maximum(z, 0.0) + jnp.log1p(jnp.exp(-jnp.abs(z)))
    log_a = (-LRU_C) * r * softplus
    a = jnp.exp(log_a)
    mult = jnp.sqrt(-jnp.tanh(log_a) * (a * a + 1.0))
    gx = g_in * y2
    u = mult * gx
    for s in range(n_slab):
        a_s[s] = slab(a, s)
        u_s[s] = slab(u, s)

    @pl.when(ti == 0)
    def _():
        for s in range(n_slab):
            u_s[s, 0:SUBLANES, :] = slab(gx, s)[0:SUBLANES]

    def step(t, h):
        r0 = pl.multiple_of(t * SUBLANES, SUBLANES)
        new = []
        for s in range(n_slab):
            hs = a_s[s, pl.ds(r0, SUBLANES), :] * h[s] + u_s[s, pl.ds(r0, SUBLANES), :]
            u_s[s, pl.ds(r0, SUBLANES), :] = hs
            new.append(hs)
        return tuple(new)

    h_fin = lax.fori_loop(0, tc, step, tuple(h_s[s] for s in range(n_slab)), unroll=8)
    for s in range(n_slab):
        h_s[s] = h_fin[s]
    for b in range(SUBLANES):
        for s in range(n_slab):
            cs = slice(s * LANES, (s + 1) * LANES)
            h_b = u_s[s, pl.ds(b, tc, stride=SUBLANES), :]
            o_ref[b, :, cs] = (_gelu_tanh(lg_ref[b, :, cs].astype(F32)) * h_b).astype(o_ref.dtype)


def _rg_lru(lxg, conv_w, conv_b, gate_a_w, gate_a_b, gate_x_w, gate_x_b, lam, S, B, tc, cbw):
    d_rnn = conv_w.shape[1]
    width = conv_w.shape[0]
    ncb = d_rnn // cbw
    nblk = cbw // LRU_BLOCK
    n_slab = cbw // LANES
    row = lambda v: v.reshape(1, d_rnn)
    kern = functools.partial(_lru_kernel, tc=tc, cb_width=cbw, conv_w=width)
    vec_spec = pl.BlockSpec((1, cbw), lambda n, t: (0, n))
    return pl.pallas_call(
        kern,
        grid=(ncb, S // tc),
        in_specs=[pl.BlockSpec((B, tc, cbw), lambda n, t: (0, t, n)),
                  pl.BlockSpec((B, tc, cbw), lambda n, t: (0, t, ncb + n)),
                  pl.BlockSpec((width, cbw), lambda n, t: (0, n)),
                  vec_spec,
                  pl.BlockSpec((nblk, LRU_BLOCK, LRU_BLOCK), lambda n, t: (n, 0, 0)),
                  vec_spec,
                  pl.BlockSpec((nblk, LRU_BLOCK, LRU_BLOCK), lambda n, t: (n, 0, 0)),
                  vec_spec,
                  vec_spec],
        out_specs=pl.BlockSpec((B, tc, cbw), lambda n, t: (0, t, n)),
        out_shape=jax.ShapeDtypeStruct((B, S, d_rnn), BF16),
        scratch_shapes=[pltpu.VMEM((n_slab, (tc + width - 1) * B, LANES), F32),
                        pltpu.VMEM((n_slab, tc * B, LANES), F32),
                        pltpu.VMEM((n_slab, tc * B, LANES), F32),
                        pltpu.VMEM((n_slab, B, LANES), F32)],
        compiler_params=_params(2),
        name="rg_lru",
    )(lxg, lxg, conv_w, row(conv_b), gate_a_w.astype(BF16), row(gate_a_b),
      gate_x_w.astype(BF16), row(gate_x_b), row(lam))


I16_MIN = -(2 ** 15)
I16_MAX = 2 ** 15 - 1


def _far_bucket():
    max_exact = REL_BUCKETS // 2
    large = max_exact + int(math.log(REL_MAX_DIST / max_exact) / math.log(REL_MAX_DIST / max_exact)
                            * (REL_BUCKETS - max_exact))
    return min(large, REL_BUCKETS - 1)


def _fill_bias_tiles(rb_ref, bias_s):
    rows = lax.broadcasted_iota(jnp.int32, (TQ, CK), 0)
    cols = lax.broadcasted_iota(jnp.int32, (TQ, CK), 1)
    max_exact = REL_BUCKETS // 2

    def tile_body(n, carry):
        d = n // N_HEADS
        h = n % N_HEADS
        rel = d * BIAS_TILE_STEP + rows - cols
        nf = jnp.maximum(rel, 1).astype(F32)
        large = max_exact + (jnp.log(nf / max_exact) / math.log(REL_MAX_DIST / max_exact)
                             * (REL_BUCKETS - max_exact)).astype(jnp.int32)
        large = jnp.minimum(large, REL_BUCKETS - 1)
        bucket = jnp.where(rel < max_exact, rel, large)
        acc = jnp.zeros((TQ, CK), F32)
        for b in range(REL_BUCKETS):
            acc = jnp.where(bucket == b, rb_ref[b, h], acc)
        bias_s[d, h] = (acc - rb_ref[_far_bucket(), h]) * LOG2E
        return carry

    lax.fori_loop(0, N_BIAS_TILES * N_HEADS, tile_body, 0)


def _ordered_key(score):
    bits = pltpu.bitcast(score, jnp.int32)
    return jnp.where(bits >= 0, bits, bits ^ jnp.int32(0x7FFFFFFF))


def _count_ge(arr, n_pairs, cand):
    cand16 = jnp.broadcast_to(cand, (TQ, LANES)).astype(jnp.int16)
    cand16 = jnp.concatenate([cand16] * (CK // LANES), axis=1)
    one, zero = jnp.int16(1), jnp.int16(0)

    def count_pair(p, cnt):
        cnt = cnt + jnp.where(arr[p, 0] >= cand16, one, zero)
        return cnt + jnp.where(arr[p, 1] >= cand16, one, zero)

    cnt = lax.fori_loop(0, n_pairs, count_pair, jnp.zeros((TQ, CK), jnp.int16))
    return jnp.sum(cnt.astype(jnp.int32).astype(F32), axis=1, keepdims=True)


def _kth_largest_i16(arr, n_pairs, need):
    def bit_step(step, thr):
        cand = thr + lax.shift_left(jnp.int32(1), 15 - step)
        return jnp.where(_count_ge(arr, n_pairs, cand) >= need, cand, thr)

    return lax.fori_loop(0, 16, bit_step, jnp.full((TQ, 1), I16_MIN, jnp.int32))


def _for_chunks(lo, hi, chunks_fn):
    n = hi - lo

    def pair(p, carry):
        chunks_fn(lo + 2 * p, 2)
        return carry

    lax.fori_loop(0, n // 2, pair, 0)

    @pl.when(n % 2 == 1)
    def _():
        chunks_fn(hi - 1, 1)


def _attn_kernel(rb_ref, qi_ref, kvw_ref, q_ref, kg_ref, kb_ref, o_ref,
                 bias_s, kin_t, k_t, v_aug, keys, hi_s, lo_s, qis, wbuf, qs, pbuf, m_s, accl_s,
                 *, top_k, seq):
    i = pl.program_id(1)
    k_off, v_off, kw_off = 0, HEAD_DIM, 2 * HEAD_DIM
    n_all = seq // CK
    lane = lax.broadcasted_iota(jnp.int32, (TQ, LANES), 1)

    @pl.when((pl.program_id(0) == 0) & (i == 0))
    def _():
        _fill_bias_tiles(rb_ref, bias_s)

    @pl.when(i == 0)
    def _():
        kraw = kvw_ref[:, kw_off:kw_off + IDX_DIM]
        mu = jnp.mean(kraw, axis=-1, keepdims=True)
        var = jnp.mean(jnp.square(kraw - mu), axis=-1, keepdims=True)
        kn = (kraw - mu) * lax.rsqrt(var + LN_EPS) * kg_ref[...] + kb_ref[...]
        kn = jnp.concatenate([kn, jnp.zeros((seq, LANES - IDX_DIM), F32)], axis=1)
        sub = lax.broadcasted_iota(jnp.int32, (HEAD_DIM, CK), 0)
        ones_rows = jnp.where(sub < 2, 1.0, 0.0).astype(BF16)
        for c in range(n_all):
            kin_t[c] = kn[c * CK:(c + 1) * CK, :].T[0:IDX_DIM].astype(BF16)
            k_t[c, 0:HEAD_DIM, :] = kvw_ref[c * CK:(c + 1) * CK, k_off:k_off + HEAD_DIM].T.astype(BF16)
            k_t[c, HEAD_DIM:2 * HEAD_DIM, :] = ones_rows
        v_aug[:, 0:HEAD_DIM] = kvw_ref[:, v_off:v_off + HEAD_DIM].astype(BF16)
        v_aug[:, HEAD_DIM:2 * HEAD_DIM] = jnp.ones((seq, HEAD_DIM), BF16)
        for h in range(N_HEADS):
            fb = jnp.full((TQ, LANES), rb_ref[_far_bucket(), h] * LOG2E, F32)
            fb_hi = fb.astype(BF16).astype(F32)
            qs[h * TQ:(h + 1) * TQ, HEAD_DIM:2 * HEAD_DIM] = jnp.where(
                lane == 0, fb_hi, jnp.where(lane == 1, fb - fb_hi, 0.0)).astype(BF16)

    q0 = pl.multiple_of(i * TQ, TQ)
    n_chunks = (q0 + TQ + CK - 1) // CK
    n_pairs = (n_chunks + 1) // 2
    rows = q0 + lax.broadcasted_iota(jnp.int32, (TQ, CK), 0)
    cols = lax.broadcasted_iota(jnp.int32, (TQ, CK), 1)

    qib = qi_ref[...]
    wi = kvw_ref[pl.ds(q0, TQ), kw_off + IDX_DIM:kw_off + IDX_DIM + IDX_HEADS]
    wi = wi * ((IDX_DIM ** -0.5) * (IDX_HEADS ** -0.5))
    for h in range(IDX_HEADS):
        qis[h * TQ:(h + 1) * TQ, :] = qib[:, h * IDX_DIM:(h + 1) * IDX_DIM]
        wbuf[h] = jnp.broadcast_to(wi[:, h:h + 1], (TQ, LANES))

    def score_chunk(c):
        kc = kin_t[c]
        acc = jnp.zeros((TQ, CK), F32)
        for g in range(IDX_HEADS // HEAD_GROUP):
            g0 = g * HEAD_GROUP * TQ
            d_g = jnp.dot(qis[g0:g0 + HEAD_GROUP * TQ, :], kc, preferred_element_type=F32)
            for hh in range(HEAD_GROUP):
                w_h = wbuf[g * HEAD_GROUP + hh]
                acc = acc + jnp.concatenate([w_h] * (CK // LANES), axis=1) * jnp.maximum(
                    d_g[hh * TQ:(hh + 1) * TQ], 0.0)
        key = jnp.where(c * CK + cols <= rows, _ordered_key(acc), jnp.int32(INT_MIN))
        keys[c] = key
        hi_s[c // 2, c % 2] = lax.shift_right_arithmetic(key, 16).astype(jnp.int16)
        lo_s[c // 2, c % 2] = ((key & 0xFFFF) + I16_MIN).astype(jnp.int16)

    def score_chunks(c, width):
        for k in range(width):
            score_chunk(c + k)

    _for_chunks(0, n_chunks, score_chunks)

    @pl.when(n_chunks % 2 == 1)
    def _():
        hi_s[n_chunks // 2, 1] = jnp.full((TQ, CK), I16_MIN, jnp.int16)
        lo_s[n_chunks // 2, 1] = jnp.full((TQ, CK), I16_MIN, jnp.int16)

    need = jnp.full((TQ, 1), float(top_k), F32)
    hi_k = _kth_largest_i16(hi_s, n_pairs, need)
    above = jnp.where(hi_k == I16_MAX, 0.0,
                      _count_ge(hi_s, n_pairs, jnp.minimum(hi_k + 1, I16_MAX)))
    hi_k16 = jnp.broadcast_to(hi_k, (TQ, LANES)).astype(jnp.int16)
    hi_k16 = jnp.concatenate([hi_k16] * (CK // LANES), axis=1)

    def keep_group(p, carry):
        for j in range(2):
            lo_s[p, j] = jnp.where(hi_s[p, j] == hi_k16, lo_s[p, j], jnp.int16(I16_MIN))
        return carry

    lax.fori_loop(0, n_pairs, keep_group, 0)
    lo_k = _kth_largest_i16(lo_s, n_pairs, need - above)
    thr = hi_k * 65536 + (lo_k - I16_MIN)
    thr = jnp.maximum(thr, jnp.int32(INT_MIN + 1))

    for h in range(N_HEADS):
        qs[h * TQ:(h + 1) * TQ, 0:HEAD_DIM] = q_ref[:, h * HEAD_DIM:(h + 1) * HEAD_DIM]
    m_s[...] = jnp.full(m_s.shape, 0.5 * MASK_LOGIT, F32)
    accl_s[...] = jnp.zeros(accl_s.shape, F32)

    def attn_chunk(c, near):
        kc = k_t[c]
        vc = v_aug[pl.ds(pl.multiple_of(c * CK, CK), CK), :]
        sel = keys[c] >= thr
        tile = (q0 - c * CK) // BIAS_TILE_STEP
        for g in range(N_HEADS // HEAD_GROUP):
            g0 = g * HEAD_GROUP * TQ
            s_g = jnp.dot(qs[g0:g0 + HEAD_GROUP * TQ, :], kc, preferred_element_type=F32)
            alphas = []
            for hh in range(HEAD_GROUP):
                h = g * HEAD_GROUP + hh
                s = s_g[hh * TQ:(hh + 1) * TQ]
                if near:
                    s = s + bias_s[tile, h]
                s = jnp.where(sel, s, MASK_LOGIT)
                m_old = m_s[h]
                m_new = jnp.maximum(m_old, jnp.max(s, axis=1, keepdims=True))
                p = jnp.exp2(s - jnp.concatenate([m_new] * (CK // LANES), axis=1))
                alphas.append(jnp.exp2(m_old - m_new))
                m_s[h] = m_new
                pbuf[h * TQ:(h + 1) * TQ, :] = p.astype(BF16)
            pv = jnp.dot(pbuf[g0:g0 + HEAD_GROUP * TQ, :], vc, preferred_element_type=F32)
            for hh in range(HEAD_GROUP):
                h = g * HEAD_GROUP + hh
                accl_s[h] = (jnp.concatenate([alphas[hh], alphas[hh]], axis=1) * accl_s[h]
                             + pv[hh * TQ:(hh + 1) * TQ])

    def attn_chunks(c, width, *, near):
        for k in range(width):
            attn_chunk(c + k, near)

    n_far = jnp.maximum((q0 - N_BIAS_TILES * BIAS_TILE_STEP) // CK + 1, 0)
    _for_chunks(0, n_far, functools.partial(attn_chunks, near=False))
    _for_chunks(n_far, n_chunks, functools.partial(attn_chunks, near=True))
    for h in range(N_HEADS):
        al = accl_s[h]
        o_ref[:, h * HEAD_DIM:(h + 1) * HEAD_DIM] = (
            al[:, 0:HEAD_DIM] / al[:, HEAD_DIM:2 * HEAD_DIM]).astype(o_ref.dtype)


def _sparse_attention(qi, kvw, q, rel_bias, knorm_g, knorm_b, top_k):
    B, S, q_w = q.shape
    qi_w = qi.shape[2]
    kvw_w = kvw.shape[2]
    n_all = S // CK
    assert n_all % 2 == 0
    kern = functools.partial(_attn_kernel, top_k=top_k, seq=S)
    return pl.pallas_call(
        kern,
        grid=(B, S // TQ),
        in_specs=[pl.BlockSpec(memory_space=pltpu.SMEM),
                  pl.BlockSpec((None, TQ, qi_w), lambda b, i: (b, i, 0)),
                  pl.BlockSpec((None, S, kvw_w), lambda b, i: (b, 0, 0)),
                  pl.BlockSpec((None, TQ, q_w), lambda b, i: (b, i, 0)),
                  pl.BlockSpec((1, IDX_DIM), lambda b, i: (0, 0)),
                  pl.BlockSpec((1, IDX_DIM), lambda b, i: (0, 0))],
        out_specs=pl.BlockSpec((None, TQ, q_w), lambda b, i: (b, i, 0)),
        out_shape=jax.ShapeDtypeStruct((B, S, q_w), BF16),
        scratch_shapes=[pltpu.VMEM((N_BIAS_TILES, N_HEADS, TQ, CK), F32),
                        pltpu.VMEM((n_all, IDX_DIM, CK), BF16),
                        pltpu.VMEM((n_all, 2 * HEAD_DIM, CK), BF16),
                        pltpu.VMEM((S, 2 * HEAD_DIM), BF16),
                        pltpu.VMEM((n_all, TQ, CK), jnp.int32),
                        pltpu.VMEM((n_all // 2, 2, TQ, CK), jnp.int16),
                        pltpu.VMEM((n_all // 2, 2, TQ, CK), jnp.int16),
                        pltpu.VMEM((IDX_HEADS * TQ, IDX_DIM), BF16),
                        pltpu.VMEM((IDX_HEADS, TQ, LANES), F32),
                        pltpu.VMEM((N_HEADS * TQ, 2 * HEAD_DIM), BF16),
                        pltpu.VMEM((N_HEADS * TQ, CK), BF16),
                        pltpu.VMEM((N_HEADS, TQ, LANES), F32),
                        pltpu.VMEM((N_HEADS, TQ, 2 * HEAD_DIM), F32)],
        compiler_params=_params(2),
        name="dsa_attention",
    )(rel_bias, qi, kvw, q, knorm_g.reshape(1, IDX_DIM), knorm_b.reshape(1, IDX_DIM))


def _merge_kernel(yl_ref, ya_ref, wl_ref, wa_ref, gl_ref, ga_ref, o_ref):
    pl_ = jnp.dot(yl_ref[...], wl_ref[...], preferred_element_type=F32)
    pa_ = jnp.dot(ya_ref[...], wa_ref[...], preferred_element_type=F32)
    merged = (_sigmoid(gl_ref[...].astype(F32)) * pl_
              + _sigmoid(ga_ref[...].astype(F32)) * pa_)
    o_ref[...] = merged.astype(o_ref.dtype)


def _gated_merge(y_lru, y_att, w_l, w_a, gates, tm, tn):
    M, D = y_lru.shape
    N = w_l.shape[1]
    nj = N // tn
    return pl.pallas_call(
        _merge_kernel,
        grid=(M // tm, nj),
        in_specs=[pl.BlockSpec((tm, D), lambda i, j: (i, 0)),
                  pl.BlockSpec((tm, D), lambda i, j: (i, 0)),
                  pl.BlockSpec((D, tn), lambda i, j: (0, j)),
                  pl.BlockSpec((D, tn), lambda i, j: (0, j)),
                  pl.BlockSpec((tm, tn), lambda i, j: (i, j)),
                  pl.BlockSpec((tm, tn), lambda i, j: (i, nj + j))],
        out_specs=pl.BlockSpec((tm, tn), lambda i, j: (i, j)),
        out_shape=jax.ShapeDtypeStruct((M, N), BF16),
        compiler_params=_params(2),
        name="gated_merge",
    )(y_lru, y_att, w_l, w_a, gates, gates)


def _layer_norm_rows(v, g, b):
    mu = jnp.mean(v, axis=-1, keepdims=True)
    var = jnp.mean(jnp.square(v - mu), axis=-1, keepdims=True)
    return (v - mu) * lax.rsqrt(var + LN_EPS) * g + b


def _outproj_ln_kernel(m_ref, x_ref, w_ref, g_ref, b_ref, o32_ref, o16_ref):
    proj = jnp.dot(m_ref[...], w_ref[...], preferred_element_type=F32)
    h = _layer_norm_rows(ALPHA * x_ref[...] + proj, g_ref[...], b_ref[...])
    o32_ref[...] = h
    o16_ref[...] = h.astype(BF16)


def _outproj_ln(merged, x2d, w_out, g, b, tm):
    M, D = x2d.shape
    return pl.pallas_call(
        _outproj_ln_kernel,
        grid=(M // tm,),
        in_specs=[pl.BlockSpec((tm, D), lambda i: (i, 0)),
                  pl.BlockSpec((tm, D), lambda i: (i, 0)),
                  pl.BlockSpec((D, D), lambda i: (0, 0)),
                  pl.BlockSpec((1, D), lambda i: (0, 0)),
                  pl.BlockSpec((1, D), lambda i: (0, 0))],
        out_specs=[pl.BlockSpec((tm, D), lambda i: (i, 0)),
                   pl.BlockSpec((tm, D), lambda i: (i, 0))],
        out_shape=[jax.ShapeDtypeStruct((M, D), F32),
                   jax.ShapeDtypeStruct((M, D), BF16)],
        compiler_params=_params(1),
        name="outproj_ln1",
    )(merged, x2d, w_out, g.reshape(1, D), b.reshape(1, D))


def _ffn_up_kernel(halo_ref, h_ref, wg_ref, wv_ref, cwg_ref, cwv_ref, cbg_ref, cbv_ref, o_ref,
                   wg16, wv16, *, tm, conv_w, tiles_per_seq):
    i = pl.program_id(1)

    @pl.when(i == 0)
    def _():
        wg16[...] = wg_ref[...].astype(BF16)
        wv16[...] = wv_ref[...].astype(BF16)

    keep = jnp.where(i % tiles_per_seq > 0, 1.0, 0.0).astype(BF16)
    lhs = jnp.concatenate([halo_ref[...] * keep, h_ref[...]], axis=0)

    def conv_half(w_ref, cw_ref, cb_ref):
        up = jnp.dot(lhs, w_ref[...], preferred_element_type=F32)
        y = cb_ref[...] + cw_ref[conv_w - 1:conv_w, :] * up[SUBLANES:SUBLANES + tm]
        for j in range(conv_w - 1):
            back = pltpu.roll(up, conv_w - 1 - j, axis=0)
            y = y + cw_ref[j:j + 1, :] * back[SUBLANES:SUBLANES + tm]
        return y

    yg = conv_half(wg16, cwg_ref, cbg_ref)
    yv = conv_half(wv16, cwv_ref, cbv_ref)
    o_ref[...] = (_gelu_tanh(yg) * yv).astype(o_ref.dtype)


def _ffn_up(h16, w_up, conv_w, conv_b, S, tm, tn):
    M, D = h16.shape
    d_ff = w_up.shape[1] // 2
    width = conv_w.shape[0]
    assert width - 1 <= SUBLANES and S % tm == 0
    nj = d_ff // tn
    kern = functools.partial(_ffn_up_kernel, tm=tm, conv_w=width, tiles_per_seq=S // tm)
    halo_blocks = tm // SUBLANES
    cb2 = conv_b.reshape(1, 2 * d_ff)
    return pl.pallas_call(
        kern,
        grid=(nj, M // tm),
        in_specs=[pl.BlockSpec((SUBLANES, D), lambda j, i: (jnp.maximum(i * halo_blocks - 1, 0), 0)),
                  pl.BlockSpec((tm, D), lambda j, i: (i, 0)),
                  pl.BlockSpec((D, tn), lambda j, i: (0, j)),
                  pl.BlockSpec((D, tn), lambda j, i: (0, nj + j)),
                  pl.BlockSpec((width, tn), lambda j, i: (0, j)),
                  pl.BlockSpec((width, tn), lambda j, i: (0, nj + j)),
                  pl.BlockSpec((1, tn), lambda j, i: (0, j)),
                  pl.BlockSpec((1, tn), lambda j, i: (0, nj + j))],
        out_specs=pl.BlockSpec((tm, tn), lambda j, i: (i, j)),
        out_shape=jax.ShapeDtypeStruct((M, d_ff), BF16),
        scratch_shapes=[pltpu.VMEM((D, tn), BF16), pltpu.VMEM((D, tn), BF16)],
        compiler_params=_params(2),
        name="ffn_up_conv_geglu",
    )(h16, h16, w_up, w_up, conv_w, conv_w, cb2, cb2)


def _ffn_down_kernel(a_ref, h_ref, w_ref, g_ref, b_ref, o_ref):
    k = pl.program_id(1)

    @pl.when(k == 0)
    def _():
        o_ref[...] = jnp.dot(a_ref[...], w_ref[...], preferred_element_type=F32)

    @pl.when(k > 0)
    def _():
        o_ref[...] += jnp.dot(a_ref[...], w_ref[...], preferred_element_type=F32)

    @pl.when(k == pl.num_programs(1) - 1)
    def _():
        o_ref[...] = _layer_norm_rows(ALPHA * h_ref[...] + o_ref[...], g_ref[...], b_ref[...])


def _ffn_down_ln(act, h32, w_down, g, b, tm, tk):
    d_ff, D = w_down.shape
    M = act.shape[0]
    return pl.pallas_call(
        _ffn_down_kernel,
        grid=(M // tm, d_ff // tk),
        in_specs=[pl.BlockSpec((tm, tk), lambda i, k: (i, k)),
                  pl.BlockSpec((tm, D), lambda i, k: (i, 0)),
                  pl.BlockSpec((tk, D), lambda i, k: (k, 0)),
                  pl.BlockSpec((1, D), lambda i, k: (0, 0)),
                  pl.BlockSpec((1, D), lambda i, k: (0, 0))],
        out_specs=pl.BlockSpec((tm, D), lambda i, k: (i, 0)),
        out_shape=jax.ShapeDtypeStruct((M, D), F32),
        compiler_params=_params(2),
        name="ffn_down_ln2",
    )(act, h32, w_down, g.reshape(1, D), b.reshape(1, D))


def kernel(x, w_in, lru_conv_w, lru_conv_b, lru_gate_a_w, lru_gate_a_b, lru_gate_x_w, lru_gate_x_b,
           lru_lambda, idx_knorm_g, idx_knorm_b, rel_bias, w_proj_lru, w_proj_attn, w_out,
           ln1_g, ln1_b, ffn_w_up, ffn_conv_w, ffn_conv_b, ffn_w_down, ln2_g, ln2_b):
    B, S, D = x.shape
    assert B == SUBLANES and S % CK == 0 and w_in.shape[0] == DEPTH
    top_k = min(TOPK_MAX, S // 4)
    d_rnn = lru_conv_w.shape[-1]
    q_w = N_HEADS * HEAD_DIM
    qi_w = IDX_HEADS * IDX_DIM
    splits = (d_rnn, d_rnn, q_w, HEAD_DIM, HEAD_DIM, qi_w, IDX_DIM, IDX_HEADS, D, D)
    offs = [0]
    for s_ in splits:
        offs.append(offs[-1] + s_)
    M = B * S

    l = 0
    w = w_in[l].astype(BF16)
    x2d = x.reshape(M, D)
    w_lru = w[:, offs[0]:offs[2]]
    w_q = w[:, offs[2]:offs[3]]
    pad = LANES - IDX_DIM - IDX_HEADS
    w_kvw = jnp.concatenate([w[:, offs[3]:offs[5]], w[:, offs[6]:offs[8]],
                             jnp.zeros((D, pad), BF16)], axis=1)
    w_qi = w[:, offs[5]:offs[6]]
    w_g = w[:, offs[8]:offs[10]]

    tm_in = min(M, 2048)
    lxg, x16 = _project_and_cast(x2d, w_lru, BF16, min(M, 1024), 1024)
    lxg = lxg.reshape(B, S, 2 * d_rnn)
    q = _project(x16, w_q, BF16, tm_in, 1024,
                 out_scale=(HEAD_DIM ** -0.5) * LOG2E).reshape(B, S, q_w)
    kvw = _project(x16, w_kvw, F32, tm_in, w_kvw.shape[1]).reshape(B, S, w_kvw.shape[1])
    qi = _project(x16, w_qi, BF16, tm_in, 1024).reshape(B, S, qi_w)
    gates = _project(x16, w_g, BF16, tm_in, 1024)

    y_lru = _rg_lru(lxg, lru_conv_w[l], lru_conv_b[l], lru_gate_a_w[l], lru_gate_a_b[l],
                    lru_gate_x_w[l], lru_gate_x_b[l], lru_lambda[l], S, B, tc=min(S, 256), cbw=256)

    y_att = _sparse_attention(qi, kvw, q, rel_bias, idx_knorm_g[l], idx_knorm_b[l], top_k)

    merged = _gated_merge(y_lru.reshape(M, d_rnn), y_att.reshape(M, q_w),
                          w_proj_lru[l].astype(BF16), w_proj_attn[l].astype(BF16),
                          gates, tm=min(M, 1024), tn=512)
    h32, h16 = _outproj_ln(merged, x2d, w_out[l].astype(BF16), ln1_g[l], ln1_b[l], tm=512)
    act = _ffn_up(h16, ffn_w_up[l], ffn_conv_w[l], ffn_conv_b[l], S,
                  tm=min(S, 1024), tn=512)
    out = _ffn_down_ln(act, h32, ffn_w_down[l].astype(BF16), ln2_g[l], ln2_b[l],
                       tm=min(M, 1024), tk=1024)
    return out.reshape(B, S, D)
```

```python
import functools
import math

import jax
import jax.numpy as jnp
from jax import lax
from jax.experimental import pallas as pl
from jax.experimental.pallas import tpu as pltpu

N_HEADS = 16
HEAD_DIM = 128
IDX_HEADS = 16
IDX_DIM = 64
TOPK_MAX = 256
LRU_BLOCK = 128
LRU_C = 8.0
REL_BUCKETS = 32
REL_MAX_DIST = 128
LN_EPS = 1e-5
DEPTH = 1
ALPHA = (2.0 * DEPTH) ** 0.25

LANES = 128
SUBLANES = 8
VMEM_LIMIT_BYTES = 56 * 1024 * 1024

TQ = 256
CK = 256
HEAD_GROUP = 8
BIAS_TILE_STEP = 256
N_BIAS_TILES = 2
assert TQ % BIAS_TILE_STEP == 0 and CK % BIAS_TILE_STEP == 0
assert N_BIAS_TILES * BIAS_TILE_STEP - (CK - 1) >= REL_MAX_DIST

INT_MIN = -(2 ** 31)
MASK_LOGIT = -2e30
LOG2E = math.log2(math.e)
F32 = jnp.float32
BF16 = jnp.bfloat16


def _params(n_axes):
    return pltpu.CompilerParams(dimension_semantics=("arbitrary",) * n_axes,
                                vmem_limit_bytes=VMEM_LIMIT_BYTES)


def _matmul_kernel(x_ref, w_ref, o_ref, *, out_scale):
    acc = jnp.dot(x_ref[...], w_ref[...], preferred_element_type=F32)
    if out_scale != 1.0:
        acc = acc * out_scale
    o_ref[...] = acc.astype(o_ref.dtype)


def _project(x2d, w, out_dtype, tm, tn, out_scale=1.0):
    M, K = x2d.shape
    N = w.shape[1]
    return pl.pallas_call(
        functools.partial(_matmul_kernel, out_scale=out_scale),
        grid=(M // tm, N // tn),
        in_specs=[pl.BlockSpec((tm, K), lambda i, j: (i, 0)),
                  pl.BlockSpec((K, tn), lambda i, j: (0, j))],
        out_specs=pl.BlockSpec((tm, tn), lambda i, j: (i, j)),
        out_shape=jax.ShapeDtypeStruct((M, N), out_dtype),
        compiler_params=_params(2),
        name="in_proj",
    )(x2d, w)


def _matmul_w32_kernel(x_ref, w_ref, o_ref, w16, *, out_scale):
    @pl.when(pl.program_id(1) == 0)
    def _():
        w16[...] = w_ref[...].astype(BF16)

    acc = jnp.dot(x_ref[...], w16[...], preferred_element_type=F32)
    if out_scale != 1.0:
        acc = acc * out_scale
    o_ref[...] = acc.astype(o_ref.dtype)


def _project_w32(x2d, w32, col0, n_cols, out_dtype, tm, tn, out_scale=1.0):
    M, K = x2d.shape
    assert col0 % tn == 0 and n_cols % tn == 0
    j0 = col0 // tn
    return pl.pallas_call(
        functools.partial(_matmul_w32_kernel, out_scale=out_scale),
        grid=(n_cols // tn, M // tm),
        in_specs=[pl.BlockSpec((tm, K), lambda j, i: (i, 0)),
                  pl.BlockSpec((K, tn), lambda j, i: (0, j0 + j))],
        out_specs=pl.BlockSpec((tm, tn), lambda j, i: (i, j)),
        out_shape=jax.ShapeDtypeStruct((M, n_cols), out_dtype),
        scratch_shapes=[pltpu.VMEM((K, tn), BF16)],
        compiler_params=_params(2),
        name="in_proj_w32",
    )(x2d, w32)


def _matmul_cast_kernel(x_ref, w_ref, o_ref, x16_ref):
    @pl.when(pl.program_id(1) == 0)
    def _():
        x16_ref[...] = x_ref[...].astype(BF16)

    o_ref[...] = jnp.dot(x16_ref[...], w_ref[...], preferred_element_type=F32).astype(o_ref.dtype)


def _project_and_cast(x2d, w, out_dtype, tm, tn):
    M, K = x2d.shape
    N = w.shape[1]
    return pl.pallas_call(
        _matmul_cast_kernel,
        grid=(M // tm, N // tn),
        in_specs=[pl.BlockSpec((tm, K), lambda i, j: (i, 0)),
                  pl.BlockSpec((K, tn), lambda i, j: (0, j))],
        out_specs=[pl.BlockSpec((tm, tn), lambda i, j: (i, j)),
                   pl.BlockSpec((tm, K), lambda i, j: (i, 0))],
        out_shape=[jax.ShapeDtypeStruct((M, N), out_dtype),
                   jax.ShapeDtypeStruct((M, K), BF16)],
        compiler_params=_params(2),
        name="in_proj_cast",
    )(x2d, w)


def _gelu_tanh(x):
    c = math.sqrt(2.0 / math.pi)
    half = 0.5 * x
    return half + half * jnp.tanh(x * (c + (c * 0.044715) * (x * x)))


def _sigmoid(x):
    return 0.5 * jnp.tanh(0.5 * x) + 0.5


def _lru_kernel(lx_ref, lg_ref, cw_ref, cb_ref, wa_ref, ba_ref, wx_ref, bx_ref, lam_ref, o_ref,
                xs, a_s, u_s, h_s, *, tc, cb_width, conv_w):
    ti = pl.program_id(1)
    halo_rows = (conv_w - 1) * SUBLANES
    n_slab = cb_width // LANES
    rows = tc * SUBLANES
    slab = lambda v, s: v[:, s * LANES:(s + 1) * LANES]

    @pl.when(ti == 0)
    def _():
        xs[:, 0:halo_rows, :] = jnp.zeros((n_slab, halo_rows, LANES), F32)
        h_s[...] = jnp.zeros(h_s.shape, F32)

    for b in range(SUBLANES):
        for s in range(n_slab):
            xs[s, pl.ds(halo_rows + b, tc, stride=SUBLANES), :] = (
                lx_ref[b, :, s * LANES:(s + 1) * LANES].astype(F32))
    y_parts = []
    for s in range(n_slab):
        cs = slice(s * LANES, (s + 1) * LANES)
        y = cb_ref[:, cs] + cw_ref[0:1, cs] * xs[s, 0:rows, :]
        for j in range(1, conv_w):
            y = y + cw_ref[j:j + 1, cs] * xs[s, j * SUBLANES:j * SUBLANES + rows, :]
        y_parts.append(y)
        xs[s, 0:halo_rows, :] = xs[s, rows:rows + halo_rows, :]
    y2 = jnp.concatenate(y_parts, axis=1)
    yb = y2.astype(BF16)
    r_parts, i_parts = [], []
    for n in range(cb_width // LRU_BLOCK):
        blk = yb[:, n * LRU_BLOCK:(n + 1) * LRU_BLOCK]
        r_parts.append(jnp.dot(blk, wa_ref[n], preferred_element_type=F32))
        i_parts.append(jnp.dot(blk, wx_ref[n], preferred_element_type=F32))
    r = _sigmoid(jnp.concatenate(r_parts, axis=1) + ba_ref[...])
    g_in = _sigmoid(jnp.concatenate(i_parts, axis=1) + bx_ref[...])

    z = -lam_ref[...]
    softplus = jnp.maximum(z, 0.0) + jnp.log1p(jnp.exp(-jnp.abs(z)))
    rate = (-LRU_C) * softplus
    log_a = rate * r
    a = jnp.exp2((rate * LOG2E) * r)
    mult = jnp.sqrt(-jnp.tanh(log_a) * (a * a + 1.0))
    gx = g_in * y2
    u = mult * gx
    for s in range(n_slab):
        a_s[s] = slab(a, s)
        u_s[s] = slab(u, s)

    @pl.when(ti == 0)
    def _():
        for s in range(n_slab):
            u_s[s, 0:SUBLANES, :] = slab(gx, s)[0:SUBLANES]

    def step(t, h):
        r0 = pl.multiple_of(t * SUBLANES, SUBLANES)
        new = []
        for s in range(n_slab):
            hs = a_s[s, pl.ds(r0, SUBLANES), :] * h[s] + u_s[s, pl.ds(r0, SUBLANES), :]
            u_s[s, pl.ds(r0, SUBLANES), :] = hs
            new.append(hs)
        return tuple(new)

    h_fin = lax.fori_loop(0, tc, step, tuple(h_s[s] for s in range(n_slab)), unroll=8)
    for s in range(n_slab):
        h_s[s] = h_fin[s]
    for b in range(SUBLANES):
        for s in range(n_slab):
            cs = slice(s * LANES, (s + 1) * LANES)
            h_b = u_s[s, pl.ds(b, tc, stride=SUBLANES), :]
            o_ref[b, :, cs] = (_gelu_tanh(lg_ref[b, :, cs].astype(F32)) * h_b).astype(o_ref.dtype)


def _rg_lru(lxg, conv_w, conv_b, gate_a_w, gate_a_b, gate_x_w, gate_x_b, lam, S, B, tc, cbw):
    d_rnn = conv_w.shape[1]
    width = conv_w.shape[0]
    ncb = d_rnn // cbw
    nblk = cbw // LRU_BLOCK
    n_slab = cbw // LANES
    row = lambda v: v.reshape(1, d_rnn)
    kern = functools.partial(_lru_kernel, tc=tc, cb_width=cbw, conv_w=width)
    vec_spec = pl.BlockSpec((1, cbw), lambda n, t: (0, n))
    return pl.pallas_call(
        kern,
        grid=(ncb, S // tc),
        in_specs=[pl.BlockSpec((B, tc, cbw), lambda n, t: (0, t, n)),
                  pl.BlockSpec((B, tc, cbw), lambda n, t: (0, t, ncb + n)),
                  pl.BlockSpec((width, cbw), lambda n, t: (0, n)),
                  vec_spec,
                  pl.BlockSpec((nblk, LRU_BLOCK, LRU_BLOCK), lambda n, t: (n, 0, 0)),
                  vec_spec,
                  pl.BlockSpec((nblk, LRU_BLOCK, LRU_BLOCK), lambda n, t: (n, 0, 0)),
                  vec_spec,
                  vec_spec],
        out_specs=pl.BlockSpec((B, tc, cbw), lambda n, t: (0, t, n)),
        out_shape=jax.ShapeDtypeStruct((B, S, d_rnn), BF16),
        scratch_shapes=[pltpu.VMEM((n_slab, (tc + width - 1) * B, LANES), F32),
                        pltpu.VMEM((n_slab, tc * B, LANES), F32),
                        pltpu.VMEM((n_slab, tc * B, LANES), F32),
                        pltpu.VMEM((n_slab, B, LANES), F32)],
        compiler_params=_params(2),
        name="rg_lru",
    )(lxg, lxg, conv_w, row(conv_b), gate_a_w.astype(BF16), row(gate_a_b),
      gate_x_w.astype(BF16), row(gate_x_b), row(lam))


I16_MIN = -(2 ** 15)
I16_MAX = 2 ** 15 - 1


def _far_bucket():
    max_exact = REL_BUCKETS // 2
    large = max_exact + int(math.log(REL_MAX_DIST / max_exact) / math.log(REL_MAX_DIST / max_exact)
                            * (REL_BUCKETS - max_exact))
    return min(large, REL_BUCKETS - 1)


def _fill_bias_tiles(rb_ref, bias_s):
    rows = lax.broadcasted_iota(jnp.int32, (TQ, CK), 0)
    cols = lax.broadcasted_iota(jnp.int32, (TQ, CK), 1)
    max_exact = REL_BUCKETS // 2

    def tile_body(n, carry):
        d = n // N_HEADS
        h = n % N_HEADS
        rel = d * BIAS_TILE_STEP + rows - cols
        nf = jnp.maximum(rel, 1).astype(F32)
        large = max_exact + (jnp.log(nf / max_exact) / math.log(REL_MAX_DIST / max_exact)
                             * (REL_BUCKETS - max_exact)).astype(jnp.int32)
        large = jnp.minimum(large, REL_BUCKETS - 1)
        bucket = jnp.where(rel < max_exact, rel, large)
        acc = jnp.zeros((TQ, CK), F32)
        for b in range(REL_BUCKETS):
            acc = jnp.where(bucket == b, rb_ref[b, h], acc)
        bias_s[d, h] = (acc - rb_ref[_far_bucket(), h]) * LOG2E
        return carry

    lax.fori_loop(0, N_BIAS_TILES * N_HEADS, tile_body, 0)


def _ordered_key(score):
    bits = pltpu.bitcast(score, jnp.int32)
    return jnp.where(bits >= 0, bits, bits ^ jnp.int32(0x7FFFFFFF))


def _count_ge(arr, n_pairs, cand):
    cand16 = jnp.broadcast_to(cand, (TQ, LANES)).astype(jnp.int16)
    cand16 = jnp.concatenate([cand16] * (CK // LANES), axis=1)
    one, zero = jnp.int16(1), jnp.int16(0)

    def count_pair(p, cnt):
        cnt = cnt + jnp.where(arr[p, 0] >= cand16, one, zero)
        return cnt + jnp.where(arr[p, 1] >= cand16, one, zero)

    cnt = lax.fori_loop(0, n_pairs, count_pair, jnp.zeros((TQ, CK), jnp.int16))
    return jnp.sum(cnt.astype(jnp.int32).astype(F32), axis=1, keepdims=True)


def _kth_largest_i16(arr, n_pairs, need):
    def bit_step(step, thr):
        cand = thr + lax.shift_left(jnp.int32(1), 15 - step)
        return jnp.where(_count_ge(arr, n_pairs, cand) >= need, cand, thr)

    return lax.fori_loop(0, 16, bit_step, jnp.full((TQ, 1), I16_MIN, jnp.int32))


def _for_chunks(lo, hi, chunks_fn):
    n = hi - lo

    def pair(p, carry):
        chunks_fn(lo + 2 * p, 2)
        return carry

    lax.fori_loop(0, n // 2, pair, 0)

    @pl.when(n % 2 == 1)
    def _():
        chunks_fn(hi - 1, 1)


def _attn_kernel(rb_ref, qi_ref, kvw_ref, q_ref, kg_ref, kb_ref, o_ref,
                 bias_s, kin_t, k_t, v_aug, keys, hi_s, lo_s, qis, wbuf, qs, pbuf, m_s, accl_s,
                 *, top_k, seq):
    i = pl.program_id(1)
    k_off, v_off, kw_off = 0, HEAD_DIM, 2 * HEAD_DIM
    n_all = seq // CK
    lane = lax.broadcasted_iota(jnp.int32, (TQ, LANES), 1)

    @pl.when((pl.program_id(0) == 0) & (i == 0))
    def _():
        _fill_bias_tiles(rb_ref, bias_s)

    @pl.when(i == 0)
    def _():
        kraw = kvw_ref[:, kw_off:kw_off + IDX_DIM]
        mu = jnp.mean(kraw, axis=-1, keepdims=True)
        var = jnp.mean(jnp.square(kraw - mu), axis=-1, keepdims=True)
        kn = (kraw - mu) * lax.rsqrt(var + LN_EPS) * kg_ref[...] + kb_ref[...]
        kn = jnp.concatenate([kn, jnp.zeros((seq, LANES - IDX_DIM), F32)], axis=1)
        sub = lax.broadcasted_iota(jnp.int32, (HEAD_DIM, CK), 0)
        ones_rows = jnp.where(sub < 2, 1.0, 0.0).astype(BF16)
        for c in range(n_all):
            kin_t[c] = kn[c * CK:(c + 1) * CK, :].T[0:IDX_DIM].astype(BF16)
            k_t[c, 0:HEAD_DIM, :] = kvw_ref[c * CK:(c + 1) * CK, k_off:k_off + HEAD_DIM].T.astype(BF16)
            k_t[c, HEAD_DIM:2 * HEAD_DIM, :] = ones_rows
        v_aug[:, 0:HEAD_DIM] = kvw_ref[:, v_off:v_off + HEAD_DIM].astype(BF16)
        v_aug[:, HEAD_DIM:2 * HEAD_DIM] = jnp.ones((seq, HEAD_DIM), BF16)
        for h in range(N_HEADS):
            fb = jnp.full((TQ, LANES), rb_ref[_far_bucket(), h] * LOG2E, F32)
            fb_hi = fb.astype(BF16).astype(F32)
            qs[h * TQ:(h + 1) * TQ, HEAD_DIM:2 * HEAD_DIM] = jnp.where(
                lane == 0, fb_hi, jnp.where(lane == 1, fb - fb_hi, 0.0)).astype(BF16)

    q0 = pl.multiple_of(i * TQ, TQ)
    n_chunks = (q0 + TQ + CK - 1) // CK
    n_pairs = (n_chunks + 1) // 2
    rows = q0 + lax.broadcasted_iota(jnp.int32, (TQ, CK), 0)
    cols = lax.broadcasted_iota(jnp.int32, (TQ, CK), 1)

    qib = qi_ref[...]
    wi = kvw_ref[pl.ds(q0, TQ), kw_off + IDX_DIM:kw_off + IDX_DIM + IDX_HEADS]
    wi = wi * ((IDX_DIM ** -0.5) * (IDX_HEADS ** -0.5))
    for h in range(IDX_HEADS):
        qis[h * TQ:(h + 1) * TQ, :] = qib[:, h * IDX_DIM:(h + 1) * IDX_DIM]
        wbuf[h] = jnp.broadcast_to(wi[:, h:h + 1], (TQ, LANES))

    def score_chunk(c):
        kc = kin_t[c]
        acc = jnp.zeros((TQ, CK), F32)
        for g in range(IDX_HEADS // HEAD_GROUP):
            g0 = g * HEAD_GROUP * TQ
            d_g = jnp.dot(qis[g0:g0 + HEAD_GROUP * TQ, :], kc, preferred_element_type=F32)
            for hh in range(HEAD_GROUP):
                w_h = wbuf[g * HEAD_GROUP + hh]
                acc = acc + jnp.concatenate([w_h] * (CK // LANES), axis=1) * jnp.maximum(
                    d_g[hh * TQ:(hh + 1) * TQ], 0.0)
        key = jnp.where(c * CK + cols <= rows, _ordered_key(acc), jnp.int32(INT_MIN))
        keys[c] = key
        hi_s[c // 2, c % 2] = lax.shift_right_arithmetic(key, 16).astype(jnp.int16)
        lo_s[c // 2, c % 2] = ((key & 0xFFFF) + I16_MIN).astype(jnp.int16)

    def score_chunks(c, width):
        for k in range(width):
            score_chunk(c + k)

    _for_chunks(0, n_chunks, score_chunks)

    @pl.when(n_chunks % 2 == 1)
    def _():
        hi_s[n_chunks // 2, 1] = jnp.full((TQ, CK), I16_MIN, jnp.int16)
        lo_s[n_chunks // 2, 1] = jnp.full((TQ, CK), I16_MIN, jnp.int16)

    need = jnp.full((TQ, 1), float(top_k), F32)
    hi_k = _kth_largest_i16(hi_s, n_pairs, need)
    above = jnp.where(hi_k == I16_MAX, 0.0,
                      _count_ge(hi_s, n_pairs, jnp.minimum(hi_k + 1, I16_MAX)))
    hi_k16 = jnp.broadcast_to(hi_k, (TQ, LANES)).astype(jnp.int16)
    hi_k16 = jnp.concatenate([hi_k16] * (CK // LANES), axis=1)

    def keep_group(p, carry):
        for j in range(2):
            lo_s[p, j] = jnp.where(hi_s[p, j] == hi_k16, lo_s[p, j], jnp.int16(I16_MIN))
        return carry

    lax.fori_loop(0, n_pairs, keep_group, 0)
    lo_k = _kth_largest_i16(lo_s, n_pairs, need - above)
    thr = hi_k * 65536 + (lo_k - I16_MIN)
    thr = jnp.maximum(thr, jnp.int32(INT_MIN + 1))

    for h in range(N_HEADS):
        qs[h * TQ:(h + 1) * TQ, 0:HEAD_DIM] = q_ref[:, h * HEAD_DIM:(h + 1) * HEAD_DIM]
    m_s[...] = jnp.full(m_s.shape, 0.5 * MASK_LOGIT, F32)
    accl_s[...] = jnp.zeros(accl_s.shape, F32)

    def attn_chunk(c, near):
        kc = k_t[c]
        vc = v_aug[pl.ds(pl.multiple_of(c * CK, CK), CK), :]
        sel = keys[c] >= thr
        tile = (q0 - c * CK) // BIAS_TILE_STEP
        for g in range(N_HEADS // HEAD_GROUP):
            g0 = g * HEAD_GROUP * TQ
            s_g = jnp.dot(qs[g0:g0 + HEAD_GROUP * TQ, :], kc, preferred_element_type=F32)
            alphas = []
            for hh in range(HEAD_GROUP):
                h = g * HEAD_GROUP + hh
                s = s_g[hh * TQ:(hh + 1) * TQ]
                if near:
                    s = s + bias_s[tile, h]
                s = jnp.where(sel, s, MASK_LOGIT)
                m_old = m_s[h]
                m_new = jnp.maximum(m_old, jnp.max(s, axis=1, keepdims=True))
                p = jnp.exp2(s - jnp.concatenate([m_new] * (CK // LANES), axis=1))
                alphas.append(jnp.exp2(m_old - m_new))
                m_s[h] = m_new
                pbuf[h * TQ:(h + 1) * TQ, :] = p.astype(BF16)
            pv = jnp.dot(pbuf[g0:g0 + HEAD_GROUP * TQ, :], vc, preferred_element_type=F32)
            for hh in range(HEAD_GROUP):
                h = g * HEAD_GROUP + hh
                accl_s[h] = (jnp.concatenate([alphas[hh], alphas[hh]], axis=1) * accl_s[h]
                             + pv[hh * TQ:(hh + 1) * TQ])

    def attn_chunks(c, width, *, near):
        for k in range(width):
            attn_chunk(c + k, near)

    n_far = jnp.maximum((q0 - N_BIAS_TILES * BIAS_TILE_STEP) // CK + 1, 0)
    _for_chunks(0, n_far, functools.partial(attn_chunks, near=False))
    _for_chunks(n_far, n_chunks, functools.partial(attn_chunks, near=True))
    for h in range(N_HEADS):
        al = accl_s[h]
        o_ref[:, h * HEAD_DIM:(h + 1) * HEAD_DIM] = (
            al[:, 0:HEAD_DIM] / al[:, HEAD_DIM:2 * HEAD_DIM]).astype(o_ref.dtype)


def _sparse_attention(qi, kvw, q, rel_bias, knorm_g, knorm_b, top_k):
    B, S, q_w = q.shape
    qi_w = qi.shape[2]
    kvw_w = kvw.shape[2]
    n_all = S // CK
    assert n_all % 2 == 0
    kern = functools.partial(_attn_kernel, top_k=top_k, seq=S)
    return pl.pallas_call(
        kern,
        grid=(B, S // TQ),
        in_specs=[pl.BlockSpec(memory_space=pltpu.SMEM),
                  pl.BlockSpec((None, TQ, qi_w), lambda b, i: (b, i, 0)),
                  pl.BlockSpec((None, S, kvw_w), lambda b, i: (b, 0, 0)),
                  pl.BlockSpec((None, TQ, q_w), lambda b, i: (b, i, 0)),
                  pl.BlockSpec((1, IDX_DIM), lambda b, i: (0, 0)),
                  pl.BlockSpec((1, IDX_DIM), lambda b, i: (0, 0))],
        out_specs=pl.BlockSpec((None, TQ, q_w), lambda b, i: (b, i, 0)),
        out_shape=jax.ShapeDtypeStruct((B, S, q_w), BF16),
        scratch_shapes=[pltpu.VMEM((N_BIAS_TILES, N_HEADS, TQ, CK), F32),
                        pltpu.VMEM((n_all, IDX_DIM, CK), BF16),
                        pltpu.VMEM((n_all, 2 * HEAD_DIM, CK), BF16),
                        pltpu.VMEM((S, 2 * HEAD_DIM), BF16),
                        pltpu.VMEM((n_all, TQ, CK), jnp.int32),
                        pltpu.VMEM((n_all // 2, 2, TQ, CK), jnp.int16),
                        pltpu.VMEM((n_all // 2, 2, TQ, CK), jnp.int16),
                        pltpu.VMEM((IDX_HEADS * TQ, IDX_DIM), BF16),
                        pltpu.VMEM((IDX_HEADS, TQ, LANES), F32),
                        pltpu.VMEM((N_HEADS * TQ, 2 * HEAD_DIM), BF16),
                        pltpu.VMEM((N_HEADS * TQ, CK), BF16),
                        pltpu.VMEM((N_HEADS, TQ, LANES), F32),
                        pltpu.VMEM((N_HEADS, TQ, 2 * HEAD_DIM), F32)],
        compiler_params=_params(2),
        name="dsa_attention",
    )(rel_bias, qi, kvw, q, knorm_g.reshape(1, IDX_DIM), knorm_b.reshape(1, IDX_DIM))


def _merge_kernel(yl_ref, ya_ref, wl_ref, wa_ref, gl_ref, ga_ref, o_ref):
    pl_ = jnp.dot(yl_ref[...], wl_ref[...], preferred_element_type=F32)
    pa_ = jnp.dot(ya_ref[...], wa_ref[...], preferred_element_type=F32)
    merged = (_sigmoid(gl_ref[...].astype(F32)) * pl_
              + _sigmoid(ga_ref[...].astype(F32)) * pa_)
    o_ref[...] = merged.astype(o_ref.dtype)


def _gated_merge(y_lru, y_att, w_l, w_a, gates, tm, tn):
    M, D = y_lru.shape
    N = w_l.shape[1]
    nj = N // tn
    return pl.pallas_call(
        _merge_kernel,
        grid=(M // tm, nj),
        in_specs=[pl.BlockSpec((tm, D), lambda i, j: (i, 0)),
                  pl.BlockSpec((tm, D), lambda i, j: (i, 0)),
                  pl.BlockSpec((D, tn), lambda i, j: (0, j)),
                  pl.BlockSpec((D, tn), lambda i, j: (0, j)),
                  pl.BlockSpec((tm, tn), lambda i, j: (i, j)),
                  pl.BlockSpec((tm, tn), lambda i, j: (i, nj + j))],
        out_specs=pl.BlockSpec((tm, tn), lambda i, j: (i, j)),
        out_shape=jax.ShapeDtypeStruct((M, N), BF16),
        compiler_params=_params(2),
        name="gated_merge",
    )(y_lru, y_att, w_l, w_a, gates, gates)


def _layer_norm_rows(v, g, b):
    mu = jnp.mean(v, axis=-1, keepdims=True)
    var = jnp.mean(jnp.square(v - mu), axis=-1, keepdims=True)
    return (v - mu) * lax.rsqrt(var + LN_EPS) * g + b


def _outproj_ln_kernel(m_ref, x_ref, w_ref, g_ref, b_ref, o32_ref, o16_ref):
    proj = jnp.dot(m_ref[...], w_ref[...], preferred_element_type=F32)
    h = _layer_norm_rows(ALPHA * x_ref[...] + proj, g_ref[...], b_ref[...])
    o32_ref[...] = h
    o16_ref[...] = h.astype(BF16)


def _outproj_ln(merged, x2d, w_out, g, b, tm):
    M, D = x2d.shape
    return pl.pallas_call(
        _outproj_ln_kernel,
        grid=(M // tm,),
        in_specs=[pl.BlockSpec((tm, D), lambda i: (i, 0)),
                  pl.BlockSpec((tm, D), lambda i: (i, 0)),
                  pl.BlockSpec((D, D), lambda i: (0, 0)),
                  pl.BlockSpec((1, D), lambda i: (0, 0)),
                  pl.BlockSpec((1, D), lambda i: (0, 0))],
        out_specs=[pl.BlockSpec((tm, D), lambda i: (i, 0)),
                   pl.BlockSpec((tm, D), lambda i: (i, 0))],
        out_shape=[jax.ShapeDtypeStruct((M, D), F32),
                   jax.ShapeDtypeStruct((M, D), BF16)],
        compiler_params=_params(1),
        name="outproj_ln1",
    )(merged, x2d, w_out, g.reshape(1, D), b.reshape(1, D))


def _ffn_up_kernel(halo_ref, h_ref, wg_ref, wv_ref, cwg_ref, cwv_ref, cbg_ref, cbv_ref, o_ref,
                   wg16, wv16, *, tm, conv_w, tiles_per_seq):
    i = pl.program_id(1)

    @pl.when(i == 0)
    def _():
        wg16[...] = wg_ref[...].astype(BF16)
        wv16[...] = wv_ref[...].astype(BF16)

    keep = jnp.where(i % tiles_per_seq > 0, 1.0, 0.0).astype(BF16)
    lhs = jnp.concatenate([halo_ref[...] * keep, h_ref[...]], axis=0)

    def conv_half(lhs_rows, n_rows, w_ref, cw_ref, cb_ref):
        up = jnp.dot(lhs_rows, w_ref[...], preferred_element_type=F32)
        y = cb_ref[...] + cw_ref[conv_w - 1:conv_w, :] * up[SUBLANES:SUBLANES + n_rows]
        for j in range(conv_w - 1):
            back = pltpu.roll(up, conv_w - 1 - j, axis=0)
            y = y + cw_ref[j:j + 1, :] * back[SUBLANES:SUBLANES + n_rows]
        return y

    yg = conv_half(lhs, tm, wg16, cwg_ref, cbg_ref)
    yv = conv_half(lhs, tm, wv16, cwv_ref, cbv_ref)
    o_ref[...] = (_gelu_tanh(yg) * yv).astype(o_ref.dtype)


def _ffn_up(h16, w_up, conv_w, conv_b, S, tm, tn):
    M, D = h16.shape
    d_ff = w_up.shape[1] // 2
    width = conv_w.shape[0]
    assert width - 1 <= SUBLANES and S % tm == 0
    nj = d_ff // tn
    kern = functools.partial(_ffn_up_kernel, tm=tm, conv_w=width, tiles_per_seq=S // tm)
    halo_blocks = tm // SUBLANES
    cb2 = conv_b.reshape(1, 2 * d_ff)
    return pl.pallas_call(
        kern,
        grid=(nj, M // tm),
        in_specs=[pl.BlockSpec((SUBLANES, D), lambda j, i: (jnp.maximum(i * halo_blocks - 1, 0), 0)),
                  pl.BlockSpec((tm, D), lambda j, i: (i, 0)),
                  pl.BlockSpec((D, tn), lambda j, i: (0, j)),
                  pl.BlockSpec((D, tn), lambda j, i: (0, nj + j)),
                  pl.BlockSpec((width, tn), lambda j, i: (0, j)),
                  pl.BlockSpec((width, tn), lambda j, i: (0, nj + j)),
                  pl.BlockSpec((1, tn), lambda j, i: (0, j)),
                  pl.BlockSpec((1, tn), lambda j, i: (0, nj + j))],
        out_specs=pl.BlockSpec((tm, tn), lambda j, i: (i, j)),
        out_shape=jax.ShapeDtypeStruct((M, d_ff), BF16),
        scratch_shapes=[pltpu.VMEM((D, tn), BF16), pltpu.VMEM((D, tn), BF16)],
        compiler_params=_params(2),
        name="ffn_up_conv_geglu",
    )(h16, h16, w_up, w_up, conv_w, conv_w, cb2, cb2)


def _ffn_down_kernel(a_ref, h_ref, w_ref, g_ref, b_ref, o_ref):
    k = pl.program_id(1)

    @pl.when(k == 0)
    def _():
        o_ref[...] = jnp.dot(a_ref[...], w_ref[...], preferred_element_type=F32)

    @pl.when(k > 0)
    def _():
        o_ref[...] += jnp.dot(a_ref[...], w_ref[...], preferred_element_type=F32)

    @pl.when(k == pl.num_programs(1) - 1)
    def _():
        o_ref[...] = _layer_norm_rows(ALPHA * h_ref[...] + o_ref[...], g_ref[...], b_ref[...])


def _ffn_down_ln(act, h32, w_down, g, b, tm, tk):
    d_ff, D = w_down.shape
    M = act.shape[0]
    return pl.pallas_call(
        _ffn_down_kernel,
        grid=(M // tm, d_ff // tk),
        in_specs=[pl.BlockSpec((tm, tk), lambda i, k: (i, k)),
                  pl.BlockSpec((tm, D), lambda i, k: (i, 0)),
                  pl.BlockSpec((tk, D), lambda i, k: (k, 0)),
                  pl.BlockSpec((1, D), lambda i, k: (0, 0)),
                  pl.BlockSpec((1, D), lambda i, k: (0, 0))],
        out_specs=pl.BlockSpec((tm, D), lambda i, k: (i, 0)),
        out_shape=jax.ShapeDtypeStruct((M, D), F32),
        compiler_params=_params(2),
        name="ffn_down_ln2",
    )(act, h32, w_down, g.reshape(1, D), b.reshape(1, D))


def kernel(x, w_in, lru_conv_w, lru_conv_b, lru_gate_a_w, lru_gate_a_b, lru_gate_x_w, lru_gate_x_b,
           lru_lambda, idx_knorm_g, idx_knorm_b, rel_bias, w_proj_lru, w_proj_attn, w_out,
           ln1_g, ln1_b, ffn_w_up, ffn_conv_w, ffn_conv_b, ffn_w_down, ln2_g, ln2_b):
    B, S, D = x.shape
    assert B == SUBLANES and S % CK == 0 and w_in.shape[0] == DEPTH
    top_k = min(TOPK_MAX, S // 4)
    d_rnn = lru_conv_w.shape[-1]
    q_w = N_HEADS * HEAD_DIM
    qi_w = IDX_HEADS * IDX_DIM
    splits = (d_rnn, d_rnn, q_w, HEAD_DIM, HEAD_DIM, qi_w, IDX_DIM, IDX_HEADS, D, D)
    offs = [0]
    for s_ in splits:
        offs.append(offs[-1] + s_)
    M = B * S

    l = 0
    w = w_in[l]
    x2d = x.reshape(M, D)
    w_lru = w[:, offs[0]:offs[2]].astype(BF16)
    pad = LANES - IDX_DIM - IDX_HEADS
    w_kvw = jnp.concatenate([w[:, offs[3]:offs[5]], w[:, offs[6]:offs[8]],
                             jnp.zeros((D, pad), F32)], axis=1).astype(BF16)
    w_qi = w[:, offs[5]:offs[6]].astype(BF16)
    w_g = w[:, offs[8]:offs[10]]

    tm_in = min(M, 2048)
    tn_in = 1024
    lxg, x16 = _project_and_cast(x2d, w_lru, BF16, min(M, 1024), tn_in)
    lxg = lxg.reshape(B, S, 2 * d_rnn)
    q = _project_w32(x16, w, offs[2], q_w, BF16, tm_in, tn_in,
                     out_scale=(HEAD_DIM ** -0.5) * LOG2E).reshape(B, S, q_w)
    kvw = _project(x16, w_kvw, F32, tm_in, w_kvw.shape[1]).reshape(B, S, w_kvw.shape[1])
    qi = _project(x16, w_qi, BF16, tm_in, tn_in).reshape(B, S, qi_w)
    gates = _project_w32(x16, w_g, 0, 2 * D, BF16, tm_in, tn_in)

    y_lru = _rg_lru(lxg, lru_conv_w[l], lru_conv_b[l], lru_gate_a_w[l], lru_gate_a_b[l],
                    lru_gate_x_w[l], lru_gate_x_b[l], lru_lambda[l], S, B, tc=min(S, 256), cbw=256)

    y_att = _sparse_attention(qi, kvw, q, rel_bias, idx_knorm_g[l], idx_knorm_b[l], top_k)

    merged = _gated_merge(y_lru.reshape(M, d_rnn), y_att.reshape(M, q_w),
                          w_proj_lru[l].astype(BF16), w_proj_attn[l].astype(BF16),
                          gates, tm=min(M, 1024), tn=512)
    h32, h16 = _outproj_ln(merged, x2d, w_out[l].astype(BF16), ln1_g[l], ln1_b[l], tm=512)
    act = _ffn_up(h16, ffn_w_up[l], ffn_conv_w[l], ffn_conv_b[l], S,
                  tm=min(S, 1024), tn=768)
    out = _ffn_down_ln(act, h32, ffn_w_down[l].astype(BF16), ln2_g[l], ln2_b[l],
                       tm=min(M, 1024), tk=1024)
    return out.reshape(B, S, D)
```

```python
import functools
import math

import jax
import jax.numpy as jnp
from jax import lax
from jax.experimental import pallas as pl
from jax.experimental.pallas import tpu as pltpu

N_HEADS = 16
HEAD_DIM = 128
IDX_HEADS = 16
IDX_DIM = 64
TOPK_MAX = 256
LRU_BLOCK = 128
LRU_C = 8.0
REL_BUCKETS = 32
REL_MAX_DIST = 128
LN_EPS = 1e-5
DEPTH = 1
ALPHA = (2.0 * DEPTH) ** 0.25

LANES = 128
SUBLANES = 8
VMEM_LIMIT_BYTES = 56 * 1024 * 1024

TQ = 256
CK = 256
HEAD_GROUP = 16
BIAS_TILE_STEP = 256
N_BIAS_TILES = 2
assert TQ % BIAS_TILE_STEP == 0 and CK % BIAS_TILE_STEP == 0
assert N_BIAS_TILES * BIAS_TILE_STEP - (CK - 1) >= REL_MAX_DIST

INT_MIN = -(2 ** 31)
MASK_LOGIT = -2e30
LOG2E = math.log2(math.e)
F32 = jnp.float32
BF16 = jnp.bfloat16


def _params(n_axes):
    return pltpu.CompilerParams(dimension_semantics=("arbitrary",) * n_axes,
                                vmem_limit_bytes=VMEM_LIMIT_BYTES)


def _matmul_kernel(x_ref, w_ref, o_ref, *, out_scale):
    acc = jnp.dot(x_ref[...], w_ref[...], preferred_element_type=F32)
    if out_scale != 1.0:
        acc = acc * out_scale
    o_ref[...] = acc.astype(o_ref.dtype)


def _project(x2d, w, out_dtype, tm, tn, out_scale=1.0):
    M, K = x2d.shape
    N = w.shape[1]
    return pl.pallas_call(
        functools.partial(_matmul_kernel, out_scale=out_scale),
        grid=(M // tm, N // tn),
        in_specs=[pl.BlockSpec((tm, K), lambda i, j: (i, 0)),
                  pl.BlockSpec((K, tn), lambda i, j: (0, j))],
        out_specs=pl.BlockSpec((tm, tn), lambda i, j: (i, j)),
        out_shape=jax.ShapeDtypeStruct((M, N), out_dtype),
        compiler_params=_params(2),
        name="in_proj",
    )(x2d, w)


def _matmul_cast_kernel(x_ref, w_ref, o_ref, x16_ref):
    @pl.when(pl.program_id(1) == 0)
    def _():
        x16_ref[...] = x_ref[...].astype(BF16)

    o_ref[...] = jnp.dot(x16_ref[...], w_ref[...], preferred_element_type=F32).astype(o_ref.dtype)


def _project_and_cast(x2d, w, out_dtype, tm, tn):
    M, K = x2d.shape
    N = w.shape[1]
    return pl.pallas_call(
        _matmul_cast_kernel,
        grid=(M // tm, N // tn),
        in_specs=[pl.BlockSpec((tm, K), lambda i, j: (i, 0)),
                  pl.BlockSpec((K, tn), lambda i, j: (0, j))],
        out_specs=[pl.BlockSpec((tm, tn), lambda i, j: (i, j)),
                   pl.BlockSpec((tm, K), lambda i, j: (i, 0))],
        out_shape=[jax.ShapeDtypeStruct((M, N), out_dtype),
                   jax.ShapeDtypeStruct((M, K), BF16)],
        compiler_params=_params(2),
        name="in_proj_cast",
    )(x2d, w)


def _gelu_tanh(x):
    c = math.sqrt(2.0 / math.pi)
    half = 0.5 * x
    return half + half * jnp.tanh(x * (c + (c * 0.044715) * (x * x)))


def _sigmoid(x):
    return 0.5 * jnp.tanh(0.5 * x) + 0.5


def _lru_kernel(lx_ref, lg_ref, cw_ref, cb_ref, wa_ref, ba_ref, wx_ref, bx_ref, lam_ref, o_ref,
                xs, a_s, u_s, h_s, *, tc, cb_width, conv_w):
    ti = pl.program_id(1)
    halo_rows = (conv_w - 1) * SUBLANES
    n_slab = cb_width // LANES
    rows = tc * SUBLANES
    slab = lambda v, s: v[:, s * LANES:(s + 1) * LANES]

    @pl.when(ti == 0)
    def _():
        xs[:, 0:halo_rows, :] = jnp.zeros((n_slab, halo_rows, LANES), F32)
        h_s[...] = jnp.zeros(h_s.shape, F32)

    for b in range(SUBLANES):
        for s in range(n_slab):
            xs[s, pl.ds(halo_rows + b, tc, stride=SUBLANES), :] = (
                lx_ref[b, :, s * LANES:(s + 1) * LANES].astype(F32))
    y_parts = []
    for s in range(n_slab):
        cs = slice(s * LANES, (s + 1) * LANES)
        y = cb_ref[:, cs] + cw_ref[0:1, cs] * xs[s, 0:rows, :]
        for j in range(1, conv_w):
            y = y + cw_ref[j:j + 1, cs] * xs[s, j * SUBLANES:j * SUBLANES + rows, :]
        y_parts.append(y)
        xs[s, 0:halo_rows, :] = xs[s, rows:rows + halo_rows, :]
    y2 = jnp.concatenate(y_parts, axis=1)
    yb = y2.astype(BF16)
    r_parts, i_parts = [], []
    for n in range(cb_width // LRU_BLOCK):
        blk = yb[:, n * LRU_BLOCK:(n + 1) * LRU_BLOCK]
        r_parts.append(jnp.dot(blk, wa_ref[n], preferred_element_type=F32))
        i_parts.append(jnp.dot(blk, wx_ref[n], preferred_element_type=F32))
    r = _sigmoid(jnp.concatenate(r_parts, axis=1) + ba_ref[...])
    g_in = _sigmoid(jnp.concatenate(i_parts, axis=1) + bx_ref[...])

    z = -lam_ref[...]
    softplus = jnp.maximum(z, 0.0) + jnp.log1p(jnp.exp(-jnp.abs(z)))
    rate = (-LRU_C) * softplus
    log_a = rate * r
    a = jnp.exp2((rate * LOG2E) * r)
    mult = jnp.sqrt(-jnp.tanh(log_a) * (a * a + 1.0))
    gx = g_in * y2
    u = mult * gx
    for s in range(n_slab):
        a_s[s] = slab(a, s)
        u_s[s] = slab(u, s)

    @pl.when(ti == 0)
    def _():
        for s in range(n_slab):
            u_s[s, 0:SUBLANES, :] = slab(gx, s)[0:SUBLANES]

    def step(t, h):
        r0 = pl.multiple_of(t * SUBLANES, SUBLANES)
        new = []
        for s in range(n_slab):
            hs = a_s[s, pl.ds(r0, SUBLANES), :] * h[s] + u_s[s, pl.ds(r0, SUBLANES), :]
            u_s[s, pl.ds(r0, SUBLANES), :] = hs
            new.append(hs)
        return tuple(new)

    h_fin = lax.fori_loop(0, tc, step, tuple(h_s[s] for s in range(n_slab)), unroll=8)
    for s in range(n_slab):
        h_s[s] = h_fin[s]
    for b in range(SUBLANES):
        for s in range(n_slab):
            cs = slice(s * LANES, (s + 1) * LANES)
            h_b = u_s[s, pl.ds(b, tc, stride=SUBLANES), :]
            o_ref[b, :, cs] = (_gelu_tanh(lg_ref[b, :, cs].astype(F32)) * h_b).astype(o_ref.dtype)


def _rg_lru(lxg, conv_w, conv_b, gate_a_w, gate_a_b, gate_x_w, gate_x_b, lam, S, B, tc, cbw):
    d_rnn = conv_w.shape[1]
    width = conv_w.shape[0]
    ncb = d_rnn // cbw
    nblk = cbw // LRU_BLOCK
    n_slab = cbw // LANES
    row = lambda v: v.reshape(1, d_rnn)
    kern = functools.partial(_lru_kernel, tc=tc, cb_width=cbw, conv_w=width)
    vec_spec = pl.BlockSpec((1, cbw), lambda n, t: (0, n))
    return pl.pallas_call(
        kern,
        grid=(ncb, S // tc),
        in_specs=[pl.BlockSpec((B, tc, cbw), lambda n, t: (0, t, n)),
                  pl.BlockSpec((B, tc, cbw), lambda n, t: (0, t, ncb + n)),
                  pl.BlockSpec((width, cbw), lambda n, t: (0, n)),
                  vec_spec,
                  pl.BlockSpec((nblk, LRU_BLOCK, LRU_BLOCK), lambda n, t: (n, 0, 0)),
                  vec_spec,
                  pl.BlockSpec((nblk, LRU_BLOCK, LRU_BLOCK), lambda n, t: (n, 0, 0)),
                  vec_spec,
                  vec_spec],
        out_specs=pl.BlockSpec((B, tc, cbw), lambda n, t: (0, t, n)),
        out_shape=jax.ShapeDtypeStruct((B, S, d_rnn), BF16),
        scratch_shapes=[pltpu.VMEM((n_slab, (tc + width - 1) * B, LANES), F32),
                        pltpu.VMEM((n_slab, tc * B, LANES), F32),
                        pltpu.VMEM((n_slab, tc * B, LANES), F32),
                        pltpu.VMEM((n_slab, B, LANES), F32)],
        compiler_params=_params(2),
        name="rg_lru",
    )(lxg, lxg, conv_w, row(conv_b), gate_a_w.astype(BF16), row(gate_a_b),
      gate_x_w.astype(BF16), row(gate_x_b), row(lam))


I16_MIN = -(2 ** 15)
I16_MAX = 2 ** 15 - 1


def _far_bucket():
    max_exact = REL_BUCKETS // 2
    large = max_exact + int(math.log(REL_MAX_DIST / max_exact) / math.log(REL_MAX_DIST / max_exact)
                            * (REL_BUCKETS - max_exact))
    return min(large, REL_BUCKETS - 1)


def _fill_bias_tiles(rb_ref, bias_s):
    rows = lax.broadcasted_iota(jnp.int32, (TQ, CK), 0)
    cols = lax.broadcasted_iota(jnp.int32, (TQ, CK), 1)
    max_exact = REL_BUCKETS // 2

    def tile_body(n, carry):
        d = n // N_HEADS
        h = n % N_HEADS
        rel = d * BIAS_TILE_STEP + rows - cols
        nf = jnp.maximum(rel, 1).astype(F32)
        large = max_exact + (jnp.log(nf / max_exact) / math.log(REL_MAX_DIST / max_exact)
                             * (REL_BUCKETS - max_exact)).astype(jnp.int32)
        large = jnp.minimum(large, REL_BUCKETS - 1)
        bucket = jnp.where(rel < max_exact, rel, large)
        acc = jnp.zeros((TQ, CK), F32)
        for b in range(REL_BUCKETS):
            acc = jnp.where(bucket == b, rb_ref[b, h], acc)
        bias_s[d, h] = (acc - rb_ref[_far_bucket(), h]) * LOG2E
        return carry

    lax.fori_loop(0, N_BIAS_TILES * N_HEADS, tile_body, 0)


def _ordered_key(score):
    bits = pltpu.bitcast(score, jnp.int32)
    return jnp.where(bits >= 0, bits, bits ^ jnp.int32(0x7FFFFFFF))


def _count_ge(arr, n_pairs, cand):
    cand16 = jnp.broadcast_to(cand, (TQ, LANES)).astype(jnp.int16)
    cand16 = jnp.concatenate([cand16] * (CK // LANES), axis=1)
    one, zero = jnp.int16(1), jnp.int16(0)

    def count_pair(p, cnt):
        cnt = cnt + jnp.where(arr[p, 0] >= cand16, one, zero)
        return cnt + jnp.where(arr[p, 1] >= cand16, one, zero)

    cnt = lax.fori_loop(0, n_pairs, count_pair, jnp.zeros((TQ, CK), jnp.int16))
    return jnp.sum(cnt.astype(jnp.int32).astype(F32), axis=1, keepdims=True)


def _kth_largest_i16(arr, n_pairs, need):
    def bit_step(step, thr):
        cand = thr + lax.shift_left(jnp.int32(1), 15 - step)
        return jnp.where(_count_ge(arr, n_pairs, cand) >= need, cand, thr)

    return lax.fori_loop(0, 16, bit_step, jnp.full((TQ, 1), I16_MIN, jnp.int32))


def _for_chunks(lo, hi, chunks_fn):
    n = hi - lo

    def pair(p, carry):
        chunks_fn(lo + 2 * p, 2)
        return carry

    lax.fori_loop(0, n // 2, pair, 0)

    @pl.when(n % 2 == 1)
    def _():
        chunks_fn(hi - 1, 1)


def _attn_kernel(rb_ref, qi_ref, kvw_ref, q_ref, kg_ref, kb_ref, o_ref,
                 bias_s, kin_t, k_t, v_aug, keys, hi_s, lo_s, qis, wbuf, qs, pbuf, m_s, accl_s,
                 *, top_k, seq):
    i = pl.program_id(1)
    k_off, v_off, kw_off = 0, HEAD_DIM, 2 * HEAD_DIM
    n_all = seq // CK
    lane = lax.broadcasted_iota(jnp.int32, (TQ, LANES), 1)

    @pl.when((pl.program_id(0) == 0) & (i == 0))
    def _():
        _fill_bias_tiles(rb_ref, bias_s)

    @pl.when(i == 0)
    def _():
        kraw = kvw_ref[:, kw_off:kw_off + IDX_DIM]
        mu = jnp.mean(kraw, axis=-1, keepdims=True)
        var = jnp.mean(jnp.square(kraw - mu), axis=-1, keepdims=True)
        kn = (kraw - mu) * lax.rsqrt(var + LN_EPS) * kg_ref[...] + kb_ref[...]
        kn = jnp.concatenate([kn, jnp.zeros((seq, LANES - IDX_DIM), F32)], axis=1)
        sub = lax.broadcasted_iota(jnp.int32, (HEAD_DIM, CK), 0)
        ones_rows = jnp.where(sub < 2, 1.0, 0.0).astype(BF16)
        for c in range(n_all):
            kin_t[c] = kn[c * CK:(c + 1) * CK, :].T[0:IDX_DIM].astype(BF16)
            k_t[c, 0:HEAD_DIM, :] = kvw_ref[c * CK:(c + 1) * CK, k_off:k_off + HEAD_DIM].T.astype(BF16)
            k_t[c, HEAD_DIM:2 * HEAD_DIM, :] = ones_rows
        v_aug[:, 0:HEAD_DIM] = kvw_ref[:, v_off:v_off + HEAD_DIM].astype(BF16)
        v_aug[:, HEAD_DIM:2 * HEAD_DIM] = jnp.ones((seq, HEAD_DIM), BF16)
        for h in range(N_HEADS):
            fb = jnp.full((TQ, LANES), rb_ref[_far_bucket(), h] * LOG2E, F32)
            fb_hi = fb.astype(BF16).astype(F32)
            qs[h * TQ:(h + 1) * TQ, HEAD_DIM:2 * HEAD_DIM] = jnp.where(
                lane == 0, fb_hi, jnp.where(lane == 1, fb - fb_hi, 0.0)).astype(BF16)

    q0 = pl.multiple_of(i * TQ, TQ)
    n_chunks = (q0 + TQ + CK - 1) // CK
    n_pairs = (n_chunks + 1) // 2
    rows = q0 + lax.broadcasted_iota(jnp.int32, (TQ, CK), 0)
    cols = lax.broadcasted_iota(jnp.int32, (TQ, CK), 1)

    qib = qi_ref[...]
    wi = kvw_ref[pl.ds(q0, TQ), kw_off + IDX_DIM:kw_off + IDX_DIM + IDX_HEADS]
    wi = wi * ((IDX_DIM ** -0.5) * (IDX_HEADS ** -0.5))
    for h in range(IDX_HEADS):
        qis[h * TQ:(h + 1) * TQ, :] = qib[:, h * IDX_DIM:(h + 1) * IDX_DIM]
        wbuf[h] = jnp.broadcast_to(wi[:, h:h + 1], (TQ, LANES))

    def score_chunk(c):
        kc = kin_t[c]
        acc = jnp.zeros((TQ, CK), F32)
        for g in range(IDX_HEADS // HEAD_GROUP):
            g0 = g * HEAD_GROUP * TQ
            d_g = jnp.dot(qis[g0:g0 + HEAD_GROUP * TQ, :], kc, preferred_element_type=F32)
            for hh in range(HEAD_GROUP):
                w_h = wbuf[g * HEAD_GROUP + hh]
                acc = acc + jnp.concatenate([w_h] * (CK // LANES), axis=1) * jnp.maximum(
                    d_g[hh * TQ:(hh + 1) * TQ], 0.0)
        key = jnp.where(c * CK + cols <= rows, _ordered_key(acc), jnp.int32(INT_MIN))
        keys[c] = key
        hi_s[c // 2, c % 2] = lax.shift_right_arithmetic(key, 16).astype(jnp.int16)
        lo_s[c // 2, c % 2] = ((key & 0xFFFF) + I16_MIN).astype(jnp.int16)

    def score_chunks(c, width):
        for k in range(width):
            score_chunk(c + k)

    _for_chunks(0, n_chunks, score_chunks)

    @pl.when(n_chunks % 2 == 1)
    def _():
        hi_s[n_chunks // 2, 1] = jnp.full((TQ, CK), I16_MIN, jnp.int16)
        lo_s[n_chunks // 2, 1] = jnp.full((TQ, CK), I16_MIN, jnp.int16)

    need = jnp.full((TQ, 1), float(top_k), F32)
    hi_k = _kth_largest_i16(hi_s, n_pairs, need)
    above = jnp.where(hi_k == I16_MAX, 0.0,
                      _count_ge(hi_s, n_pairs, jnp.minimum(hi_k + 1, I16_MAX)))
    hi_k16 = jnp.broadcast_to(hi_k, (TQ, LANES)).astype(jnp.int16)
    hi_k16 = jnp.concatenate([hi_k16] * (CK // LANES), axis=1)

    def keep_group(p, carry):
        for j in range(2):
            lo_s[p, j] = jnp.where(hi_s[p, j] == hi_k16, lo_s[p, j], jnp.int16(I16_MIN))
        return carry

    lax.fori_loop(0, n_pairs, keep_group, 0)
    lo_k = _kth_largest_i16(lo_s, n_pairs, need - above)
    thr = hi_k * 65536 + (lo_k - I16_MIN)
    thr = jnp.maximum(thr, jnp.int32(INT_MIN + 1))

    for h in range(N_HEADS):
        qs[h * TQ:(h + 1) * TQ, 0:HEAD_DIM] = q_ref[:, h * HEAD_DIM:(h + 1) * HEAD_DIM]
    m_s[...] = jnp.full(m_s.shape, 0.5 * MASK_LOGIT, F32)
    accl_s[...] = jnp.zeros(accl_s.shape, F32)

    def attn_chunk(c, near):
        kc = k_t[c]
        vc = v_aug[pl.ds(pl.multiple_of(c * CK, CK), CK), :]
        sel = keys[c] >= thr
        tile = (q0 - c * CK) // BIAS_TILE_STEP
        for g in range(N_HEADS // HEAD_GROUP):
            g0 = g * HEAD_GROUP * TQ
            s_g = jnp.dot(qs[g0:g0 + HEAD_GROUP * TQ, :], kc, preferred_element_type=F32)
            alphas = []
            for hh in range(HEAD_GROUP):
                h = g * HEAD_GROUP + hh
                s = s_g[hh * TQ:(hh + 1) * TQ]
                if near:
                    s = s + bias_s[tile, h]
                s = jnp.where(sel, s, MASK_LOGIT)
                m_old = m_s[h]
                m_new = jnp.maximum(m_old, jnp.max(s, axis=1, keepdims=True))
                p = jnp.exp2(s - jnp.concatenate([m_new] * (CK // LANES), axis=1))
                alphas.append(jnp.exp2(m_old - m_new))
                m_s[h] = m_new
                pbuf[h * TQ:(h + 1) * TQ, :] = p.astype(BF16)
            pv = jnp.dot(pbuf[g0:g0 + HEAD_GROUP * TQ, :], vc, preferred_element_type=F32)
            for hh in range(HEAD_GROUP):
                h = g * HEAD_GROUP + hh
                accl_s[h] = (jnp.concatenate([alphas[hh], alphas[hh]], axis=1) * accl_s[h]
                             + pv[hh * TQ:(hh + 1) * TQ])

    def attn_chunks(c, width, *, near):
        for k in range(width):
            attn_chunk(c + k, near)

    n_far = jnp.maximum((q0 - N_BIAS_TILES * BIAS_TILE_STEP) // CK + 1, 0)
    _for_chunks(0, n_far, functools.partial(attn_chunks, near=False))
    _for_chunks(n_far, n_chunks, functools.partial(attn_chunks, near=True))
    for h in range(N_HEADS):
        al = accl_s[h]
        o_ref[:, h * HEAD_DIM:(h + 1) * HEAD_DIM] = (
            al[:, 0:HEAD_DIM] / al[:, HEAD_DIM:2 * HEAD_DIM]).astype(o_ref.dtype)


def _sparse_attention(qi, kvw, q, rel_bias, knorm_g, knorm_b, top_k):
    B, S, q_w = q.shape
    qi_w = qi.shape[2]
    kvw_w = kvw.shape[2]
    n_all = S // CK
    assert n_all % 2 == 0
    kern = functools.partial(_attn_kernel, top_k=top_k, seq=S)
    return pl.pallas_call(
        kern,
        grid=(B, S // TQ),
        in_specs=[pl.BlockSpec(memory_space=pltpu.SMEM),
                  pl.BlockSpec((None, TQ, qi_w), lambda b, i: (b, i, 0)),
                  pl.BlockSpec((None, S, kvw_w), lambda b, i: (b, 0, 0)),
                  pl.BlockSpec((None, TQ, q_w), lambda b, i: (b, i, 0)),
                  pl.BlockSpec((1, IDX_DIM), lambda b, i: (0, 0)),
                  pl.BlockSpec((1, IDX_DIM), lambda b, i: (0, 0))],
        out_specs=pl.BlockSpec((None, TQ, q_w), lambda b, i: (b, i, 0)),
        out_shape=jax.ShapeDtypeStruct((B, S, q_w), BF16),
        scratch_shapes=[pltpu.VMEM((N_BIAS_TILES, N_HEADS, TQ, CK), F32),
                        pltpu.VMEM((n_all, IDX_DIM, CK), BF16),
                        pltpu.VMEM((n_all, 2 * HEAD_DIM, CK), BF16),
                        pltpu.VMEM((S, 2 * HEAD_DIM), BF16),
                        pltpu.VMEM((n_all, TQ, CK), jnp.int32),
                        pltpu.VMEM((n_all // 2, 2, TQ, CK), jnp.int16),
                        pltpu.VMEM((n_all // 2, 2, TQ, CK), jnp.int16),
                        pltpu.VMEM((IDX_HEADS * TQ, IDX_DIM), BF16),
                        pltpu.VMEM((IDX_HEADS, TQ, LANES), F32),
                        pltpu.VMEM((N_HEADS * TQ, 2 * HEAD_DIM), BF16),
                        pltpu.VMEM((N_HEADS * TQ, CK), BF16),
                        pltpu.VMEM((N_HEADS, TQ, LANES), F32),
                        pltpu.VMEM((N_HEADS, TQ, 2 * HEAD_DIM), F32)],
        compiler_params=_params(2),
        name="dsa_attention",
    )(rel_bias, qi, kvw, q, knorm_g.reshape(1, IDX_DIM), knorm_b.reshape(1, IDX_DIM))


def _merge_kernel(yl_ref, ya_ref, wl_ref, wa_ref, gl_ref, ga_ref, o_ref):
    pl_ = jnp.dot(yl_ref[...], wl_ref[...], preferred_element_type=F32)
    pa_ = jnp.dot(ya_ref[...], wa_ref[...], preferred_element_type=F32)
    merged = (_sigmoid(gl_ref[...].astype(F32)) * pl_
              + _sigmoid(ga_ref[...].astype(F32)) * pa_)
    o_ref[...] = merged.astype(o_ref.dtype)


def _gated_merge(y_lru, y_att, w_l, w_a, gates, tm, tn):
    M, D = y_lru.shape
    N = w_l.shape[1]
    nj = N // tn
    return pl.pallas_call(
        _merge_kernel,
        grid=(M // tm, nj),
        in_specs=[pl.BlockSpec((tm, D), lambda i, j: (i, 0)),
                  pl.BlockSpec((tm, D), lambda i, j: (i, 0)),
                  pl.BlockSpec((D, tn), lambda i, j: (0, j)),
                  pl.BlockSpec((D, tn), lambda i, j: (0, j)),
                  pl.BlockSpec((tm, tn), lambda i, j: (i, j)),
                  pl.BlockSpec((tm, tn), lambda i, j: (i, nj + j))],
        out_specs=pl.BlockSpec((tm, tn), lambda i, j: (i, j)),
        out_shape=jax.ShapeDtypeStruct((M, N), BF16),
        compiler_params=_params(2),
        name="gated_merge",
    )(y_lru, y_att, w_l, w_a, gates, gates)


def _layer_norm_rows(v, g, b):
    mu = jnp.mean(v, axis=-1, keepdims=True)
    var = jnp.mean(jnp.square(v - mu), axis=-1, keepdims=True)
    return (v - mu) * lax.rsqrt(var + LN_EPS) * g + b


def _outproj_ln_kernel(m_ref, x_ref, w_ref, g_ref, b_ref, o32_ref, o16_ref):
    proj = jnp.dot(m_ref[...], w_ref[...], preferred_element_type=F32)
    h = _layer_norm_rows(ALPHA * x_ref[...] + proj, g_ref[...], b_ref[...])
    o32_ref[...] = h
    o16_ref[...] = h.astype(BF16)


def _outproj_ln(merged, x2d, w_out, g, b, tm):
    M, D = x2d.shape
    return pl.pallas_call(
        _outproj_ln_kernel,
        grid=(M // tm,),
        in_specs=[pl.BlockSpec((tm, D), lambda i: (i, 0)),
                  pl.BlockSpec((tm, D), lambda i: (i, 0)),
                  pl.BlockSpec((D, D), lambda i: (0, 0)),
                  pl.BlockSpec((1, D), lambda i: (0, 0)),
                  pl.BlockSpec((1, D), lambda i: (0, 0))],
        out_specs=[pl.BlockSpec((tm, D), lambda i: (i, 0)),
                   pl.BlockSpec((tm, D), lambda i: (i, 0))],
        out_shape=[jax.ShapeDtypeStruct((M, D), F32),
                   jax.ShapeDtypeStruct((M, D), BF16)],
        compiler_params=_params(1),
        name="outproj_ln1",
    )(merged, x2d, w_out, g.reshape(1, D), b.reshape(1, D))


def _ffn_up_kernel(halo_ref, h_ref, wg_ref, wv_ref, cwg_ref, cwv_ref, cbg_ref, cbv_ref, o_ref,
                   wg16, wv16, *, tm, conv_w, tiles_per_seq):
    i = pl.program_id(1)

    @pl.when(i == 0)
    def _():
        wg16[...] = wg_ref[...].astype(BF16)
        wv16[...] = wv_ref[...].astype(BF16)

    keep = jnp.where(i % tiles_per_seq > 0, 1.0, 0.0).astype(BF16)
    lhs = jnp.concatenate([halo_ref[...] * keep, h_ref[...]], axis=0)

    def conv_half(lhs_rows, n_rows, w_ref, cw_ref, cb_ref):
        up = jnp.dot(lhs_rows, w_ref[...], preferred_element_type=F32)
        y = cb_ref[...] + cw_ref[conv_w - 1:conv_w, :] * up[SUBLANES:SUBLANES + n_rows]
        for j in range(conv_w - 1):
            back = pltpu.roll(up, conv_w - 1 - j, axis=0)
            y = y + cw_ref[j:j + 1, :] * back[SUBLANES:SUBLANES + n_rows]
        return y

    yg = conv_half(lhs, tm, wg16, cwg_ref, cbg_ref)
    yv = conv_half(lhs, tm, wv16, cwv_ref, cbv_ref)
    o_ref[...] = (_gelu_tanh(yg) * yv).astype(o_ref.dtype)


def _ffn_up(h16, w_up, conv_w, conv_b, S, tm, tn):
    M, D = h16.shape
    d_ff = w_up.shape[1] // 2
    width = conv_w.shape[0]
    assert width - 1 <= SUBLANES and S % tm == 0
    nj = d_ff // tn
    kern = functools.partial(_ffn_up_kernel, tm=tm, conv_w=width, tiles_per_seq=S // tm)
    halo_blocks = tm // SUBLANES
    cb2 = conv_b.reshape(1, 2 * d_ff)
    return pl.pallas_call(
        kern,
        grid=(nj, M // tm),
        in_specs=[pl.BlockSpec((SUBLANES, D), lambda j, i: (jnp.maximum(i * halo_blocks - 1, 0), 0)),
                  pl.BlockSpec((tm, D), lambda j, i: (i, 0)),
                  pl.BlockSpec((D, tn), lambda j, i: (0, j)),
                  pl.BlockSpec((D, tn), lambda j, i: (0, nj + j)),
                  pl.BlockSpec((width, tn), lambda j, i: (0, j)),
                  pl.BlockSpec((width, tn), lambda j, i: (0, nj + j)),
                  pl.BlockSpec((1, tn), lambda j, i: (0, j)),
                  pl.BlockSpec((1, tn), lambda j, i: (0, nj + j))],
        out_specs=pl.BlockSpec((tm, tn), lambda j, i: (i, j)),
        out_shape=jax.ShapeDtypeStruct((M, d_ff), BF16),
        scratch_shapes=[pltpu.VMEM((D, tn), BF16), pltpu.VMEM((D, tn), BF16)],
        compiler_params=_params(2),
        name="ffn_up_conv_geglu",
    )(h16, h16, w_up, w_up, conv_w, conv_w, cb2, cb2)


def _ffn_down_kernel(a_ref, h_ref, w_ref, g_ref, b_ref, o_ref):
    k = pl.program_id(1)

    @pl.when(k == 0)
    def _():
        o_ref[...] = jnp.dot(a_ref[...], w_ref[...], preferred_element_type=F32)

    @pl.when(k > 0)
    def _():
        o_ref[...] += jnp.dot(a_ref[...], w_ref[...], preferred_element_type=F32)

    @pl.when(k == pl.num_programs(1) - 1)
    def _():
        o_ref[...] = _layer_norm_rows(ALPHA * h_ref[...] + o_ref[...], g_ref[...], b_ref[...])


def _ffn_down_ln(act, h32, w_down, g, b, tm, tk):
    d_ff, D = w_down.shape
    M = act.shape[0]
    return pl.pallas_call(
        _ffn_down_kernel,
        grid=(M // tm, d_ff // tk),
        in_specs=[pl.BlockSpec((tm, tk), lambda i, k: (i, k)),
                  pl.BlockSpec((tm, D), lambda i, k: (i, 0)),
                  pl.BlockSpec((tk, D), lambda i, k: (k, 0)),
                  pl.BlockSpec((1, D), lambda i, k: (0, 0)),
                  pl.BlockSpec((1, D), lambda i, k: (0, 0))],
        out_specs=pl.BlockSpec((tm, D), lambda i, k: (i, 0)),
        out_shape=jax.ShapeDtypeStruct((M, D), F32),
        compiler_params=_params(2),
        name="ffn_down_ln2",
    )(act, h32, w_down, g.reshape(1, D), b.reshape(1, D))


def kernel(x, w_in, lru_conv_w, lru_conv_b, lru_gate_a_w, lru_gate_a_b, lru_gate_x_w, lru_gate_x_b,
           lru_lambda, idx_knorm_g, idx_knorm_b, rel_bias, w_proj_lru, w_proj_attn, w_out,
           ln1_g, ln1_b, ffn_w_up, ffn_conv_w, ffn_conv_b, ffn_w_down, ln2_g, ln2_b):
    B, S, D = x.shape
    assert B == SUBLANES and S % CK == 0 and w_in.shape[0] == DEPTH
    top_k = min(TOPK_MAX, S // 4)
    d_rnn = lru_conv_w.shape[-1]
    q_w = N_HEADS * HEAD_DIM
    qi_w = IDX_HEADS * IDX_DIM
    splits = (d_rnn, d_rnn, q_w, HEAD_DIM, HEAD_DIM, qi_w, IDX_DIM, IDX_HEADS, D, D)
    offs = [0]
    for s_ in splits:
        offs.append(offs[-1] + s_)
    M = B * S

    l = 0
    w = w_in[l].astype(BF16)
    x2d = x.reshape(M, D)
    w_lru = w[:, offs[0]:offs[2]]
    w_q = w[:, offs[2]:offs[3]]
    pad = LANES - IDX_DIM - IDX_HEADS
    w_kvw = jnp.concatenate([w[:, offs[3]:offs[5]], w[:, offs[6]:offs[8]],
                             jnp.zeros((D, pad), BF16)], axis=1)
    w_qi = w[:, offs[5]:offs[6]]
    w_g = w[:, offs[8]:offs[10]]

    tm_in = min(M, 2048)
    lxg, x16 = _project_and_cast(x2d, w_lru, BF16, min(M, 1024), 1024)
    lxg = lxg.reshape(B, S, 2 * d_rnn)
    q = _project(x16, w_q, BF16, tm_in, 1024,
                 out_scale=(HEAD_DIM ** -0.5) * LOG2E).reshape(B, S, q_w)
    kvw = _project(x16, w_kvw, F32, tm_in, w_kvw.shape[1]).reshape(B, S, w_kvw.shape[1])
    qi = _project(x16, w_qi, BF16, tm_in, 1024).reshape(B, S, qi_w)
    gates = _project(x16, w_g, BF16, tm_in, 1024)

    y_lru = _rg_lru(lxg, lru_conv_w[l], lru_conv_b[l], lru_gate_a_w[l], lru_gate_a_b[l],
                    lru_gate_x_w[l], lru_gate_x_b[l], lru_lambda[l], S, B, tc=min(S, 512), cbw=256)

    y_att = _sparse_attention(qi, kvw, q, rel_bias, idx_knorm_g[l], idx_knorm_b[l], top_k)

    merged = _gated_merge(y_lru.reshape(M, d_rnn), y_att.reshape(M, q_w),
                          w_proj_lru[l].astype(BF16), w_proj_attn[l].astype(BF16),
                          gates, tm=min(M, 1024), tn=512)
    h32, h16 = _outproj_ln(merged, x2d, w_out[l].astype(BF16), ln1_g[l], ln1_b[l], tm=512)
    act = _ffn_up(h16, ffn_w_up[l], ffn_conv_w[l], ffn_conv_b[l], S,
                  tm=min(S, 1024), tn=768)
    out = _ffn_down_ln(act, h32, ffn_w_down[l].astype(BF16), ln2_g[l], ln2_b[l],
                       tm=min(M, 1024), tk=1024)
    return out.reshape(B, S, D)
```

```python
import functools
import math

import jax
import jax.numpy as jnp
from jax import lax
from jax.experimental import pallas as pl
from jax.experimental.pallas import tpu as pltpu

N_HEADS = 16
HEAD_DIM = 128
IDX_HEADS = 16
IDX_DIM = 64
TOPK_MAX = 256
LRU_BLOCK = 128
LRU_C = 8.0
REL_BUCKETS = 32
REL_MAX_DIST = 128
LN_EPS = 1e-5
DEPTH = 1
ALPHA = (2.0 * DEPTH) ** 0.25

LANES = 128
SUBLANES = 8
VMEM_LIMIT_BYTES = 56 * 1024 * 1024

TQ = 256
CK = 256
HEAD_GROUP = 16
BIAS_TILE_STEP = 256
N_BIAS_TILES = 2
assert TQ % BIAS_TILE_STEP == 0 and CK % BIAS_TILE_STEP == 0
assert N_BIAS_TILES * BIAS_TILE_STEP - (CK - 1) >= REL_MAX_DIST

INT_MIN = -(2 ** 31)
MASK_LOGIT = -2e30
LOG2E = math.log2(math.e)
F32 = jnp.float32
BF16 = jnp.bfloat16


def _params(n_axes):
    return pltpu.CompilerParams(dimension_semantics=("arbitrary",) * n_axes,
                                vmem_limit_bytes=VMEM_LIMIT_BYTES)


def _matmul_kernel(x_ref, w_ref, o_ref, *, out_scale):
    acc = jnp.dot(x_ref[...], w_ref[...], preferred_element_type=F32)
    if out_scale != 1.0:
        acc = acc * out_scale
    o_ref[...] = acc.astype(o_ref.dtype)


def _project(x2d, w, out_dtype, tm, tn, out_scale=1.0):
    M, K = x2d.shape
    N = w.shape[1]
    return pl.pallas_call(
        functools.partial(_matmul_kernel, out_scale=out_scale),
        grid=(M // tm, N // tn),
        in_specs=[pl.BlockSpec((tm, K), lambda i, j: (i, 0)),
                  pl.BlockSpec((K, tn), lambda i, j: (0, j))],
        out_specs=pl.BlockSpec((tm, tn), lambda i, j: (i, j)),
        out_shape=jax.ShapeDtypeStruct((M, N), out_dtype),
        compiler_params=_params(2),
        name="in_proj",
    )(x2d, w)


def _matmul_cast_kernel(x_ref, w_ref, o_ref, x16_ref):
    @pl.when(pl.program_id(1) == 0)
    def _():
        x16_ref[...] = x_ref[...].astype(BF16)

    o_ref[...] = jnp.dot(x16_ref[...], w_ref[...], preferred_element_type=F32).astype(o_ref.dtype)


def _project_and_cast(x2d, w, out_dtype, tm, tn):
    M, K = x2d.shape
    N = w.shape[1]
    return pl.pallas_call(
        _matmul_cast_kernel,
        grid=(M // tm, N // tn),
        in_specs=[pl.BlockSpec((tm, K), lambda i, j: (i, 0)),
                  pl.BlockSpec((K, tn), lambda i, j: (0, j))],
        out_specs=[pl.BlockSpec((tm, tn), lambda i, j: (i, j)),
                   pl.BlockSpec((tm, K), lambda i, j: (i, 0))],
        out_shape=[jax.ShapeDtypeStruct((M, N), out_dtype),
                   jax.ShapeDtypeStruct((M, K), BF16)],
        compiler_params=_params(2),
        name="in_proj_cast",
    )(x2d, w)


def _gelu_tanh(x):
    c = math.sqrt(2.0 / math.pi)
    half = 0.5 * x
    return half + half * jnp.tanh(x * (c + (c * 0.044715) * (x * x)))


def _sigmoid(x):
    return 0.5 * jnp.tanh(0.5 * x) + 0.5


def _lru_kernel(lx_ref, lg_ref, cw_ref, cb_ref, wa_ref, ba_ref, wx_ref, bx_ref, lam_ref, o_ref,
                xs, a_s, u_s, h_s, *, tc, cb_width, conv_w):
    ti = pl.program_id(1)
    halo_rows = (conv_w - 1) * SUBLANES
    n_slab = cb_width // LANES
    rows = tc * SUBLANES
    slab = lambda v, s: v[:, s * LANES:(s + 1) * LANES]

    @pl.when(ti == 0)
    def _():
        xs[:, 0:halo_rows, :] = jnp.zeros((n_slab, halo_rows, LANES), F32)
        h_s[...] = jnp.zeros(h_s.shape, F32)

    for b in range(SUBLANES):
        for s in range(n_slab):
            xs[s, pl.ds(halo_rows + b, tc, stride=SUBLANES), :] = (
                lx_ref[b, :, s * LANES:(s + 1) * LANES].astype(F32))
    y_parts = []
    for s in range(n_slab):
        cs = slice(s * LANES, (s + 1) * LANES)
        y = cb_ref[:, cs] + cw_ref[0:1, cs] * xs[s, 0:rows, :]
        for j in range(1, conv_w):
            y = y + cw_ref[j:j + 1, cs] * xs[s, j * SUBLANES:j * SUBLANES + rows, :]
        y_parts.append(y)
        xs[s, 0:halo_rows, :] = xs[s, rows:rows + halo_rows, :]
    y2 = jnp.concatenate(y_parts, axis=1)
    yb = y2.astype(BF16)
    r_parts, i_parts = [], []
    for n in range(cb_width // LRU_BLOCK):
        blk = yb[:, n * LRU_BLOCK:(n + 1) * LRU_BLOCK]
        r_parts.append(jnp.dot(blk, wa_ref[n], preferred_element_type=F32))
        i_parts.append(jnp.dot(blk, wx_ref[n], preferred_element_type=F32))
    r = _sigmoid(jnp.concatenate(r_parts, axis=1) + ba_ref[...])
    g_in = _sigmoid(jnp.concatenate(i_parts, axis=1) + bx_ref[...])

    z = -lam_ref[...]
    softplus = jnp.maximum(z, 0.0) + jnp.log1p(jnp.exp(-jnp.abs(z)))
    rate = (-LRU_C) * softplus
    log_a = rate * r
    a = jnp.exp2((rate * LOG2E) * r)
    mult = jnp.sqrt(-jnp.tanh(log_a) * (a * a + 1.0))
    gx = g_in * y2
    u = mult * gx
    for s in range(n_slab):
        a_s[s] = slab(a, s)
        u_s[s] = slab(u, s)

    @pl.when(ti == 0)
    def _():
        for s in range(n_slab):
            u_s[s, 0:SUBLANES, :] = slab(gx, s)[0:SUBLANES]

    def step(t, h):
        r0 = pl.multiple_of(t * SUBLANES, SUBLANES)
        new = []
        for s in range(n_slab):
            hs = a_s[s, pl.ds(r0, SUBLANES), :] * h[s] + u_s[s, pl.ds(r0, SUBLANES), :]
            u_s[s, pl.ds(r0, SUBLANES), :] = hs
            new.append(hs)
        return tuple(new)

    h_fin = lax.fori_loop(0, tc, step, tuple(h_s[s] for s in range(n_slab)), unroll=8)
    for s in range(n_slab):
        h_s[s] = h_fin[s]
    for b in range(SUBLANES):
        for s in range(n_slab):
            cs = slice(s * LANES, (s + 1) * LANES)
            h_b = u_s[s, pl.ds(b, tc, stride=SUBLANES), :]
            o_ref[b, :, cs] = (_gelu_tanh(lg_ref[b, :, cs].astype(F32)) * h_b).astype(o_ref.dtype)


def _rg_lru(lxg, conv_w, conv_b, gate_a_w, gate_a_b, gate_x_w, gate_x_b, lam, S, B, tc, cbw):
    d_rnn = conv_w.shape[1]
    width = conv_w.shape[0]
    ncb = d_rnn // cbw
    nblk = cbw // LRU_BLOCK
    n_slab = cbw // LANES
    row = lambda v: v.reshape(1, d_rnn)
    kern = functools.partial(_lru_kernel, tc=tc, cb_width=cbw, conv_w=width)
    vec_spec = pl.BlockSpec((1, cbw), lambda n, t: (0, n))
    return pl.pallas_call(
        kern,
        grid=(ncb, S // tc),
        in_specs=[pl.BlockSpec((B, tc, cbw), lambda n, t: (0, t, n)),
                  pl.BlockSpec((B, tc, cbw), lambda n, t: (0, t, ncb + n)),
                  pl.BlockSpec((width, cbw), lambda n, t: (0, n)),
                  vec_spec,
                  pl.BlockSpec((nblk, LRU_BLOCK, LRU_BLOCK), lambda n, t: (n, 0, 0)),
                  vec_spec,
                  pl.BlockSpec((nblk, LRU_BLOCK, LRU_BLOCK), lambda n, t: (n, 0, 0)),
                  vec_spec,
                  vec_spec],
        out_specs=pl.BlockSpec((B, tc, cbw), lambda n, t: (0, t, n)),
        out_shape=jax.ShapeDtypeStruct((B, S, d_rnn), BF16),
        scratch_shapes=[pltpu.VMEM((n_slab, (tc + width - 1) * B, LANES), F32),
                        pltpu.VMEM((n_slab, tc * B, LANES), F32),
                        pltpu.VMEM((n_slab, tc * B, LANES), F32),
                        pltpu.VMEM((n_slab, B, LANES), F32)],
        compiler_params=_params(2),
        name="rg_lru",
    )(lxg, lxg, conv_w, row(conv_b), gate_a_w.astype(BF16), row(gate_a_b),
      gate_x_w.astype(BF16), row(gate_x_b), row(lam))


I16_MIN = -(2 ** 15)
I16_MAX = 2 ** 15 - 1


def _far_bucket():
    max_exact = REL_BUCKETS // 2
    large = max_exact + int(math.log(REL_MAX_DIST / max_exact) / math.log(REL_MAX_DIST / max_exact)
                            * (REL_BUCKETS - max_exact))
    return min(large, REL_BUCKETS - 1)


def _fill_bias_tiles(rb_ref, bias_s):
    rows = lax.broadcasted_iota(jnp.int32, (TQ, CK), 0)
    cols = lax.broadcasted_iota(jnp.int32, (TQ, CK), 1)
    max_exact = REL_BUCKETS // 2

    def tile_body(n, carry):
        d = n // N_HEADS
        h = n % N_HEADS
        rel = d * BIAS_TILE_STEP + rows - cols
        nf = jnp.maximum(rel, 1).astype(F32)
        large = max_exact + (jnp.log(nf / max_exact) / math.log(REL_MAX_DIST / max_exact)
                             * (REL_BUCKETS - max_exact)).astype(jnp.int32)
        large = jnp.minimum(large, REL_BUCKETS - 1)
        bucket = jnp.where(rel < max_exact, rel, large)
        acc = jnp.zeros((TQ, CK), F32)
        for b in range(REL_BUCKETS):
            acc = jnp.where(bucket == b, rb_ref[b, h], acc)
        bias_s[d, h] = (acc - rb_ref[_far_bucket(), h]) * LOG2E
        return carry

    lax.fori_loop(0, N_BIAS_TILES * N_HEADS, tile_body, 0)


def _ordered_key(score):
    bits = pltpu.bitcast(score, jnp.int32)
    return jnp.where(bits >= 0, bits, bits ^ jnp.int32(0x7FFFFFFF))


def _count_ge(arr, n_pairs, cand):
    cand16 = jnp.broadcast_to(cand, (TQ, LANES)).astype(jnp.int16)
    cand16 = jnp.concatenate([cand16] * (CK // LANES), axis=1)
    one, zero = jnp.int16(1), jnp.int16(0)

    def count_pair(p, cnt):
        cnt = cnt + jnp.where(arr[p, 0] >= cand16, one, zero)
        return cnt + jnp.where(arr[p, 1] >= cand16, one, zero)

    cnt = lax.fori_loop(0, n_pairs, count_pair, jnp.zeros((TQ, CK), jnp.int16))
    return jnp.sum(cnt.astype(jnp.int32).astype(F32), axis=1, keepdims=True)


def _kth_largest_i16(arr, n_pairs, need):
    def bit_step(step, thr):
        cand = thr + lax.shift_left(jnp.int32(1), 15 - step)
        return jnp.where(_count_ge(arr, n_pairs, cand) >= need, cand, thr)

    return lax.fori_loop(0, 16, bit_step, jnp.full((TQ, 1), I16_MIN, jnp.int32))


def _for_chunks(lo, hi, chunks_fn):
    n = hi - lo

    def pair(p, carry):
        chunks_fn(lo + 2 * p, 2)
        return carry

    lax.fori_loop(0, n // 2, pair, 0)

    @pl.when(n % 2 == 1)
    def _():
        chunks_fn(hi - 1, 1)


def _attn_kernel(rb_ref, qi_ref, kvw_ref, q_ref, kg_ref, kb_ref, o_ref,
                 bias_s, kin_t, k_t, v_aug, keys, hi_s, lo_s, qis, wbuf, qs, pbuf, m_s, accl_s,
                 *, top_k, seq):
    i = pl.program_id(1)
    k_off, v_off, kw_off = 0, HEAD_DIM, 2 * HEAD_DIM
    n_all = seq // CK
    lane = lax.broadcasted_iota(jnp.int32, (TQ, LANES), 1)

    @pl.when((pl.program_id(0) == 0) & (i == 0))
    def _():
        _fill_bias_tiles(rb_ref, bias_s)

    @pl.when(i == 0)
    def _():
        kraw = kvw_ref[:, kw_off:kw_off + IDX_DIM]
        mu = jnp.mean(kraw, axis=-1, keepdims=True)
        var = jnp.mean(jnp.square(kraw - mu), axis=-1, keepdims=True)
        kn = (kraw - mu) * lax.rsqrt(var + LN_EPS) * kg_ref[...] + kb_ref[...]
        kn = jnp.concatenate([kn, jnp.zeros((seq, LANES - IDX_DIM), F32)], axis=1)
        sub = lax.broadcasted_iota(jnp.int32, (HEAD_DIM, CK), 0)
        ones_rows = jnp.where(sub < 2, 1.0, 0.0).astype(BF16)
        for c in range(n_all):
            kin_t[c] = kn[c * CK:(c + 1) * CK, :].T[0:IDX_DIM].astype(BF16)
            k_t[c, 0:HEAD_DIM, :] = kvw_ref[c * CK:(c + 1) * CK, k_off:k_off + HEAD_DIM].T.astype(BF16)
            k_t[c, HEAD_DIM:2 * HEAD_DIM, :] = ones_rows
        v_aug[:, 0:HEAD_DIM] = kvw_ref[:, v_off:v_off + HEAD_DIM].astype(BF16)
        v_aug[:, HEAD_DIM:2 * HEAD_DIM] = jnp.ones((seq, HEAD_DIM), BF16)
        for h in range(N_HEADS):
            fb = jnp.full((TQ, LANES), rb_ref[_far_bucket(), h] * LOG2E, F32)
            fb_hi = fb.astype(BF16).astype(F32)
            qs[h * TQ:(h + 1) * TQ, HEAD_DIM:2 * HEAD_DIM] = jnp.where(
                lane == 0, fb_hi, jnp.where(lane == 1, fb - fb_hi, 0.0)).astype(BF16)

    q0 = pl.multiple_of(i * TQ, TQ)
    n_chunks = (q0 + TQ + CK - 1) // CK
    n_pairs = (n_chunks + 1) // 2
    rows = q0 + lax.broadcasted_iota(jnp.int32, (TQ, CK), 0)
    cols = lax.broadcasted_iota(jnp.int32, (TQ, CK), 1)

    qib = qi_ref[...]
    wi = kvw_ref[pl.ds(q0, TQ), kw_off + IDX_DIM:kw_off + IDX_DIM + IDX_HEADS]
    wi = wi * ((IDX_DIM ** -0.5) * (IDX_HEADS ** -0.5))
    for h in range(IDX_HEADS):
        qis[h * TQ:(h + 1) * TQ, :] = qib[:, h * IDX_DIM:(h + 1) * IDX_DIM]
        wbuf[h] = jnp.broadcast_to(wi[:, h:h + 1], (TQ, LANES))

    def score_chunk(c):
        kc = kin_t[c]
        acc = jnp.zeros((TQ, CK), F32)
        for g in range(IDX_HEADS // HEAD_GROUP):
            g0 = g * HEAD_GROUP * TQ
            d_g = jnp.dot(qis[g0:g0 + HEAD_GROUP * TQ, :], kc, preferred_element_type=F32)
            for hh in range(HEAD_GROUP):
                w_h = wbuf[g * HEAD_GROUP + hh]
                acc = acc + jnp.concatenate([w_h] * (CK // LANES), axis=1) * jnp.maximum(
                    d_g[hh * TQ:(hh + 1) * TQ], 0.0)
        key = jnp.where(c * CK + cols <= rows, _ordered_key(acc), jnp.int32(INT_MIN))
        keys[c] = key
        hi_s[c // 2, c % 2] = lax.shift_right_arithmetic(key, 16).astype(jnp.int16)
        lo_s[c // 2, c % 2] = ((key & 0xFFFF) + I16_MIN).astype(jnp.int16)

    def score_chunks(c, width):
        for k in range(width):
            score_chunk(c + k)

    _for_chunks(0, n_chunks, score_chunks)

    @pl.when(n_chunks % 2 == 1)
    def _():
        hi_s[n_chunks // 2, 1] = jnp.full((TQ, CK), I16_MIN, jnp.int16)
        lo_s[n_chunks // 2, 1] = jnp.full((TQ, CK), I16_MIN, jnp.int16)

    need = jnp.full((TQ, 1), float(top_k), F32)
    hi_k = _kth_largest_i16(hi_s, n_pairs, need)
    above = jnp.where(hi_k == I16_MAX, 0.0,
                      _count_ge(hi_s, n_pairs, jnp.minimum(hi_k + 1, I16_MAX)))
    hi_k16 = jnp.broadcast_to(hi_k, (TQ, LANES)).astype(jnp.int16)
    hi_k16 = jnp.concatenate([hi_k16] * (CK // LANES), axis=1)

    def keep_group(p, carry):
        for j in range(2):
            lo_s[p, j] = jnp.where(hi_s[p, j] == hi_k16, lo_s[p, j], jnp.int16(I16_MIN))
        return carry

    lax.fori_loop(0, n_pairs, keep_group, 0)
    lo_k = _kth_largest_i16(lo_s, n_pairs, need - above)
    thr = hi_k * 65536 + (lo_k - I16_MIN)

    def count_keys(pred):
        def body(c, cnt):
            return cnt + jnp.where(pred(c, keys[c]), 1.0, 0.0)

        cnt = lax.fori_loop(0, n_chunks, body, jnp.zeros((TQ, CK), F32))
        return jnp.sum(cnt, axis=1, keepdims=True)

    surplus = jnp.where(thr != INT_MIN, count_keys(lambda c, k: k >= thr) - need, 0.0)

    @pl.when(jnp.max(surplus) > 0.0)
    def _():
        keep_ties = need - count_keys(lambda c, k: k > thr)
        col_bits = max(1, (seq - 1).bit_length())

        def bit_step(step, t):
            cand = t + lax.shift_left(jnp.int32(1), col_bits - 1 - step)
            before = count_keys(lambda c, k: (k == thr) & (c * CK + cols < cand))
            return jnp.where(before < keep_ties, cand, t)

        last = lax.fori_loop(0, col_bits, bit_step, jnp.zeros((TQ, 1), jnp.int32))

        def demote(c, carry):
            k = keys[c]
            drop = (k == thr) & (c * CK + cols > last) & (surplus > 0.0)
            keys[c] = jnp.where(drop, k - 1, k)
            return carry

        lax.fori_loop(0, n_chunks, demote, 0)

    thr = jnp.maximum(thr, jnp.int32(INT_MIN + 1))

    for h in range(N_HEADS):
        qs[h * TQ:(h + 1) * TQ, 0:HEAD_DIM] = q_ref[:, h * HEAD_DIM:(h + 1) * HEAD_DIM]
    m_s[...] = jnp.full(m_s.shape, 0.5 * MASK_LOGIT, F32)
    accl_s[...] = jnp.zeros(accl_s.shape, F32)

    def attn_chunk(c, near):
        kc = k_t[c]
        vc = v_aug[pl.ds(pl.multiple_of(c * CK, CK), CK), :]
        sel = keys[c] >= thr
        tile = (q0 - c * CK) // BIAS_TILE_STEP
        for g in range(N_HEADS // HEAD_GROUP):
            g0 = g * HEAD_GROUP * TQ
            s_g = jnp.dot(qs[g0:g0 + HEAD_GROUP * TQ, :], kc, preferred_element_type=F32)
            alphas = []
            for hh in range(HEAD_GROUP):
                h = g * HEAD_GROUP + hh
                s = s_g[hh * TQ:(hh + 1) * TQ]
                if near:
                    s = s + bias_s[tile, h]
                s = jnp.where(sel, s, MASK_LOGIT)
                m_old = m_s[h]
                m_new = jnp.maximum(m_old, jnp.max(s, axis=1, keepdims=True))
                p = jnp.exp2(s - jnp.concatenate([m_new] * (CK // LANES), axis=1))
                alphas.append(jnp.exp2(m_old - m_new))
                m_s[h] = m_new
                pbuf[h * TQ:(h + 1) * TQ, :] = p.astype(BF16)
            pv = jnp.dot(pbuf[g0:g0 + HEAD_GROUP * TQ, :], vc, preferred_element_type=F32)
            for hh in range(HEAD_GROUP):
                h = g * HEAD_GROUP + hh
                accl_s[h] = (jnp.concatenate([alphas[hh], alphas[hh]], axis=1) * accl_s[h]
                             + pv[hh * TQ:(hh + 1) * TQ])

    def attn_chunks(c, width, *, near):
        for k in range(width):
            attn_chunk(c + k, near)

    n_far = jnp.maximum((q0 - N_BIAS_TILES * BIAS_TILE_STEP) // CK + 1, 0)
    _for_chunks(0, n_far, functools.partial(attn_chunks, near=False))
    _for_chunks(n_far, n_chunks, functools.partial(attn_chunks, near=True))
    for h in range(N_HEADS):
        al = accl_s[h]
        o_ref[:, h * HEAD_DIM:(h + 1) * HEAD_DIM] = (
            al[:, 0:HEAD_DIM] / al[:, HEAD_DIM:2 * HEAD_DIM]).astype(o_ref.dtype)


def _sparse_attention(qi, kvw, q, rel_bias, knorm_g, knorm_b, top_k):
    B, S, q_w = q.shape
    qi_w = qi.shape[2]
    kvw_w = kvw.shape[2]
    n_all = S // CK
    assert n_all % 2 == 0
    kern = functools.partial(_attn_kernel, top_k=top_k, seq=S)
    return pl.pallas_call(
        kern,
        grid=(B, S // TQ),
        in_specs=[pl.BlockSpec(memory_space=pltpu.SMEM),
                  pl.BlockSpec((None, TQ, qi_w), lambda b, i: (b, i, 0)),
                  pl.BlockSpec((None, S, kvw_w), lambda b, i: (b, 0, 0)),
                  pl.BlockSpec((None, TQ, q_w), lambda b, i: (b, i, 0)),
                  pl.BlockSpec((1, IDX_DIM), lambda b, i: (0, 0)),
                  pl.BlockSpec((1, IDX_DIM), lambda b, i: (0, 0))],
        out_specs=pl.BlockSpec((None, TQ, q_w), lambda b, i: (b, i, 0)),
        out_shape=jax.ShapeDtypeStruct((B, S, q_w), BF16),
        scratch_shapes=[pltpu.VMEM((N_BIAS_TILES, N_HEADS, TQ, CK), F32),
                        pltpu.VMEM((n_all, IDX_DIM, CK), BF16),
                        pltpu.VMEM((n_all, 2 * HEAD_DIM, CK), BF16),
                        pltpu.VMEM((S, 2 * HEAD_DIM), BF16),
                        pltpu.VMEM((n_all, TQ, CK), jnp.int32),
                        pltpu.VMEM((n_all // 2, 2, TQ, CK), jnp.int16),
                        pltpu.VMEM((n_all // 2, 2, TQ, CK), jnp.int16),
                        pltpu.VMEM((IDX_HEADS * TQ, IDX_DIM), BF16),
                        pltpu.VMEM((IDX_HEADS, TQ, LANES), F32),
                        pltpu.VMEM((N_HEADS * TQ, 2 * HEAD_DIM), BF16),
                        pltpu.VMEM((N_HEADS * TQ, CK), BF16),
                        pltpu.VMEM((N_HEADS, TQ, LANES), F32),
                        pltpu.VMEM((N_HEADS, TQ, 2 * HEAD_DIM), F32)],
        compiler_params=_params(2),
        name="dsa_attention",
    )(rel_bias, qi, kvw, q, knorm_g.reshape(1, IDX_DIM), knorm_b.reshape(1, IDX_DIM))


def _merge_kernel(yl_ref, ya_ref, wl_ref, wa_ref, gl_ref, ga_ref, o_ref):
    pl_ = jnp.dot(yl_ref[...], wl_ref[...], preferred_element_type=F32)
    pa_ = jnp.dot(ya_ref[...], wa_ref[...], preferred_element_type=F32)
    merged = (_sigmoid(gl_ref[...].astype(F32)) * pl_
              + _sigmoid(ga_ref[...].astype(F32)) * pa_)
    o_ref[...] = merged.astype(o_ref.dtype)


def _gated_merge(y_lru, y_att, w_l, w_a, gates, tm, tn):
    M, D = y_lru.shape
    N = w_l.shape[1]
    nj = N // tn
    return pl.pallas_call(
        _merge_kernel,
        grid=(M // tm, nj),
        in_specs=[pl.BlockSpec((tm, D), lambda i, j: (i, 0)),
                  pl.BlockSpec((tm, D), lambda i, j: (i, 0)),
                  pl.BlockSpec((D, tn), lambda i, j: (0, j)),
                  pl.BlockSpec((D, tn), lambda i, j: (0, j)),
                  pl.BlockSpec((tm, tn), lambda i, j: (i, j)),
                  pl.BlockSpec((tm, tn), lambda i, j: (i, nj + j))],
        out_specs=pl.BlockSpec((tm, tn), lambda i, j: (i, j)),
        out_shape=jax.ShapeDtypeStruct((M, N), BF16),
        compiler_params=_params(2),
        name="gated_merge",
    )(y_lru, y_att, w_l, w_a, gates, gates)


def _layer_norm_rows(v, g, b):
    mu = jnp.mean(v, axis=-1, keepdims=True)
    var = jnp.mean(jnp.square(v - mu), axis=-1, keepdims=True)
    return (v - mu) * lax.rsqrt(var + LN_EPS) * g + b


def _outproj_ln_kernel(m_ref, x_ref, w_ref, g_ref, b_ref, o32_ref, o16_ref):
    proj = jnp.dot(m_ref[...], w_ref[...], preferred_element_type=F32)
    h = _layer_norm_rows(ALPHA * x_ref[...] + proj, g_ref[...], b_ref[...])
    o32_ref[...] = h
    o16_ref[...] = h.astype(BF16)


def _outproj_ln(merged, x2d, w_out, g, b, tm):
    M, D = x2d.shape
    return pl.pallas_call(
        _outproj_ln_kernel,
        grid=(M // tm,),
        in_specs=[pl.BlockSpec((tm, D), lambda i: (i, 0)),
                  pl.BlockSpec((tm, D), lambda i: (i, 0)),
                  pl.BlockSpec((D, D), lambda i: (0, 0)),
                  pl.BlockSpec((1, D), lambda i: (0, 0)),
                  pl.BlockSpec((1, D), lambda i: (0, 0))],
        out_specs=[pl.BlockSpec((tm, D), lambda i: (i, 0)),
                   pl.BlockSpec((tm, D), lambda i: (i, 0))],
        out_shape=[jax.ShapeDtypeStruct((M, D), F32),
                   jax.ShapeDtypeStruct((M, D), BF16)],
        compiler_params=_params(1),
        name="outproj_ln1",
    )(merged, x2d, w_out, g.reshape(1, D), b.reshape(1, D))


def _ffn_up_kernel(halo_ref, h_ref, wg_ref, wv_ref, cwg_ref, cwv_ref, cbg_ref, cbv_ref, o_ref,
                   wg16, wv16, *, tm, conv_w, tiles_per_seq):
    i = pl.program_id(1)

    @pl.when(i == 0)
    def _():
        wg16[...] = wg_ref[...].astype(BF16)
        wv16[...] = wv_ref[...].astype(BF16)

    keep = jnp.where(i % tiles_per_seq > 0, 1.0, 0.0).astype(BF16)
    lhs = jnp.concatenate([halo_ref[...] * keep, h_ref[...]], axis=0)

    def conv_half(lhs_rows, n_rows, w_ref, cw_ref, cb_ref):
        up = jnp.dot(lhs_rows, w_ref[...], preferred_element_type=F32)
        y = cb_ref[...] + cw_ref[conv_w - 1:conv_w, :] * up[SUBLANES:SUBLANES + n_rows]
        for j in range(conv_w - 1):
            back = pltpu.roll(up, conv_w - 1 - j, axis=0)
            y = y + cw_ref[j:j + 1, :] * back[SUBLANES:SUBLANES + n_rows]
        return y

    yg = conv_half(lhs, tm, wg16, cwg_ref, cbg_ref)
    yv = conv_half(lhs, tm, wv16, cwv_ref, cbv_ref)
    o_ref[...] = (_gelu_tanh(yg) * yv).astype(o_ref.dtype)


def _ffn_up(h16, w_up, conv_w, conv_b, S, tm, tn):
    M, D = h16.shape
    d_ff = w_up.shape[1] // 2
    width = conv_w.shape[0]
    assert width - 1 <= SUBLANES and S % tm == 0
    nj = d_ff // tn
    kern = functools.partial(_ffn_up_kernel, tm=tm, conv_w=width, tiles_per_seq=S // tm)
    halo_blocks = tm // SUBLANES
    cb2 = conv_b.reshape(1, 2 * d_ff)
    return pl.pallas_call(
        kern,
        grid=(nj, M // tm),
        in_specs=[pl.BlockSpec((SUBLANES, D), lambda j, i: (jnp.maximum(i * halo_blocks - 1, 0), 0)),
                  pl.BlockSpec((tm, D), lambda j, i: (i, 0)),
                  pl.BlockSpec((D, tn), lambda j, i: (0, j)),
                  pl.BlockSpec((D, tn), lambda j, i: (0, nj + j)),
                  pl.BlockSpec((width, tn), lambda j, i: (0, j)),
                  pl.BlockSpec((width, tn), lambda j, i: (0, nj + j)),
                  pl.BlockSpec((1, tn), lambda j, i: (0, j)),
                  pl.BlockSpec((1, tn), lambda j, i: (0, nj + j))],
        out_specs=pl.BlockSpec((tm, tn), lambda j, i: (i, j)),
        out_shape=jax.ShapeDtypeStruct((M, d_ff), BF16),
        scratch_shapes=[pltpu.VMEM((D, tn), BF16), pltpu.VMEM((D, tn), BF16)],
        compiler_params=_params(2),
        name="ffn_up_conv_geglu",
    )(h16, h16, w_up, w_up, conv_w, conv_w, cb2, cb2)


def _ffn_down_kernel(a_ref, h_ref, w_ref, g_ref, b_ref, o_ref):
    k = pl.program_id(1)

    @pl.when(k == 0)
    def _():
        o_ref[...] = jnp.dot(a_ref[...], w_ref[...], preferred_element_type=F32)

    @pl.when(k > 0)
    def _():
        o_ref[...] += jnp.dot(a_ref[...], w_ref[...], preferred_element_type=F32)

    @pl.when(k == pl.num_programs(1) - 1)
    def _():
        o_ref[...] = _layer_norm_rows(ALPHA * h_ref[...] + o_ref[...], g_ref[...], b_ref[...])


def _ffn_down_ln(act, h32, w_down, g, b, tm, tk):
    d_ff, D = w_down.shape
    M = act.shape[0]
    return pl.pallas_call(
        _ffn_down_kernel,
        grid=(M // tm, d_ff // tk),
        in_specs=[pl.BlockSpec((tm, tk), lambda i, k: (i, k)),
                  pl.BlockSpec((tm, D), lambda i, k: (i, 0)),
                  pl.BlockSpec((tk, D), lambda i, k: (k, 0)),
                  pl.BlockSpec((1, D), lambda i, k: (0, 0)),
                  pl.BlockSpec((1, D), lambda i, k: (0, 0))],
        out_specs=pl.BlockSpec((tm, D), lambda i, k: (i, 0)),
        out_shape=jax.ShapeDtypeStruct((M, D), F32),
        compiler_params=_params(2),
        name="ffn_down_ln2",
    )(act, h32, w_down, g.reshape(1, D), b.reshape(1, D))


def kernel(x, w_in, lru_conv_w, lru_conv_b, lru_gate_a_w, lru_gate_a_b, lru_gate_x_w, lru_gate_x_b,
           lru_lambda, idx_knorm_g, idx_knorm_b, rel_bias, w_proj_lru, w_proj_attn, w_out,
           ln1_g, ln1_b, ffn_w_up, ffn_conv_w, ffn_conv_b, ffn_w_down, ln2_g, ln2_b):
    B, S, D = x.shape
    assert B == SUBLANES and S % CK == 0 and w_in.shape[0] == DEPTH
    top_k = min(TOPK_MAX, S // 4)
    d_rnn = lru_conv_w.shape[-1]
    q_w = N_HEADS * HEAD_DIM
    qi_w = IDX_HEADS * IDX_DIM
    splits = (d_rnn, d_rnn, q_w, HEAD_DIM, HEAD_DIM, qi_w, IDX_DIM, IDX_HEADS, D, D)
    offs = [0]
    for s_ in splits:
        offs.append(offs[-1] + s_)
    M = B * S

    l = 0
    w = w_in[l].astype(BF16)
    x2d = x.reshape(M, D)
    w_lru = w[:, offs[0]:offs[2]]
    w_q = w[:, offs[2]:offs[3]]
    pad = LANES - IDX_DIM - IDX_HEADS
    w_kvw = jnp.concatenate([w[:, offs[3]:offs[5]], w[:, offs[6]:offs[8]],
                             jnp.zeros((D, pad), BF16)], axis=1)
    w_qi = w[:, offs[5]:offs[6]]
    w_g = w[:, offs[8]:offs[10]]

    tm_in = min(M, 2048)
    lxg, x16 = _project_and_cast(x2d, w_lru, BF16, min(M, 1024), 1024)
    lxg = lxg.reshape(B, S, 2 * d_rnn)
    q = _project(x16, w_q, BF16, tm_in, 1024,
                 out_scale=(HEAD_DIM ** -0.5) * LOG2E).reshape(B, S, q_w)
    kvw = _project(x16, w_kvw, F32, tm_in, w_kvw.shape[1]).reshape(B, S, w_kvw.shape[1])
    qi = _project(x16, w_qi, BF16, tm_in, 1024).reshape(B, S, qi_w)
    gates = _project(x16, w_g, BF16, tm_in, 1024)

    y_lru = _rg_lru(lxg, lru_conv_w[l], lru_conv_b[l], lru_gate_a_w[l], lru_gate_a_b[l],
                    lru_gate_x_w[l], lru_gate_x_b[l], lru_lambda[l], S, B, tc=min(S, 512), cbw=256)

    y_att = _sparse_attention(qi, kvw, q, rel_bias, idx_knorm_g[l], idx_knorm_b[l], top_k)

    merged = _gated_merge(y_lru.reshape(M, d_rnn), y_att.reshape(M, q_w),
                          w_proj_lru[l].astype(BF16), w_proj_attn[l].astype(BF16),
                          gates, tm=min(M, 1024), tn=512)
    h32, h16 = _outproj_ln(merged, x2d, w_out[l].astype(BF16), ln1_g[l], ln1_b[l], tm=512)
    act = _ffn_up(h16, ffn_w_up[l], ffn_conv_w[l], ffn_conv_b[l], S,
                  tm=min(S, 1024), tn=768)
    out = _ffn_down_ln(act, h32, ffn_w_down[l].astype(BF16), ln2_g[l], ln2_b[l],
                       tm=min(M, 1024), tk=1024)
    return out.reshape(B, S, D)
```

```python
import functools
import math

import jax
import jax.numpy as jnp
from jax import lax
from jax.experimental import pallas as pl
from jax.experimental.pallas import tpu as pltpu

N_HEADS = 16
HEAD_DIM = 128
IDX_HEADS = 16
IDX_DIM = 64
TOPK_MAX = 256
LRU_BLOCK = 128
LRU_C = 8.0
REL_BUCKETS = 32
REL_MAX_DIST = 128
LN_EPS = 1e-5
DEPTH = 1
ALPHA = (2.0 * DEPTH) ** 0.25

LANES = 128
SUBLANES = 8
VMEM_LIMIT_BYTES = 56 * 1024 * 1024

TQ = 256
CK = 256
HEAD_GROUP = 16
BIAS_TILE_STEP = 256
N_BIAS_TILES = 2
assert TQ % BIAS_TILE_STEP == 0 and CK % BIAS_TILE_STEP == 0
assert N_BIAS_TILES * BIAS_TILE_STEP - (CK - 1) >= REL_MAX_DIST

INT_MIN = -(2 ** 31)
MASK_LOGIT = -2e30
LOG2E = math.log2(math.e)
F32 = jnp.float32
BF16 = jnp.bfloat16


def _params(n_axes):
    return pltpu.CompilerParams(dimension_semantics=("arbitrary",) * n_axes,
                                vmem_limit_bytes=VMEM_LIMIT_BYTES)


def _matmul_kernel(x_ref, w_ref, o_ref, *, out_scale):
    acc = jnp.dot(x_ref[...], w_ref[...], preferred_element_type=F32)
    if out_scale != 1.0:
        acc = acc * out_scale
    o_ref[...] = acc.astype(o_ref.dtype)


def _project(x2d, w, out_dtype, tm, tn, out_scale=1.0):
    M, K = x2d.shape
    N = w.shape[1]
    return pl.pallas_call(
        functools.partial(_matmul_kernel, out_scale=out_scale),
        grid=(M // tm, N // tn),
        in_specs=[pl.BlockSpec((tm, K), lambda i, j: (i, 0)),
                  pl.BlockSpec((K, tn), lambda i, j: (0, j))],
        out_specs=pl.BlockSpec((tm, tn), lambda i, j: (i, j)),
        out_shape=jax.ShapeDtypeStruct((M, N), out_dtype),
        compiler_params=_params(2),
        name="in_proj",
    )(x2d, w)


def _matmul_cast_kernel(x_ref, w_ref, o_ref, x16_ref):
    @pl.when(pl.program_id(1) == 0)
    def _():
        x16_ref[...] = x_ref[...].astype(BF16)

    o_ref[...] = jnp.dot(x16_ref[...], w_ref[...], preferred_element_type=F32).astype(o_ref.dtype)


def _project_and_cast(x2d, w, out_dtype, tm, tn):
    M, K = x2d.shape
    N = w.shape[1]
    return pl.pallas_call(
        _matmul_cast_kernel,
        grid=(M // tm, N // tn),
        in_specs=[pl.BlockSpec((tm, K), lambda i, j: (i, 0)),
                  pl.BlockSpec((K, tn), lambda i, j: (0, j))],
        out_specs=[pl.BlockSpec((tm, tn), lambda i, j: (i, j)),
                   pl.BlockSpec((tm, K), lambda i, j: (i, 0))],
        out_shape=[jax.ShapeDtypeStruct((M, N), out_dtype),
                   jax.ShapeDtypeStruct((M, K), BF16)],
        compiler_params=_params(2),
        name="in_proj_cast",
    )(x2d, w)


def _gelu_tanh(x):
    c = math.sqrt(2.0 / math.pi)
    half = 0.5 * x
    return half + half * jnp.tanh(x * (c + (c * 0.044715) * (x * x)))


def _sigmoid(x):
    return 0.5 * jnp.tanh(0.5 * x) + 0.5


def _lru_kernel(lx_ref, lg_ref, cw_ref, cb_ref, wa_ref, ba_ref, wx_ref, bx_ref, lam_ref, o_ref,
                xs, a_s, u_s, h_s, *, tc, cb_width, conv_w):
    ti = pl.program_id(1)
    halo_rows = (conv_w - 1) * SUBLANES
    n_slab = cb_width // LANES
    rows = tc * SUBLANES
    slab = lambda v, s: v[:, s * LANES:(s + 1) * LANES]

    @pl.when(ti == 0)
    def _():
        xs[:, 0:halo_rows, :] = jnp.zeros((n_slab, halo_rows, LANES), F32)
        h_s[...] = jnp.zeros(h_s.shape, F32)

    for b in range(SUBLANES):
        for s in range(n_slab):
            xs[s, pl.ds(halo_rows + b, tc, stride=SUBLANES), :] = (
                lx_ref[b, :, s * LANES:(s + 1) * LANES].astype(F32))
    y_parts = []
    for s in range(n_slab):
        cs = slice(s * LANES, (s + 1) * LANES)
        y = cb_ref[:, cs] + cw_ref[0:1, cs] * xs[s, 0:rows, :]
        for j in range(1, conv_w):
            y = y + cw_ref[j:j + 1, cs] * xs[s, j * SUBLANES:j * SUBLANES + rows, :]
        y_parts.append(y)
        xs[s, 0:halo_rows, :] = xs[s, rows:rows + halo_rows, :]
    y2 = jnp.concatenate(y_parts, axis=1)
    yb = y2.astype(BF16)
    r_parts, i_parts = [], []
    for n in range(cb_width // LRU_BLOCK):
        blk = yb[:, n * LRU_BLOCK:(n + 1) * LRU_BLOCK]
        r_parts.append(jnp.dot(blk, wa_ref[n], preferred_element_type=F32))
        i_parts.append(jnp.dot(blk, wx_ref[n], preferred_element_type=F32))
    r = _sigmoid(jnp.concatenate(r_parts, axis=1) + ba_ref[...])
    g_in = _sigmoid(jnp.concatenate(i_parts, axis=1) + bx_ref[...])

    z = -lam_ref[...]
    softplus = jnp.maximum(z, 0.0) + jnp.log1p(jnp.exp(-jnp.abs(z)))
    rate = (-LRU_C) * softplus
    log_a = rate * r
    a = jnp.exp2((rate * LOG2E) * r)
    mult = jnp.sqrt(-jnp.tanh(log_a) * (a * a + 1.0))
    gx = g_in * y2
    u = mult * gx
    for s in range(n_slab):
        a_s[s] = slab(a, s)
        u_s[s] = slab(u, s)

    @pl.when(ti == 0)
    def _():
        for s in range(n_slab):
            u_s[s, 0:SUBLANES, :] = slab(gx, s)[0:SUBLANES]

    def step(t, h):
        r0 = pl.multiple_of(t * SUBLANES, SUBLANES)
        new = []
        for s in range(n_slab):
            hs = a_s[s, pl.ds(r0, SUBLANES), :] * h[s] + u_s[s, pl.ds(r0, SUBLANES), :]
            u_s[s, pl.ds(r0, SUBLANES), :] = hs
            new.append(hs)
        return tuple(new)

    h_fin = lax.fori_loop(0, tc, step, tuple(h_s[s] for s in range(n_slab)), unroll=8)
    for s in range(n_slab):
        h_s[s] = h_fin[s]
    for b in range(SUBLANES):
        for s in range(n_slab):
            cs = slice(s * LANES, (s + 1) * LANES)
            h_b = u_s[s, pl.ds(b, tc, stride=SUBLANES), :]
            o_ref[b, :, cs] = (_gelu_tanh(lg_ref[b, :, cs].astype(F32)) * h_b).astype(o_ref.dtype)


def _rg_lru(lxg, conv_w, conv_b, gate_a_w, gate_a_b, gate_x_w, gate_x_b, lam, S, B, tc, cbw):
    d_rnn = conv_w.shape[1]
    width = conv_w.shape[0]
    ncb = d_rnn // cbw
    nblk = cbw // LRU_BLOCK
    n_slab = cbw // LANES
    row = lambda v: v.reshape(1, d_rnn)
    kern = functools.partial(_lru_kernel, tc=tc, cb_width=cbw, conv_w=width)
    vec_spec = pl.BlockSpec((1, cbw), lambda n, t: (0, n))
    return pl.pallas_call(
        kern,
        grid=(ncb, S // tc),
        in_specs=[pl.BlockSpec((B, tc, cbw), lambda n, t: (0, t, n)),
                  pl.BlockSpec((B, tc, cbw), lambda n, t: (0, t, ncb + n)),
                  pl.BlockSpec((width, cbw), lambda n, t: (0, n)),
                  vec_spec,
                  pl.BlockSpec((nblk, LRU_BLOCK, LRU_BLOCK), lambda n, t: (n, 0, 0)),
                  vec_spec,
                  pl.BlockSpec((nblk, LRU_BLOCK, LRU_BLOCK), lambda n, t: (n, 0, 0)),
                  vec_spec,
                  vec_spec],
        out_specs=pl.BlockSpec((B, tc, cbw), lambda n, t: (0, t, n)),
        out_shape=jax.ShapeDtypeStruct((B, S, d_rnn), BF16),
        scratch_shapes=[pltpu.VMEM((n_slab, (tc + width - 1) * B, LANES), F32),
                        pltpu.VMEM((n_slab, tc * B, LANES), F32),
                        pltpu.VMEM((n_slab, tc * B, LANES), F32),
                        pltpu.VMEM((n_slab, B, LANES), F32)],
        compiler_params=_params(2),
        name="rg_lru",
    )(lxg, lxg, conv_w, row(conv_b), gate_a_w.astype(BF16), row(gate_a_b),
      gate_x_w.astype(BF16), row(gate_x_b), row(lam))


I16_MIN = -(2 ** 15)
I16_MAX = 2 ** 15 - 1


def _far_bucket():
    max_exact = REL_BUCKETS // 2
    large = max_exact + int(math.log(REL_MAX_DIST / max_exact) / math.log(REL_MAX_DIST / max_exact)
                            * (REL_BUCKETS - max_exact))
    return min(large, REL_BUCKETS - 1)


def _fill_bias_tiles(rb_ref, bias_s):
    rows = lax.broadcasted_iota(jnp.int32, (TQ, CK), 0)
    cols = lax.broadcasted_iota(jnp.int32, (TQ, CK), 1)
    max_exact = REL_BUCKETS // 2

    def tile_body(n, carry):
        d = n // N_HEADS
        h = n % N_HEADS
        rel = d * BIAS_TILE_STEP + rows - cols
        nf = jnp.maximum(rel, 1).astype(F32)
        large = max_exact + (jnp.log(nf / max_exact) / math.log(REL_MAX_DIST / max_exact)
                             * (REL_BUCKETS - max_exact)).astype(jnp.int32)
        large = jnp.minimum(large, REL_BUCKETS - 1)
        bucket = jnp.where(rel < max_exact, rel, large)
        acc = jnp.zeros((TQ, CK), F32)
        for b in range(REL_BUCKETS):
            acc = jnp.where(bucket == b, rb_ref[b, h], acc)
        bias_s[d, h] = (acc - rb_ref[_far_bucket(), h]) * LOG2E
        return carry

    lax.fori_loop(0, N_BIAS_TILES * N_HEADS, tile_body, 0)


def _ordered_key(score):
    bits = pltpu.bitcast(score, jnp.int32)
    return jnp.where(bits >= 0, bits, bits ^ jnp.int32(0x7FFFFFFF))


def _count_ge(arr, n_pairs, cand):
    cand16 = jnp.broadcast_to(cand, (TQ, LANES)).astype(jnp.int16)
    cand16 = jnp.concatenate([cand16] * (CK // LANES), axis=1)
    one, zero = jnp.int16(1), jnp.int16(0)

    def count_pair(p, cnt):
        cnt = cnt + jnp.where(arr[p, 0] >= cand16, one, zero)
        return cnt + jnp.where(arr[p, 1] >= cand16, one, zero)

    cnt = lax.fori_loop(0, n_pairs, count_pair, jnp.zeros((TQ, CK), jnp.int16))
    return jnp.sum(cnt.astype(jnp.int32).astype(F32), axis=1, keepdims=True)


def _kth_largest_i16(arr, n_pairs, need):
    def bit_step(step, carry):
        thr, n_ge = carry
        cand = thr + lax.shift_left(jnp.int32(1), 15 - step)
        n_cand = _count_ge(arr, n_pairs, cand)
        take = n_cand >= need
        return jnp.where(take, cand, thr), jnp.where(take, n_cand, n_ge)

    everything = jnp.full((TQ, 1), 1.0, F32) * (n_pairs * (2 * CK)).astype(F32)
    return lax.fori_loop(0, 16, bit_step, (jnp.full((TQ, 1), I16_MIN, jnp.int32), everything))


def _for_chunks(lo, hi, chunks_fn):
    n = hi - lo

    def pair(p, carry):
        chunks_fn(lo + 2 * p, 2)
        return carry

    lax.fori_loop(0, n // 2, pair, 0)

    @pl.when(n % 2 == 1)
    def _():
        chunks_fn(hi - 1, 1)


def _attn_kernel(rb_ref, qi_ref, kvw_ref, q_ref, kg_ref, kb_ref, o_ref,
                 bias_s, kin_t, k_t, v_aug, keys, hi_s, lo_s, qis, wbuf, qs, pbuf, m_s, accl_s,
                 *, top_k, seq):
    i = pl.program_id(1)
    k_off, v_off, kw_off = 0, HEAD_DIM, 2 * HEAD_DIM
    n_all = seq // CK
    lane = lax.broadcasted_iota(jnp.int32, (TQ, LANES), 1)

    @pl.when((pl.program_id(0) == 0) & (i == 0))
    def _():
        _fill_bias_tiles(rb_ref, bias_s)

    @pl.when(i == 0)
    def _():
        kraw = kvw_ref[:, kw_off:kw_off + IDX_DIM]
        mu = jnp.mean(kraw, axis=-1, keepdims=True)
        var = jnp.mean(jnp.square(kraw - mu), axis=-1, keepdims=True)
        kn = (kraw - mu) * lax.rsqrt(var + LN_EPS) * kg_ref[...] + kb_ref[...]
        kn = jnp.concatenate([kn, jnp.zeros((seq, LANES - IDX_DIM), F32)], axis=1)
        sub = lax.broadcasted_iota(jnp.int32, (HEAD_DIM, CK), 0)
        ones_rows = jnp.where(sub < 2, 1.0, 0.0).astype(BF16)
        for c in range(n_all):
            kin_t[c] = kn[c * CK:(c + 1) * CK, :].T[0:IDX_DIM].astype(BF16)
            k_t[c, 0:HEAD_DIM, :] = kvw_ref[c * CK:(c + 1) * CK, k_off:k_off + HEAD_DIM].T.astype(BF16)
            k_t[c, HEAD_DIM:2 * HEAD_DIM, :] = ones_rows
        v_aug[:, 0:HEAD_DIM] = kvw_ref[:, v_off:v_off + HEAD_DIM].astype(BF16)
        v_aug[:, HEAD_DIM:2 * HEAD_DIM] = jnp.ones((seq, HEAD_DIM), BF16)
        for h in range(N_HEADS):
            fb = jnp.full((TQ, LANES), rb_ref[_far_bucket(), h] * LOG2E, F32)
            fb_hi = fb.astype(BF16).astype(F32)
            qs[h * TQ:(h + 1) * TQ, HEAD_DIM:2 * HEAD_DIM] = jnp.where(
                lane == 0, fb_hi, jnp.where(lane == 1, fb - fb_hi, 0.0)).astype(BF16)

    q0 = pl.multiple_of(i * TQ, TQ)
    n_chunks = (q0 + TQ + CK - 1) // CK
    n_pairs = (n_chunks + 1) // 2
    rows = q0 + lax.broadcasted_iota(jnp.int32, (TQ, CK), 0)
    cols = lax.broadcasted_iota(jnp.int32, (TQ, CK), 1)

    qib = qi_ref[...]
    wi = kvw_ref[pl.ds(q0, TQ), kw_off + IDX_DIM:kw_off + IDX_DIM + IDX_HEADS]
    wi = wi * ((IDX_DIM ** -0.5) * (IDX_HEADS ** -0.5))
    for h in range(IDX_HEADS):
        qis[h * TQ:(h + 1) * TQ, :] = qib[:, h * IDX_DIM:(h + 1) * IDX_DIM]
        wbuf[h] = jnp.broadcast_to(wi[:, h:h + 1], (TQ, LANES))

    def score_chunk(c):
        kc = kin_t[c]
        acc = jnp.zeros((TQ, CK), F32)
        for g in range(IDX_HEADS // HEAD_GROUP):
            g0 = g * HEAD_GROUP * TQ
            d_g = jnp.dot(qis[g0:g0 + HEAD_GROUP * TQ, :], kc, preferred_element_type=F32)
            for hh in range(HEAD_GROUP):
                w_h = wbuf[g * HEAD_GROUP + hh]
                acc = acc + jnp.concatenate([w_h] * (CK // LANES), axis=1) * jnp.maximum(
                    d_g[hh * TQ:(hh + 1) * TQ], 0.0)
        key = jnp.where(c * CK + cols <= rows, _ordered_key(acc), jnp.int32(INT_MIN))
        keys[c] = key
        hi_s[c // 2, c % 2] = lax.shift_right_arithmetic(key, 16).astype(jnp.int16)
        lo_s[c // 2, c % 2] = ((key & 0xFFFF) + I16_MIN).astype(jnp.int16)

    def score_chunks(c, width):
        for k in range(width):
            score_chunk(c + k)

    _for_chunks(0, n_chunks, score_chunks)

    @pl.when(n_chunks % 2 == 1)
    def _():
        hi_s[n_chunks // 2, 1] = jnp.full((TQ, CK), I16_MIN, jnp.int16)
        lo_s[n_chunks // 2, 1] = jnp.full((TQ, CK), I16_MIN, jnp.int16)

    need = jnp.full((TQ, 1), float(top_k), F32)
    hi_k, n_hi_ge = _kth_largest_i16(hi_s, n_pairs, need)
    above = jnp.where(hi_k == I16_MAX, 0.0,
                      _count_ge(hi_s, n_pairs, jnp.minimum(hi_k + 1, I16_MAX)))
    hi_k16 = jnp.broadcast_to(hi_k, (TQ, LANES)).astype(jnp.int16)
    hi_k16 = jnp.concatenate([hi_k16] * (CK // LANES), axis=1)

    def keep_group(p, carry):
        for j in range(2):
            lo_s[p, j] = jnp.where(hi_s[p, j] == hi_k16, lo_s[p, j], jnp.int16(I16_MIN))
        return carry

    lax.fori_loop(0, n_pairs, keep_group, 0)
    lo_k, n_lo_ge = _kth_largest_i16(lo_s, n_pairs, need - above)
    thr = hi_k * 65536 + (lo_k - I16_MIN)

    n_ge = above + jnp.where(lo_k == I16_MIN, n_hi_ge - above, n_lo_ge)
    surplus = jnp.where(thr != INT_MIN, n_ge - need, 0.0)

    def count_keys(pred):
        def body(c, cnt):
            return cnt + jnp.where(pred(c, keys[c]), 1.0, 0.0)

        cnt = lax.fori_loop(0, n_chunks, body, jnp.zeros((TQ, CK), F32))
        return jnp.sum(cnt, axis=1, keepdims=True)

    @pl.when(jnp.max(surplus) > 0.0)
    def _():
        keep_ties = need - count_keys(lambda c, k: k > thr)
        col_bits = max(1, (seq - 1).bit_length())

        def bit_step(step, t):
            cand = t + lax.shift_left(jnp.int32(1), col_bits - 1 - step)
            before = count_keys(lambda c, k: (k == thr) & (c * CK + cols < cand))
            return jnp.where(before < keep_ties, cand, t)

        last = lax.fori_loop(0, col_bits, bit_step, jnp.zeros((TQ, 1), jnp.int32))

        def demote(c, carry):
            k = keys[c]
            drop = (k == thr) & (c * CK + cols > last) & (surplus > 0.0)
            keys[c] = jnp.where(drop, k - 1, k)
            return carry

        lax.fori_loop(0, n_chunks, demote, 0)

    thr = jnp.maximum(thr, jnp.int32(INT_MIN + 1))

    for h in range(N_HEADS):
        qs[h * TQ:(h + 1) * TQ, 0:HEAD_DIM] = q_ref[:, h * HEAD_DIM:(h + 1) * HEAD_DIM]
    m_s[...] = jnp.full(m_s.shape, 0.5 * MASK_LOGIT, F32)
    accl_s[...] = jnp.zeros(accl_s.shape, F32)

    def attn_chunk(c, near):
        kc = k_t[c]
        vc = v_aug[pl.ds(pl.multiple_of(c * CK, CK), CK), :]
        sel = keys[c] >= thr
        tile = (q0 - c * CK) // BIAS_TILE_STEP
        for g in range(N_HEADS // HEAD_GROUP):
            g0 = g * HEAD_GROUP * TQ
            s_g = jnp.dot(qs[g0:g0 + HEAD_GROUP * TQ, :], kc, preferred_element_type=F32)
            alphas = []
            for hh in range(HEAD_GROUP):
                h = g * HEAD_GROUP + hh
                s = s_g[hh * TQ:(hh + 1) * TQ]
                if near:
                    s = s + bias_s[tile, h]
                s = jnp.where(sel, s, MASK_LOGIT)
                m_old = m_s[h]
                m_new = jnp.maximum(m_old, jnp.max(s, axis=1, keepdims=True))
                p = jnp.exp2(s - jnp.concatenate([m_new] * (CK // LANES), axis=1))
                alphas.append(jnp.exp2(m_old - m_new))
                m_s[h] = m_new
                pbuf[h * TQ:(h + 1) * TQ, :] = p.astype(BF16)
            pv = jnp.dot(pbuf[g0:g0 + HEAD_GROUP * TQ, :], vc, preferred_element_type=F32)
            for hh in range(HEAD_GROUP):
                h = g * HEAD_GROUP + hh
                accl_s[h] = (jnp.concatenate([alphas[hh], alphas[hh]], axis=1) * accl_s[h]
                             + pv[hh * TQ:(hh + 1) * TQ])

    def attn_chunks(c, width, *, near):
        for k in range(width):
            attn_chunk(c + k, near)

    n_far = jnp.maximum((q0 - N_BIAS_TILES * BIAS_TILE_STEP) // CK + 1, 0)
    _for_chunks(0, n_far, functools.partial(attn_chunks, near=False))
    _for_chunks(n_far, n_chunks, functools.partial(attn_chunks, near=True))
    for h in range(N_HEADS):
        al = accl_s[h]
        o_ref[:, h * HEAD_DIM:(h + 1) * HEAD_DIM] = (
            al[:, 0:HEAD_DIM] / al[:, HEAD_DIM:2 * HEAD_DIM]).astype(o_ref.dtype)


def _sparse_attention(qi, kvw, q, rel_bias, knorm_g, knorm_b, top_k):
    B, S, q_w = q.shape
    qi_w = qi.shape[2]
    kvw_w = kvw.shape[2]
    n_all = S // CK
    assert n_all % 2 == 0
    kern = functools.partial(_attn_kernel, top_k=top_k, seq=S)
    return pl.pallas_call(
        kern,
        grid=(B, S // TQ),
        in_specs=[pl.BlockSpec(memory_space=pltpu.SMEM),
                  pl.BlockSpec((None, TQ, qi_w), lambda b, i: (b, i, 0)),
                  pl.BlockSpec((None, S, kvw_w), lambda b, i: (b, 0, 0)),
                  pl.BlockSpec((None, TQ, q_w), lambda b, i: (b, i, 0)),
                  pl.BlockSpec((1, IDX_DIM), lambda b, i: (0, 0)),
                  pl.BlockSpec((1, IDX_DIM), lambda b, i: (0, 0))],
        out_specs=pl.BlockSpec((None, TQ, q_w), lambda b, i: (b, i, 0)),
        out_shape=jax.ShapeDtypeStruct((B, S, q_w), BF16),
        scratch_shapes=[pltpu.VMEM((N_BIAS_TILES, N_HEADS, TQ, CK), F32),
                        pltpu.VMEM((n_all, IDX_DIM, CK), BF16),
                        pltpu.VMEM((n_all, 2 * HEAD_DIM, CK), BF16),
                        pltpu.VMEM((S, 2 * HEAD_DIM), BF16),
                        pltpu.VMEM((n_all, TQ, CK), jnp.int32),
                        pltpu.VMEM((n_all // 2, 2, TQ, CK), jnp.int16),
                        pltpu.VMEM((n_all // 2, 2, TQ, CK), jnp.int16),
                        pltpu.VMEM((IDX_HEADS * TQ, IDX_DIM), BF16),
                        pltpu.VMEM((IDX_HEADS, TQ, LANES), F32),
                        pltpu.VMEM((N_HEADS * TQ, 2 * HEAD_DIM), BF16),
                        pltpu.VMEM((N_HEADS * TQ, CK), BF16),
                        pltpu.VMEM((N_HEADS, TQ, LANES), F32),
                        pltpu.VMEM((N_HEADS, TQ, 2 * HEAD_DIM), F32)],
        compiler_params=_params(2),
        name="dsa_attention",
    )(rel_bias, qi, kvw, q, knorm_g.reshape(1, IDX_DIM), knorm_b.reshape(1, IDX_DIM))


def _merge_kernel(yl_ref, ya_ref, wl_ref, wa_ref, gl_ref, ga_ref, o_ref):
    pl_ = jnp.dot(yl_ref[...], wl_ref[...], preferred_element_type=F32)
    pa_ = jnp.dot(ya_ref[...], wa_ref[...], preferred_element_type=F32)
    merged = (_sigmoid(gl_ref[...].astype(F32)) * pl_
              + _sigmoid(ga_ref[...].astype(F32)) * pa_)
    o_ref[...] = merged.astype(o_ref.dtype)


def _gated_merge(y_lru, y_att, w_l, w_a, gates, tm, tn):
    M, D = y_lru.shape
    N = w_l.shape[1]
    nj = N // tn
    return pl.pallas_call(
        _merge_kernel,
        grid=(M // tm, nj),
        in_specs=[pl.BlockSpec((tm, D), lambda i, j: (i, 0)),
                  pl.BlockSpec((tm, D), lambda i, j: (i, 0)),
                  pl.BlockSpec((D, tn), lambda i, j: (0, j)),
                  pl.BlockSpec((D, tn), lambda i, j: (0, j)),
                  pl.BlockSpec((tm, tn), lambda i, j: (i, j)),
                  pl.BlockSpec((tm, tn), lambda i, j: (i, nj + j))],
        out_specs=pl.BlockSpec((tm, tn), lambda i, j: (i, j)),
        out_shape=jax.ShapeDtypeStruct((M, N), BF16),
        compiler_params=_params(2),
        name="gated_merge",
    )(y_lru, y_att, w_l, w_a, gates, gates)


def _layer_norm_rows(v, g, b):
    mu = jnp.mean(v, axis=-1, keepdims=True)
    var = jnp.mean(jnp.square(v - mu), axis=-1, keepdims=True)
    return (v - mu) * lax.rsqrt(var + LN_EPS) * g + b


def _outproj_ln_kernel(m_ref, x_ref, w_ref, g_ref, b_ref, o32_ref, o16_ref):
    proj = jnp.dot(m_ref[...], w_ref[...], preferred_element_type=F32)
    h = _layer_norm_rows(ALPHA * x_ref[...] + proj, g_ref[...], b_ref[...])
    o32_ref[...] = h
    o16_ref[...] = h.astype(BF16)


def _outproj_ln(merged, x2d, w_out, g, b, tm):
    M, D = x2d.shape
    return pl.pallas_call(
        _outproj_ln_kernel,
        grid=(M // tm,),
        in_specs=[pl.BlockSpec((tm, D), lambda i: (i, 0)),
                  pl.BlockSpec((tm, D), lambda i: (i, 0)),
                  pl.BlockSpec((D, D), lambda i: (0, 0)),
                  pl.BlockSpec((1, D), lambda i: (0, 0)),
                  pl.BlockSpec((1, D), lambda i: (0, 0))],
        out_specs=[pl.BlockSpec((tm, D), lambda i: (i, 0)),
                   pl.BlockSpec((tm, D), lambda i: (i, 0))],
        out_shape=[jax.ShapeDtypeStruct((M, D), F32),
                   jax.ShapeDtypeStruct((M, D), BF16)],
        compiler_params=_params(1),
        name="outproj_ln1",
    )(merged, x2d, w_out, g.reshape(1, D), b.reshape(1, D))


def _ffn_up_kernel(halo_ref, h_ref, wg_ref, wv_ref, cwg_ref, cwv_ref, cbg_ref, cbv_ref, o_ref,
                   wg16, wv16, *, tm, conv_w, tiles_per_seq):
    i = pl.program_id(1)

    @pl.when(i == 0)
    def _():
        wg16[...] = wg_ref[...].astype(BF16)
        wv16[...] = wv_ref[...].astype(BF16)

    keep = jnp.where(i % tiles_per_seq > 0, 1.0, 0.0).astype(BF16)
    lhs = jnp.concatenate([halo_ref[...] * keep, h_ref[...]], axis=0)

    def conv_half(lhs_rows, n_rows, w_ref, cw_ref, cb_ref):
        up = jnp.dot(lhs_rows, w_ref[...], preferred_element_type=F32)
        y = cb_ref[...] + cw_ref[conv_w - 1:conv_w, :] * up[SUBLANES:SUBLANES + n_rows]
        for j in range(conv_w - 1):
            back = pltpu.roll(up, conv_w - 1 - j, axis=0)
            y = y + cw_ref[j:j + 1, :] * back[SUBLANES:SUBLANES + n_rows]
        return y

    yg = conv_half(lhs, tm, wg16, cwg_ref, cbg_ref)
    yv = conv_half(lhs, tm, wv16, cwv_ref, cbv_ref)
    o_ref[...] = (_gelu_tanh(yg) * yv).astype(o_ref.dtype)


def _ffn_up(h16, w_up, conv_w, conv_b, S, tm, tn):
    M, D = h16.shape
    d_ff = w_up.shape[1] // 2
    width = conv_w.shape[0]
    assert width - 1 <= SUBLANES and S % tm == 0
    nj = d_ff // tn
    kern = functools.partial(_ffn_up_kernel, tm=tm, conv_w=width, tiles_per_seq=S // tm)
    halo_blocks = tm // SUBLANES
    cb2 = conv_b.reshape(1, 2 * d_ff)
    return pl.pallas_call(
        kern,
        grid=(nj, M // tm),
        in_specs=[pl.BlockSpec((SUBLANES, D), lambda j, i: (jnp.maximum(i * halo_blocks - 1, 0), 0)),
                  pl.BlockSpec((tm, D), lambda j, i: (i, 0)),
                  pl.BlockSpec((D, tn), lambda j, i: (0, j)),
                  pl.BlockSpec((D, tn), lambda j, i: (0, nj + j)),
                  pl.BlockSpec((width, tn), lambda j, i: (0, j)),
                  pl.BlockSpec((width, tn), lambda j, i: (0, nj + j)),
                  pl.BlockSpec((1, tn), lambda j, i: (0, j)),
                  pl.BlockSpec((1, tn), lambda j, i: (0, nj + j))],
        out_specs=pl.BlockSpec((tm, tn), lambda j, i: (i, j)),
        out_shape=jax.ShapeDtypeStruct((M, d_ff), BF16),
        scratch_shapes=[pltpu.VMEM((D, tn), BF16), pltpu.VMEM((D, tn), BF16)],
        compiler_params=_params(2),
        name="ffn_up_conv_geglu",
    )(h16, h16, w_up, w_up, conv_w, conv_w, cb2, cb2)


def _ffn_down_kernel(a_ref, h_ref, w_ref, g_ref, b_ref, o_ref):
    k = pl.program_id(1)

    @pl.when(k == 0)
    def _():
        o_ref[...] = jnp.dot(a_ref[...], w_ref[...], preferred_element_type=F32)

    @pl.when(k > 0)
    def _():
        o_ref[...] += jnp.dot(a_ref[...], w_ref[...], preferred_element_type=F32)

    @pl.when(k == pl.num_programs(1) - 1)
    def _():
        o_ref[...] = _layer_norm_rows(ALPHA * h_ref[...] + o_ref[...], g_ref[...], b_ref[...])


def _ffn_down_ln(act, h32, w_down, g, b, tm, tk):
    d_ff, D = w_down.shape
    M = act.shape[0]
    return pl.pallas_call(
        _ffn_down_kernel,
        grid=(M // tm, d_ff // tk),
        in_specs=[pl.BlockSpec((tm, tk), lambda i, k: (i, k)),
                  pl.BlockSpec((tm, D), lambda i, k: (i, 0)),
                  pl.BlockSpec((tk, D), lambda i, k: (k, 0)),
                  pl.BlockSpec((1, D), lambda i, k: (0, 0)),
                  pl.BlockSpec((1, D), lambda i, k: (0, 0))],
        out_specs=pl.BlockSpec((tm, D), lambda i, k: (i, 0)),
        out_shape=jax.ShapeDtypeStruct((M, D), F32),
        compiler_params=_params(2),
        name="ffn_down_ln2",
    )(act, h32, w_down, g.reshape(1, D), b.reshape(1, D))


def kernel(x, w_in, lru_conv_w, lru_conv_b, lru_gate_a_w, lru_gate_a_b, lru_gate_x_w, lru_gate_x_b,
           lru_lambda, idx_knorm_g, idx_knorm_b, rel_bias, w_proj_lru, w_proj_attn, w_out,
           ln1_g, ln1_b, ffn_w_up, ffn_conv_w, ffn_conv_b, ffn_w_down, ln2_g, ln2_b):
    B, S, D = x.shape
    assert B == SUBLANES and S % CK == 0 and w_in.shape[0] == DEPTH
    top_k = min(TOPK_MAX, S // 4)
    d_rnn = lru_conv_w.shape[-1]
    q_w = N_HEADS * HEAD_DIM
    qi_w = IDX_HEADS * IDX_DIM
    splits = (d_rnn, d_rnn, q_w, HEAD_DIM, HEAD_DIM, qi_w, IDX_DIM, IDX_HEADS, D, D)
    offs = [0]
    for s_ in splits:
        offs.append(offs[-1] + s_)
    M = B * S

    l = 0
    w = w_in[l].astype(BF16)
    x2d = x.reshape(M, D)
    w_lru = w[:, offs[0]:offs[2]]
    w_q = w[:, offs[2]:offs[3]]
    pad = LANES - IDX_DIM - IDX_HEADS
    w_kvw = jnp.concatenate([w[:, offs[3]:offs[5]], w[:, offs[6]:offs[8]],
                             jnp.zeros((D, pad), BF16)], axis=1)
    w_qi = w[:, offs[5]:offs[6]]
    w_g = w[:, offs[8]:offs[10]]

    tm_in = min(M, 2048)
    lxg, x16 = _project_and_cast(x2d, w_lru, BF16, min(M, 1024), 1024)
    lxg = lxg.reshape(B, S, 2 * d_rnn)
    q = _project(x16, w_q, BF16, tm_in, 1024,
                 out_scale=(HEAD_DIM ** -0.5) * LOG2E).reshape(B, S, q_w)
    kvw = _project(x16, w_kvw, F32, tm_in, w_kvw.shape[1]).reshape(B, S, w_kvw.shape[1])
    qi = _project(x16, w_qi, BF16, tm_in, 1024).reshape(B, S, qi_w)
    gates = _project(x16, w_g, BF16, tm_in, 1024)

    y_lru = _rg_lru(lxg, lru_conv_w[l], lru_conv_b[l], lru_gate_a_w[l], lru_gate_a_b[l],
                    lru_gate_x_w[l], lru_gate_x_b[l], lru_lambda[l], S, B, tc=min(S, 512), cbw=256)

    y_att = _sparse_attention(qi, kvw, q, rel_bias, idx_knorm_g[l], idx_knorm_b[l], top_k)

    merged = _gated_merge(y_lru.reshape(M, d_rnn), y_att.reshape(M, q_w),
                          w_proj_lru[l].astype(BF16), w_proj_attn[l].astype(BF16),
                          gates, tm=min(M, 1024), tn=512)
    h32, h16 = _outproj_ln(merged, x2d, w_out[l].astype(BF16), ln1_g[l], ln1_b[l], tm=512)
    act = _ffn_up(h16, ffn_w_up[l], ffn_conv_w[l], ffn_conv_b[l], S,
                  tm=min(S, 1024), tn=768)
    out = _ffn_down_ln(act, h32, ffn_w_down[l].astype(BF16), ln2_g[l], ln2_b[l],
                       tm=min(M, 1024), tk=1024)
    return out.reshape(B, S, D)
```

```python
import functools
import math

import jax
import jax.numpy as jnp
from jax import lax
from jax.experimental import pallas as pl
from jax.experimental.pallas import tpu as pltpu

N_HEADS = 16
HEAD_DIM = 128
IDX_HEADS = 16
IDX_DIM = 64
TOPK_MAX = 256
LRU_BLOCK = 128
LRU_C = 8.0
REL_BUCKETS = 32
REL_MAX_DIST = 128
LN_EPS = 1e-5
DEPTH = 1
ALPHA = (2.0 * DEPTH) ** 0.25

LANES = 128
SUBLANES = 8
VMEM_LIMIT_BYTES = 56 * 1024 * 1024

TQ = 256
CK = 256
HEAD_GROUP = 16
ATTN_HEAD_GROUP = 16
BIAS_TILE_STEP = 256
N_BIAS_TILES = 2
assert TQ % BIAS_TILE_STEP == 0 and CK % BIAS_TILE_STEP == 0
assert N_BIAS_TILES * BIAS_TILE_STEP - (CK - 1) >= REL_MAX_DIST

INT_MIN = -(2 ** 31)
MASK_LOGIT = -2e30
LOG2E = math.log2(math.e)
F32 = jnp.float32
BF16 = jnp.bfloat16


def _params(n_axes):
    return pltpu.CompilerParams(dimension_semantics=("arbitrary",) * n_axes,
                                vmem_limit_bytes=VMEM_LIMIT_BYTES)


def _matmul_kernel(x_ref, w_ref, o_ref, *, out_scale):
    acc = jnp.dot(x_ref[...], w_ref[...], preferred_element_type=F32)
    if out_scale != 1.0:
        acc = acc * out_scale
    o_ref[...] = acc.astype(o_ref.dtype)


def _project(x2d, w, out_dtype, tm, tn, out_scale=1.0):
    M, K = x2d.shape
    N = w.shape[1]
    return pl.pallas_call(
        functools.partial(_matmul_kernel, out_scale=out_scale),
        grid=(M // tm, N // tn),
        in_specs=[pl.BlockSpec((tm, K), lambda i, j: (i, 0)),
                  pl.BlockSpec((K, tn), lambda i, j: (0, j))],
        out_specs=pl.BlockSpec((tm, tn), lambda i, j: (i, j)),
        out_shape=jax.ShapeDtypeStruct((M, N), out_dtype),
        compiler_params=_params(2),
        name="in_proj",
    )(x2d, w)


def _matmul_cast_kernel(x_ref, w_ref, o_ref, x16_ref):
    @pl.when(pl.program_id(1) == 0)
    def _():
        x16_ref[...] = x_ref[...].astype(BF16)

    o_ref[...] = jnp.dot(x16_ref[...], w_ref[...], preferred_element_type=F32).astype(o_ref.dtype)


def _project_and_cast(x2d, w, out_dtype, tm, tn):
    M, K = x2d.shape
    N = w.shape[1]
    return pl.pallas_call(
        _matmul_cast_kernel,
        grid=(M // tm, N // tn),
        in_specs=[pl.BlockSpec((tm, K), lambda i, j: (i, 0)),
                  pl.BlockSpec((K, tn), lambda i, j: (0, j))],
        out_specs=[pl.BlockSpec((tm, tn), lambda i, j: (i, j)),
                   pl.BlockSpec((tm, K), lambda i, j: (i, 0))],
        out_shape=[jax.ShapeDtypeStruct((M, N), out_dtype),
                   jax.ShapeDtypeStruct((M, K), BF16)],
        compiler_params=_params(2),
        name="in_proj_cast",
    )(x2d, w)


def _gelu_tanh(x):
    c = math.sqrt(2.0 / math.pi)
    half = 0.5 * x
    return half + half * jnp.tanh(x * (c + (c * 0.044715) * (x * x)))


def _sigmoid(x):
    return 0.5 * jnp.tanh(0.5 * x) + 0.5


def _lru_kernel(lx_ref, lg_ref, cw_ref, cb_ref, wa_ref, ba_ref, wx_ref, bx_ref, lam_ref, o_ref,
                xs, a_s, u_s, h_s, *, tc, cb_width, conv_w):
    ti = pl.program_id(1)
    halo_rows = (conv_w - 1) * SUBLANES
    n_slab = cb_width // LANES
    rows = tc * SUBLANES
    slab = lambda v, s: v[:, s * LANES:(s + 1) * LANES]

    @pl.when(ti == 0)
    def _():
        xs[:, 0:halo_rows, :] = jnp.zeros((n_slab, halo_rows, LANES), F32)
        h_s[...] = jnp.zeros(h_s.shape, F32)

    for b in range(SUBLANES):
        for s in range(n_slab):
            xs[s, pl.ds(halo_rows + b, tc, stride=SUBLANES), :] = (
                lx_ref[b, :, s * LANES:(s + 1) * LANES].astype(F32))
    y_parts = []
    for s in range(n_slab):
        cs = slice(s * LANES, (s + 1) * LANES)
        y = cb_ref[:, cs] + cw_ref[0:1, cs] * xs[s, 0:rows, :]
        for j in range(1, conv_w):
            y = y + cw_ref[j:j + 1, cs] * xs[s, j * SUBLANES:j * SUBLANES + rows, :]
        y_parts.append(y)
        xs[s, 0:halo_rows, :] = xs[s, rows:rows + halo_rows, :]
    y2 = jnp.concatenate(y_parts, axis=1)
    yb = y2.astype(BF16)
    r_parts, i_parts = [], []
    for n in range(cb_width // LRU_BLOCK):
        blk = yb[:, n * LRU_BLOCK:(n + 1) * LRU_BLOCK]
        r_parts.append(jnp.dot(blk, wa_ref[n], preferred_element_type=F32))
        i_parts.append(jnp.dot(blk, wx_ref[n], preferred_element_type=F32))
    r = _sigmoid(jnp.concatenate(r_parts, axis=1) + ba_ref[...])
    g_in = _sigmoid(jnp.concatenate(i_parts, axis=1) + bx_ref[...])

    z = -lam_ref[...]
    softplus = jnp.maximum(z, 0.0) + jnp.log1p(jnp.exp(-jnp.abs(z)))
    rate = (-LRU_C) * softplus
    log_a = rate * r
    a = jnp.exp2((rate * LOG2E) * r)
    mult = jnp.sqrt(-jnp.tanh(log_a) * (a * a + 1.0))
    gx = g_in * y2
    u = mult * gx
    for s in range(n_slab):
        a_s[s] = slab(a, s)
        u_s[s] = slab(u, s)

    @pl.when(ti == 0)
    def _():
        for s in range(n_slab):
            u_s[s, 0:SUBLANES, :] = slab(gx, s)[0:SUBLANES]

    def step(t, h):
        r0 = pl.multiple_of(t * SUBLANES, SUBLANES)
        new = []
        for s in range(n_slab):
            hs = a_s[s, pl.ds(r0, SUBLANES), :] * h[s] + u_s[s, pl.ds(r0, SUBLANES), :]
            u_s[s, pl.ds(r0, SUBLANES), :] = hs
            new.append(hs)
        return tuple(new)

    h_fin = lax.fori_loop(0, tc, step, tuple(h_s[s] for s in range(n_slab)), unroll=8)
    for s in range(n_slab):
        h_s[s] = h_fin[s]
    for b in range(SUBLANES):
        for s in range(n_slab):
            cs = slice(s * LANES, (s + 1) * LANES)
            h_b = u_s[s, pl.ds(b, tc, stride=SUBLANES), :]
            o_ref[b, :, cs] = (_gelu_tanh(lg_ref[b, :, cs].astype(F32)) * h_b).astype(o_ref.dtype)


def _rg_lru(lxg, conv_w, conv_b, gate_a_w, gate_a_b, gate_x_w, gate_x_b, lam, S, B, tc, cbw):
    d_rnn = conv_w.shape[1]
    width = conv_w.shape[0]
    ncb = d_rnn // cbw
    nblk = cbw // LRU_BLOCK
    n_slab = cbw // LANES
    row = lambda v: v.reshape(1, d_rnn)
    kern = functools.partial(_lru_kernel, tc=tc, cb_width=cbw, conv_w=width)
    vec_spec = pl.BlockSpec((1, cbw), lambda n, t: (0, n))
    return pl.pallas_call(
        kern,
        grid=(ncb, S // tc),
        in_specs=[pl.BlockSpec((B, tc, cbw), lambda n, t: (0, t, n)),
                  pl.BlockSpec((B, tc, cbw), lambda n, t: (0, t, ncb + n)),
                  pl.BlockSpec((width, cbw), lambda n, t: (0, n)),
                  vec_spec,
                  pl.BlockSpec((nblk, LRU_BLOCK, LRU_BLOCK), lambda n, t: (n, 0, 0)),
                  vec_spec,
                  pl.BlockSpec((nblk, LRU_BLOCK, LRU_BLOCK), lambda n, t: (n, 0, 0)),
                  vec_spec,
                  vec_spec],
        out_specs=pl.BlockSpec((B, tc, cbw), lambda n, t: (0, t, n)),
        out_shape=jax.ShapeDtypeStruct((B, S, d_rnn), BF16),
        scratch_shapes=[pltpu.VMEM((n_slab, (tc + width - 1) * B, LANES), F32),
                        pltpu.VMEM((n_slab, tc * B, LANES), F32),
                        pltpu.VMEM((n_slab, tc * B, LANES), F32),
                        pltpu.VMEM((n_slab, B, LANES), F32)],
        compiler_params=_params(2),
        name="rg_lru",
    )(lxg, lxg, conv_w, row(conv_b), gate_a_w.astype(BF16), row(gate_a_b),
      gate_x_w.astype(BF16), row(gate_x_b), row(lam))


I16_MIN = -(2 ** 15)
I16_MAX = 2 ** 15 - 1


def _far_bucket():
    max_exact = REL_BUCKETS // 2
    large = max_exact + int(math.log(REL_MAX_DIST / max_exact) / math.log(REL_MAX_DIST / max_exact)
                            * (REL_BUCKETS - max_exact))
    return min(large, REL_BUCKETS - 1)


def _fill_bias_tiles(rb_ref, bias_s):
    rows = lax.broadcasted_iota(jnp.int32, (TQ, CK), 0)
    cols = lax.broadcasted_iota(jnp.int32, (TQ, CK), 1)
    max_exact = REL_BUCKETS // 2

    def tile_body(n, carry):
        d = n // N_HEADS
        h = n % N_HEADS
        rel = d * BIAS_TILE_STEP + rows - cols
        nf = jnp.maximum(rel, 1).astype(F32)
        large = max_exact + (jnp.log(nf / max_exact) / math.log(REL_MAX_DIST / max_exact)
                             * (REL_BUCKETS - max_exact)).astype(jnp.int32)
        large = jnp.minimum(large, REL_BUCKETS - 1)
        bucket = jnp.where(rel < max_exact, rel, large)
        acc = jnp.zeros((TQ, CK), F32)
        for b in range(REL_BUCKETS):
            acc = jnp.where(bucket == b, rb_ref[b, h], acc)
        bias_s[d, h] = (acc - rb_ref[_far_bucket(), h]) * LOG2E
        return carry

    lax.fori_loop(0, N_BIAS_TILES * N_HEADS, tile_body, 0)


def _ordered_key(score):
    bits = pltpu.bitcast(score, jnp.int32)
    return jnp.where(bits >= 0, bits, bits ^ jnp.int32(0x7FFFFFFF))


def _count_ge(arr, n_pairs, cand):
    cand16 = jnp.broadcast_to(cand, (TQ, LANES)).astype(jnp.int16)
    cand16 = jnp.concatenate([cand16] * (CK // LANES), axis=1)
    one, zero = jnp.int16(1), jnp.int16(0)

    def count_pair(p, cnt):
        cnt = cnt + jnp.where(arr[p, 0] >= cand16, one, zero)
        return cnt + jnp.where(arr[p, 1] >= cand16, one, zero)

    cnt = lax.fori_loop(0, n_pairs, count_pair, jnp.zeros((TQ, CK), jnp.int16))
    return jnp.sum(cnt.astype(jnp.int32).astype(F32), axis=1, keepdims=True)


def _kth_largest_i16(arr, n_pairs, need):
    def bit_step(step, thr):
        cand = thr + lax.shift_left(jnp.int32(1), 15 - step)
        return jnp.where(_count_ge(arr, n_pairs, cand) >= need, cand, thr)

    return lax.fori_loop(0, 16, bit_step, jnp.full((TQ, 1), I16_MIN, jnp.int32))


def _for_chunks(lo, hi, chunks_fn):
    n = hi - lo

    def pair(p, carry):
        chunks_fn(lo + 2 * p, 2)
        return carry

    lax.fori_loop(0, n // 2, pair, 0)

    @pl.when(n % 2 == 1)
    def _():
        chunks_fn(hi - 1, 1)


def _attn_kernel(rb_ref, qi_ref, kvw_ref, q_ref, kg_ref, kb_ref, o_ref,
                 bias_s, kin_t, k_t, v_aug, keys, hi_s, lo_s, qis, wbuf, qs, pbuf, m_s, accl_s,
                 *, top_k, seq):
    i = pl.program_id(1)
    k_off, v_off, kw_off = 0, HEAD_DIM, 2 * HEAD_DIM
    n_all = seq // CK
    lane = lax.broadcasted_iota(jnp.int32, (TQ, LANES), 1)

    @pl.when((pl.program_id(0) == 0) & (i == 0))
    def _():
        _fill_bias_tiles(rb_ref, bias_s)

    @pl.when(i == 0)
    def _():
        kraw = kvw_ref[:, kw_off:kw_off + IDX_DIM]
        mu = jnp.mean(kraw, axis=-1, keepdims=True)
        var = jnp.mean(jnp.square(kraw - mu), axis=-1, keepdims=True)
        kn = (kraw - mu) * lax.rsqrt(var + LN_EPS) * kg_ref[...] + kb_ref[...]
        kn = jnp.concatenate([kn, jnp.zeros((seq, LANES - IDX_DIM), F32)], axis=1)
        sub = lax.broadcasted_iota(jnp.int32, (HEAD_DIM, CK), 0)
        ones_rows = jnp.where(sub < 2, 1.0, 0.0).astype(BF16)
        for c in range(n_all):
            kin_t[c] = kn[c * CK:(c + 1) * CK, :].T[0:IDX_DIM].astype(BF16)
            k_t[c, 0:HEAD_DIM, :] = kvw_ref[c * CK:(c + 1) * CK, k_off:k_off + HEAD_DIM].T.astype(BF16)
            k_t[c, HEAD_DIM:2 * HEAD_DIM, :] = ones_rows
        v_aug[:, 0:HEAD_DIM] = kvw_ref[:, v_off:v_off + HEAD_DIM].astype(BF16)
        v_aug[:, HEAD_DIM:2 * HEAD_DIM] = jnp.ones((seq, HEAD_DIM), BF16)
        for h in range(N_HEADS):
            fb = jnp.full((TQ, LANES), rb_ref[_far_bucket(), h] * LOG2E, F32)
            fb_hi = fb.astype(BF16).astype(F32)
            qs[h * TQ:(h + 1) * TQ, HEAD_DIM:2 * HEAD_DIM] = jnp.where(
                lane == 0, fb_hi, jnp.where(lane == 1, fb - fb_hi, 0.0)).astype(BF16)

    q0 = pl.multiple_of(i * TQ, TQ)
    n_chunks = (q0 + TQ + CK - 1) // CK
    n_pairs = (n_chunks + 1) // 2
    rows = q0 + lax.broadcasted_iota(jnp.int32, (TQ, CK), 0)
    cols = lax.broadcasted_iota(jnp.int32, (TQ, CK), 1)

    qib = qi_ref[...]
    wi = kvw_ref[pl.ds(q0, TQ), kw_off + IDX_DIM:kw_off + IDX_DIM + IDX_HEADS]
    wi = wi * ((IDX_DIM ** -0.5) * (IDX_HEADS ** -0.5))
    for h in range(IDX_HEADS):
        qis[h * TQ:(h + 1) * TQ, :] = qib[:, h * IDX_DIM:(h + 1) * IDX_DIM]
        wbuf[h] = jnp.broadcast_to(wi[:, h:h + 1], (TQ, LANES))

    def score_chunk(c):
        kc = kin_t[c]
        acc = jnp.zeros((TQ, CK), F32)
        for g in range(IDX_HEADS // HEAD_GROUP):
            g0 = g * HEAD_GROUP * TQ
            d_g = jnp.dot(qis[g0:g0 + HEAD_GROUP * TQ, :], kc, preferred_element_type=F32)
            for hh in range(HEAD_GROUP):
                w_h = wbuf[g * HEAD_GROUP + hh]
                acc = acc + jnp.concatenate([w_h] * (CK // LANES), axis=1) * jnp.maximum(
                    d_g[hh * TQ:(hh + 1) * TQ], 0.0)
        key = jnp.where(c * CK + cols <= rows, _ordered_key(acc), jnp.int32(INT_MIN))
        keys[c] = key
        hi_s[c // 2, c % 2] = lax.shift_right_arithmetic(key, 16).astype(jnp.int16)
        lo_s[c // 2, c % 2] = ((key & 0xFFFF) + I16_MIN).astype(jnp.int16)

    def score_chunks(c, width):
        for k in range(width):
            score_chunk(c + k)

    _for_chunks(0, n_chunks, score_chunks)

    @pl.when(n_chunks % 2 == 1)
    def _():
        hi_s[n_chunks // 2, 1] = jnp.full((TQ, CK), I16_MIN, jnp.int16)
        lo_s[n_chunks // 2, 1] = jnp.full((TQ, CK), I16_MIN, jnp.int16)

    need = jnp.full((TQ, 1), float(top_k), F32)
    hi_k = _kth_largest_i16(hi_s, n_pairs, need)
    above = jnp.where(hi_k == I16_MAX, 0.0,
                      _count_ge(hi_s, n_pairs, jnp.minimum(hi_k + 1, I16_MAX)))
    hi_k16 = jnp.broadcast_to(hi_k, (TQ, LANES)).astype(jnp.int16)
    hi_k16 = jnp.concatenate([hi_k16] * (CK // LANES), axis=1)

    def keep_group(p, carry):
        for j in range(2):
            lo_s[p, j] = jnp.where(hi_s[p, j] == hi_k16, lo_s[p, j], jnp.int16(I16_MIN))
        return carry

    lax.fori_loop(0, n_pairs, keep_group, 0)
    lo_k = _kth_largest_i16(lo_s, n_pairs, need - above)
    thr = hi_k * 65536 + (lo_k - I16_MIN)

    def count_keys(pred):
        def body(c, cnt):
            hit = jnp.where(pred(c, keys[c]), 1.0, 0.0)
            for k in range(CK // LANES):
                cnt = cnt + hit[:, k * LANES:(k + 1) * LANES]
            return cnt

        cnt = lax.fori_loop(0, n_chunks, body, jnp.zeros((TQ, LANES), F32))
        return jnp.sum(cnt, axis=1, keepdims=True)

    surplus = jnp.where(thr != INT_MIN, count_keys(lambda c, k: k >= thr) - need, 0.0)

    @pl.when(jnp.max(surplus) > 0.0)
    def _():
        keep_ties = need - count_keys(lambda c, k: k > thr)
        col_bits = max(1, (seq - 1).bit_length())

        def bit_step(step, t):
            cand = t + lax.shift_left(jnp.int32(1), col_bits - 1 - step)
            before = count_keys(lambda c, k: (k == thr) & (c * CK + cols < cand))
            return jnp.where(before < keep_ties, cand, t)

        last = lax.fori_loop(0, col_bits, bit_step, jnp.zeros((TQ, 1), jnp.int32))

        def demote(c, carry):
            k = keys[c]
            drop = (k == thr) & (c * CK + cols > last) & (surplus > 0.0)
            keys[c] = jnp.where(drop, k - 1, k)
            return carry

        lax.fori_loop(0, n_chunks, demote, 0)

    thr = jnp.maximum(thr, jnp.int32(INT_MIN + 1))

    for h in range(N_HEADS):
        qs[h * TQ:(h + 1) * TQ, 0:HEAD_DIM] = q_ref[:, h * HEAD_DIM:(h + 1) * HEAD_DIM]
    m_s[...] = jnp.full(m_s.shape, 0.5 * MASK_LOGIT, F32)
    accl_s[...] = jnp.zeros(accl_s.shape, F32)

    def attn_chunk(c, near):
        kc = k_t[c]
        vc = v_aug[pl.ds(pl.multiple_of(c * CK, CK), CK), :]
        mask_add = jnp.where(keys[c] >= thr, 0.0, MASK_LOGIT)
        tile = (q0 - c * CK) // BIAS_TILE_STEP
        for g in range(N_HEADS // ATTN_HEAD_GROUP):
            g0 = g * ATTN_HEAD_GROUP * TQ
            s_g = jnp.dot(qs[g0:g0 + ATTN_HEAD_GROUP * TQ, :], kc, preferred_element_type=F32)
            alphas = []
            for hh in range(ATTN_HEAD_GROUP):
                h = g * ATTN_HEAD_GROUP + hh
                s = s_g[hh * TQ:(hh + 1) * TQ]
                if near:
                    s = s + bias_s[tile, h]
                s = s + mask_add
                m_old = m_s[h]
                m_new = jnp.maximum(m_old, jnp.max(s, axis=1, keepdims=True))
                p = jnp.exp2(s - jnp.concatenate([m_new] * (CK // LANES), axis=1))
                alphas.append(jnp.exp2(m_old - m_new))
                m_s[h] = m_new
                pbuf[h * TQ:(h + 1) * TQ, :] = p.astype(BF16)
            pv = jnp.dot(pbuf[g0:g0 + ATTN_HEAD_GROUP * TQ, :], vc, preferred_element_type=F32)
            for hh in range(ATTN_HEAD_GROUP):
                h = g * ATTN_HEAD_GROUP + hh
                accl_s[h] = (jnp.concatenate([alphas[hh], alphas[hh]], axis=1) * accl_s[h]
                             + pv[hh * TQ:(hh + 1) * TQ])

    def attn_chunks(c, width, *, near):
        for k in range(width):
            attn_chunk(c + k, near)

    n_far = jnp.maximum((q0 - N_BIAS_TILES * BIAS_TILE_STEP) // CK + 1, 0)
    _for_chunks(0, n_far, functools.partial(attn_chunks, near=False))
    _for_chunks(n_far, n_chunks, functools.partial(attn_chunks, near=True))
    for h in range(N_HEADS):
        al = accl_s[h]
        o_ref[:, h * HEAD_DIM:(h + 1) * HEAD_DIM] = (
            al[:, 0:HEAD_DIM] / al[:, HEAD_DIM:2 * HEAD_DIM]).astype(o_ref.dtype)


def _sparse_attention(qi, kvw, q, rel_bias, knorm_g, knorm_b, top_k):
    B, S, q_w = q.shape
    qi_w = qi.shape[2]
    kvw_w = kvw.shape[2]
    n_all = S // CK
    assert n_all % 2 == 0
    kern = functools.partial(_attn_kernel, top_k=top_k, seq=S)
    return pl.pallas_call(
        kern,
        grid=(B, S // TQ),
        in_specs=[pl.BlockSpec(memory_space=pltpu.SMEM),
                  pl.BlockSpec((None, TQ, qi_w), lambda b, i: (b, i, 0)),
                  pl.BlockSpec((None, S, kvw_w), lambda b, i: (b, 0, 0)),
                  pl.BlockSpec((None, TQ, q_w), lambda b, i: (b, i, 0)),
                  pl.BlockSpec((1, IDX_DIM), lambda b, i: (0, 0)),
                  pl.BlockSpec((1, IDX_DIM), lambda b, i: (0, 0))],
        out_specs=pl.BlockSpec((None, TQ, q_w), lambda b, i: (b, i, 0)),
        out_shape=jax.ShapeDtypeStruct((B, S, q_w), BF16),
        scratch_shapes=[pltpu.VMEM((N_BIAS_TILES, N_HEADS, TQ, CK), F32),
                        pltpu.VMEM((n_all, IDX_DIM, CK), BF16),
                        pltpu.VMEM((n_all, 2 * HEAD_DIM, CK), BF16),
                        pltpu.VMEM((S, 2 * HEAD_DIM), BF16),
                        pltpu.VMEM((n_all, TQ, CK), jnp.int32),
                        pltpu.VMEM((n_all // 2, 2, TQ, CK), jnp.int16),
                        pltpu.VMEM((n_all // 2, 2, TQ, CK), jnp.int16),
                        pltpu.VMEM((IDX_HEADS * TQ, IDX_DIM), BF16),
                        pltpu.VMEM((IDX_HEADS, TQ, LANES), F32),
                        pltpu.VMEM((N_HEADS * TQ, 2 * HEAD_DIM), BF16),
                        pltpu.VMEM((N_HEADS * TQ, CK), BF16),
                        pltpu.VMEM((N_HEADS, TQ, LANES), F32),
                        pltpu.VMEM((N_HEADS, TQ, 2 * HEAD_DIM), F32)],
        compiler_params=_params(2),
        name="dsa_attention",
    )(rel_bias, qi, kvw, q, knorm_g.reshape(1, IDX_DIM), knorm_b.reshape(1, IDX_DIM))


def _merge_kernel(yl_ref, ya_ref, wl_ref, wa_ref, gl_ref, ga_ref, o_ref):
    pl_ = jnp.dot(yl_ref[...], wl_ref[...], preferred_element_type=F32)
    pa_ = jnp.dot(ya_ref[...], wa_ref[...], preferred_element_type=F32)
    merged = (_sigmoid(gl_ref[...].astype(F32)) * pl_
              + _sigmoid(ga_ref[...].astype(F32)) * pa_)
    o_ref[...] = merged.astype(o_ref.dtype)


def _gated_merge(y_lru, y_att, w_l, w_a, gates, tm, tn):
    M, D = y_lru.shape
    N = w_l.shape[1]
    nj = N // tn
    return pl.pallas_call(
        _merge_kernel,
        grid=(M // tm, nj),
        in_specs=[pl.BlockSpec((tm, D), lambda i, j: (i, 0)),
                  pl.BlockSpec((tm, D), lambda i, j: (i, 0)),
                  pl.BlockSpec((D, tn), lambda i, j: (0, j)),
                  pl.BlockSpec((D, tn), lambda i, j: (0, j)),
                  pl.BlockSpec((tm, tn), lambda i, j: (i, j)),
                  pl.BlockSpec((tm, tn), lambda i, j: (i, nj + j))],
        out_specs=pl.BlockSpec((tm, tn), lambda i, j: (i, j)),
        out_shape=jax.ShapeDtypeStruct((M, N), BF16),
        compiler_params=_params(2),
        name="gated_merge",
    )(y_lru, y_att, w_l, w_a, gates, gates)


def _layer_norm_rows(v, g, b):
    mu = jnp.mean(v, axis=-1, keepdims=True)
    var = jnp.mean(jnp.square(v - mu), axis=-1, keepdims=True)
    return (v - mu) * lax.rsqrt(var + LN_EPS) * g + b


def _outproj_ln_kernel(m_ref, x_ref, w_ref, g_ref, b_ref, o32_ref, o16_ref):
    proj = jnp.dot(m_ref[...], w_ref[...], preferred_element_type=F32)
    h = _layer_norm_rows(ALPHA * x_ref[...] + proj, g_ref[...], b_ref[...])
    o32_ref[...] = h
    o16_ref[...] = h.astype(BF16)


def _outproj_ln(merged, x2d, w_out, g, b, tm):
    M, D = x2d.shape
    return pl.pallas_call(
        _outproj_ln_kernel,
        grid=(M // tm,),
        in_specs=[pl.BlockSpec((tm, D), lambda i: (i, 0)),
                  pl.BlockSpec((tm, D), lambda i: (i, 0)),
                  pl.BlockSpec((D, D), lambda i: (0, 0)),
                  pl.BlockSpec((1, D), lambda i: (0, 0)),
                  pl.BlockSpec((1, D), lambda i: (0, 0))],
        out_specs=[pl.BlockSpec((tm, D), lambda i: (i, 0)),
                   pl.BlockSpec((tm, D), lambda i: (i, 0))],
        out_shape=[jax.ShapeDtypeStruct((M, D), F32),
                   jax.ShapeDtypeStruct((M, D), BF16)],
        compiler_params=_params(1),
        name="outproj_ln1",
    )(merged, x2d, w_out, g.reshape(1, D), b.reshape(1, D))


def _ffn_up_kernel(halo_ref, h_ref, wg_ref, wv_ref, cwg_ref, cwv_ref, cbg_ref, cbv_ref, o_ref,
                   wg16, wv16, *, tm, conv_w, tiles_per_seq):
    i = pl.program_id(1)

    @pl.when(i == 0)
    def _():
        wg16[...] = wg_ref[...].astype(BF16)
        wv16[...] = wv_ref[...].astype(BF16)

    keep = jnp.where(i % tiles_per_seq > 0, 1.0, 0.0).astype(BF16)
    lhs = jnp.concatenate([halo_ref[...] * keep, h_ref[...]], axis=0)

    def conv_half(lhs_rows, n_rows, w_ref, cw_ref, cb_ref):
        up = jnp.dot(lhs_rows, w_ref[...], preferred_element_type=F32)
        y = cb_ref[...] + cw_ref[conv_w - 1:conv_w, :] * up[SUBLANES:SUBLANES + n_rows]
        for j in range(conv_w - 1):
            back = pltpu.roll(up, conv_w - 1 - j, axis=0)
            y = y + cw_ref[j:j + 1, :] * back[SUBLANES:SUBLANES + n_rows]
        return y

    yg = conv_half(lhs, tm, wg16, cwg_ref, cbg_ref)
    yv = conv_half(lhs, tm, wv16, cwv_ref, cbv_ref)
    o_ref[...] = (_gelu_tanh(yg) * yv).astype(o_ref.dtype)


def _ffn_up(h16, w_up, conv_w, conv_b, S, tm, tn):
    M, D = h16.shape
    d_ff = w_up.shape[1] // 2
    width = conv_w.shape[0]
    assert width - 1 <= SUBLANES and S % tm == 0
    nj = d_ff // tn
    kern = functools.partial(_ffn_up_kernel, tm=tm, conv_w=width, tiles_per_seq=S // tm)
    halo_blocks = tm // SUBLANES
    cb2 = conv_b.reshape(1, 2 * d_ff)
    return pl.pallas_call(
        kern,
        grid=(nj, M // tm),
        in_specs=[pl.BlockSpec((SUBLANES, D), lambda j, i: (jnp.maximum(i * halo_blocks - 1, 0), 0)),
                  pl.BlockSpec((tm, D), lambda j, i: (i, 0)),
                  pl.BlockSpec((D, tn), lambda j, i: (0, j)),
                  pl.BlockSpec((D, tn), lambda j, i: (0, nj + j)),
                  pl.BlockSpec((width, tn), lambda j, i: (0, j)),
                  pl.BlockSpec((width, tn), lambda j, i: (0, nj + j)),
                  pl.BlockSpec((1, tn), lambda j, i: (0, j)),
                  pl.BlockSpec((1, tn), lambda j, i: (0, nj + j))],
        out_specs=pl.BlockSpec((tm, tn), lambda j, i: (i, j)),
        out_shape=jax.ShapeDtypeStruct((M, d_ff), BF16),
        scratch_shapes=[pltpu.VMEM((D, tn), BF16), pltpu.VMEM((D, tn), BF16)],
        compiler_params=_params(2),
        name="ffn_up_conv_geglu",
    )(h16, h16, w_up, w_up, conv_w, conv_w, cb2, cb2)


def _ffn_down_kernel(a_ref, h_ref, w_ref, g_ref, b_ref, o_ref):
    k = pl.program_id(1)

    @pl.when(k == 0)
    def _():
        o_ref[...] = jnp.dot(a_ref[...], w_ref[...], preferred_element_type=F32)

    @pl.when(k > 0)
    def _():
        o_ref[...] += jnp.dot(a_ref[...], w_ref[...], preferred_element_type=F32)

    @pl.when(k == pl.num_programs(1) - 1)
    def _():
        o_ref[...] = _layer_norm_rows(ALPHA * h_ref[...] + o_ref[...], g_ref[...], b_ref[...])


def _ffn_down_ln(act, h32, w_down, g, b, tm, tk):
    d_ff, D = w_down.shape
    M = act.shape[0]
    return pl.pallas_call(
        _ffn_down_kernel,
        grid=(M // tm, d_ff // tk),
        in_specs=[pl.BlockSpec((tm, tk), lambda i, k: (i, k)),
                  pl.BlockSpec((tm, D), lambda i, k: (i, 0)),
                  pl.BlockSpec((tk, D), lambda i, k: (k, 0)),
                  pl.BlockSpec((1, D), lambda i, k: (0, 0)),
                  pl.BlockSpec((1, D), lambda i, k: (0, 0))],
        out_specs=pl.BlockSpec((tm, D), lambda i, k: (i, 0)),
        out_shape=jax.ShapeDtypeStruct((M, D), F32),
        compiler_params=_params(2),
        name="ffn_down_ln2",
    )(act, h32, w_down, g.reshape(1, D), b.reshape(1, D))


def kernel(x, w_in, lru_conv_w, lru_conv_b, lru_gate_a_w, lru_gate_a_b, lru_gate_x_w, lru_gate_x_b,
           lru_lambda, idx_knorm_g, idx_knorm_b, rel_bias, w_proj_lru, w_proj_attn, w_out,
           ln1_g, ln1_b, ffn_w_up, ffn_conv_w, ffn_conv_b, ffn_w_down, ln2_g, ln2_b):
    B, S, D = x.shape
    assert B == SUBLANES and S % CK == 0 and w_in.shape[0] == DEPTH
    top_k = min(TOPK_MAX, S // 4)
    d_rnn = lru_conv_w.shape[-1]
    q_w = N_HEADS * HEAD_DIM
    qi_w = IDX_HEADS * IDX_DIM
    splits = (d_rnn, d_rnn, q_w, HEAD_DIM, HEAD_DIM, qi_w, IDX_DIM, IDX_HEADS, D, D)
    offs = [0]
    for s_ in splits:
        offs.append(offs[-1] + s_)
    M = B * S

    l = 0
    w = w_in[l].astype(BF16)
    x2d = x.reshape(M, D)
    w_lru = w[:, offs[0]:offs[2]]
    w_q = w[:, offs[2]:offs[3]]
    pad = LANES - IDX_DIM - IDX_HEADS
    w_kvw = jnp.concatenate([w[:, offs[3]:offs[5]], w[:, offs[6]:offs[8]],
                             jnp.zeros((D, pad), BF16)], axis=1)
    w_qi = w[:, offs[5]:offs[6]]
    w_g = w[:, offs[8]:offs[10]]

    tm_in = min(M, 2048)
    lxg, x16 = _project_and_cast(x2d, w_lru, BF16, min(M, 1024), 1024)
    lxg = lxg.reshape(B, S, 2 * d_rnn)
    q = _project(x16, w_q, BF16, tm_in, 1024,
                 out_scale=(HEAD_DIM ** -0.5) * LOG2E).reshape(B, S, q_w)
    kvw = _project(x16, w_kvw, F32, tm_in, w_kvw.shape[1]).reshape(B, S, w_kvw.shape[1])
    qi = _project(x16, w_qi, BF16, tm_in, 1024).reshape(B, S, qi_w)
    gates = _project(x16, w_g, BF16, tm_in, 1024)

    y_lru = _rg_lru(lxg, lru_conv_w[l], lru_conv_b[l], lru_gate_a_w[l], lru_gate_a_b[l],
                    lru_gate_x_w[l], lru_gate_x_b[l], lru_lambda[l], S, B, tc=min(S, 512), cbw=256)

    y_att = _sparse_attention(qi, kvw, q, rel_bias, idx_knorm_g[l], idx_knorm_b[l], top_k)

    merged = _gated_merge(y_lru.reshape(M, d_rnn), y_att.reshape(M, q_w),
                          w_proj_lru[l].astype(BF16), w_proj_attn[l].astype(BF16),
                          gates, tm=min(M, 1024), tn=512)
    h32, h16 = _outproj_ln(merged, x2d, w_out[l].astype(BF16), ln1_g[l], ln1_b[l], tm=512)
    act = _ffn_up(h16, ffn_w_up[l], ffn_conv_w[l], ffn_conv_b[l], S,
                  tm=min(S, 1024), tn=768)
    out = _ffn_down_ln(act, h32, ffn_w_down[l].astype(BF16), ln2_g[l], ln2_b[l],
                       tm=min(M, 1024), tk=1024)
    return out.reshape(B, S, D)
```

```python
import functools
import math

import jax
import jax.numpy as jnp
from jax import lax
from jax.experimental import pallas as pl
from jax.experimental.pallas import tpu as pltpu

N_HEADS = 16
HEAD_DIM = 128
IDX_HEADS = 16
IDX_DIM = 64
TOPK_MAX = 256
LRU_BLOCK = 128
LRU_C = 8.0
REL_BUCKETS = 32
REL_MAX_DIST = 128
LN_EPS = 1e-5
DEPTH = 1
ALPHA = (2.0 * DEPTH) ** 0.25

LANES = 128
SUBLANES = 8
VMEM_LIMIT_BYTES = 56 * 1024 * 1024

TQ = 256
CK = 256
HEAD_GROUP = 16
ATTN_HEAD_GROUP = 16
BIAS_TILE_STEP = 256
N_BIAS_TILES = 2
assert TQ % BIAS_TILE_STEP == 0 and CK % BIAS_TILE_STEP == 0
assert N_BIAS_TILES * BIAS_TILE_STEP - (CK - 1) >= REL_MAX_DIST

INT_MIN = -(2 ** 31)
MASK_LOGIT = -2e30
LOG2E = math.log2(math.e)
F32 = jnp.float32
BF16 = jnp.bfloat16


def _params(n_axes):
    return pltpu.CompilerParams(dimension_semantics=("arbitrary",) * n_axes,
                                vmem_limit_bytes=VMEM_LIMIT_BYTES)


def _matmul_kernel(x_ref, w_ref, o_ref, *, out_scale):
    acc = jnp.dot(x_ref[...], w_ref[...], preferred_element_type=F32)
    if out_scale != 1.0:
        acc = acc * out_scale
    o_ref[...] = acc.astype(o_ref.dtype)


def _project(x2d, w, out_dtype, tm, tn, out_scale=1.0, col0=0, n_cols=None):
    M, K = x2d.shape
    N = w.shape[1] if n_cols is None else n_cols
    assert col0 % tn == 0 and N % tn == 0
    j0 = col0 // tn
    return pl.pallas_call(
        functools.partial(_matmul_kernel, out_scale=out_scale),
        grid=(M // tm, N // tn),
        in_specs=[pl.BlockSpec((tm, K), lambda i, j: (i, 0)),
                  pl.BlockSpec((K, tn), lambda i, j: (0, j0 + j))],
        out_specs=pl.BlockSpec((tm, tn), lambda i, j: (i, j)),
        out_shape=jax.ShapeDtypeStruct((M, N), out_dtype),
        compiler_params=_params(2),
        name="in_proj",
    )(x2d, w)


def _matmul_cast_kernel(x_ref, w_ref, o_ref, x16_ref):
    @pl.when(pl.program_id(1) == 0)
    def _():
        x16_ref[...] = x_ref[...].astype(BF16)

    o_ref[...] = jnp.dot(x16_ref[...], w_ref[...], preferred_element_type=F32).astype(o_ref.dtype)


def _project_and_cast(x2d, w, n_cols, out_dtype, tm, tn):
    M, K = x2d.shape
    N = n_cols
    assert N % tn == 0
    return pl.pallas_call(
        _matmul_cast_kernel,
        grid=(M // tm, N // tn),
        in_specs=[pl.BlockSpec((tm, K), lambda i, j: (i, 0)),
                  pl.BlockSpec((K, tn), lambda i, j: (0, j))],
        out_specs=[pl.BlockSpec((tm, tn), lambda i, j: (i, j)),
                   pl.BlockSpec((tm, K), lambda i, j: (i, 0))],
        out_shape=[jax.ShapeDtypeStruct((M, N), out_dtype),
                   jax.ShapeDtypeStruct((M, K), BF16)],
        compiler_params=_params(2),
        name="in_proj_cast",
    )(x2d, w)


def _gelu_tanh(x):
    c = math.sqrt(2.0 / math.pi)
    half = 0.5 * x
    return half + half * jnp.tanh(x * (c + (c * 0.044715) * (x * x)))


def _sigmoid(x):
    return 0.5 * jnp.tanh(0.5 * x) + 0.5


def _lru_kernel(lx_ref, lg_ref, cw_ref, cb_ref, wa_ref, ba_ref, wx_ref, bx_ref, lam_ref, o_ref,
                xs, a_s, u_s, h_s, *, tc, cb_width, conv_w):
    ti = pl.program_id(1)
    halo_rows = (conv_w - 1) * SUBLANES
    n_slab = cb_width // LANES
    rows = tc * SUBLANES
    slab = lambda v, s: v[:, s * LANES:(s + 1) * LANES]

    @pl.when(ti == 0)
    def _():
        xs[:, 0:halo_rows, :] = jnp.zeros((n_slab, halo_rows, LANES), F32)
        h_s[...] = jnp.zeros(h_s.shape, F32)

    for b in range(SUBLANES):
        for s in range(n_slab):
            xs[s, pl.ds(halo_rows + b, tc, stride=SUBLANES), :] = (
                lx_ref[b, :, s * LANES:(s + 1) * LANES].astype(F32))
    y_parts = []
    for s in range(n_slab):
        cs = slice(s * LANES, (s + 1) * LANES)
        y = cb_ref[:, cs] + cw_ref[0:1, cs] * xs[s, 0:rows, :]
        for j in range(1, conv_w):
            y = y + cw_ref[j:j + 1, cs] * xs[s, j * SUBLANES:j * SUBLANES + rows, :]
        y_parts.append(y)
        xs[s, 0:halo_rows, :] = xs[s, rows:rows + halo_rows, :]
    y2 = jnp.concatenate(y_parts, axis=1)
    yb = y2.astype(BF16)
    r_parts, i_parts = [], []
    for n in range(cb_width // LRU_BLOCK):
        blk = yb[:, n * LRU_BLOCK:(n + 1) * LRU_BLOCK]
        r_parts.append(jnp.dot(blk, wa_ref[n], preferred_element_type=F32))
        i_parts.append(jnp.dot(blk, wx_ref[n], preferred_element_type=F32))
    r = _sigmoid(jnp.concatenate(r_parts, axis=1) + ba_ref[...])
    g_in = _sigmoid(jnp.concatenate(i_parts, axis=1) + bx_ref[...])

    z = -lam_ref[...]
    softplus = jnp.maximum(z, 0.0) + jnp.log1p(jnp.exp(-jnp.abs(z)))
    rate = (-LRU_C) * softplus
    log_a = rate * r
    a = jnp.exp2((rate * LOG2E) * r)
    mult = jnp.sqrt(-jnp.tanh(log_a) * (a * a + 1.0))
    gx = g_in * y2
    u = mult * gx
    for s in range(n_slab):
        a_s[s] = slab(a, s)
        u_s[s] = slab(u, s)

    @pl.when(ti == 0)
    def _():
        for s in range(n_slab):
            u_s[s, 0:SUBLANES, :] = slab(gx, s)[0:SUBLANES]

    def step(t, h):
        r0 = pl.multiple_of(t * SUBLANES, SUBLANES)
        new = []
        for s in range(n_slab):
            hs = a_s[s, pl.ds(r0, SUBLANES), :] * h[s] + u_s[s, pl.ds(r0, SUBLANES), :]
            u_s[s, pl.ds(r0, SUBLANES), :] = hs
            new.append(hs)
        return tuple(new)

    h_fin = lax.fori_loop(0, tc, step, tuple(h_s[s] for s in range(n_slab)), unroll=8)
    for s in range(n_slab):
        h_s[s] = h_fin[s]
    for b in range(SUBLANES):
        for s in range(n_slab):
            cs = slice(s * LANES, (s + 1) * LANES)
            h_b = u_s[s, pl.ds(b, tc, stride=SUBLANES), :]
            o_ref[b, :, cs] = (_gelu_tanh(lg_ref[b, :, cs].astype(F32)) * h_b).astype(o_ref.dtype)


def _rg_lru(lxg, conv_w, conv_b, gate_a_w, gate_a_b, gate_x_w, gate_x_b, lam, S, B, tc, cbw):
    d_rnn = conv_w.shape[1]
    width = conv_w.shape[0]
    ncb = d_rnn // cbw
    nblk = cbw // LRU_BLOCK
    n_slab = cbw // LANES
    row = lambda v: v.reshape(1, d_rnn)
    kern = functools.partial(_lru_kernel, tc=tc, cb_width=cbw, conv_w=width)
    vec_spec = pl.BlockSpec((1, cbw), lambda n, t: (0, n))
    return pl.pallas_call(
        kern,
        grid=(ncb, S // tc),
        in_specs=[pl.BlockSpec((B, tc, cbw), lambda n, t: (0, t, n)),
                  pl.BlockSpec((B, tc, cbw), lambda n, t: (0, t, ncb + n)),
                  pl.BlockSpec((width, cbw), lambda n, t: (0, n)),
                  vec_spec,
                  pl.BlockSpec((nblk, LRU_BLOCK, LRU_BLOCK), lambda n, t: (n, 0, 0)),
                  vec_spec,
                  pl.BlockSpec((nblk, LRU_BLOCK, LRU_BLOCK), lambda n, t: (n, 0, 0)),
                  vec_spec,
                  vec_spec],
        out_specs=pl.BlockSpec((B, tc, cbw), lambda n, t: (0, t, n)),
        out_shape=jax.ShapeDtypeStruct((B, S, d_rnn), BF16),
        scratch_shapes=[pltpu.VMEM((n_slab, (tc + width - 1) * B, LANES), F32),
                        pltpu.VMEM((n_slab, tc * B, LANES), F32),
                        pltpu.VMEM((n_slab, tc * B, LANES), F32),
                        pltpu.VMEM((n_slab, B, LANES), F32)],
        compiler_params=_params(2),
        name="rg_lru",
    )(lxg, lxg, conv_w, row(conv_b), gate_a_w.astype(BF16), row(gate_a_b),
      gate_x_w.astype(BF16), row(gate_x_b), row(lam))


I16_MIN = -(2 ** 15)
I16_MAX = 2 ** 15 - 1


def _far_bucket():
    max_exact = REL_BUCKETS // 2
    large = max_exact + int(math.log(REL_MAX_DIST / max_exact) / math.log(REL_MAX_DIST / max_exact)
                            * (REL_BUCKETS - max_exact))
    return min(large, REL_BUCKETS - 1)


def _fill_bias_tiles(rb_ref, bias_s):
    rows = lax.broadcasted_iota(jnp.int32, (TQ, CK), 0)
    cols = lax.broadcasted_iota(jnp.int32, (TQ, CK), 1)
    max_exact = REL_BUCKETS // 2

    def tile_body(n, carry):
        d = n // N_HEADS
        h = n % N_HEADS
        rel = d * BIAS_TILE_STEP + rows - cols
        nf = jnp.maximum(rel, 1).astype(F32)
        large = max_exact + (jnp.log(nf / max_exact) / math.log(REL_MAX_DIST / max_exact)
                             * (REL_BUCKETS - max_exact)).astype(jnp.int32)
        large = jnp.minimum(large, REL_BUCKETS - 1)
        bucket = jnp.where(rel < max_exact, rel, large)
        acc = jnp.zeros((TQ, CK), F32)
        for b in range(REL_BUCKETS):
            acc = jnp.where(bucket == b, rb_ref[b, h], acc)
        bias_s[d, h] = (acc - rb_ref[_far_bucket(), h]) * LOG2E
        return carry

    lax.fori_loop(0, N_BIAS_TILES * N_HEADS, tile_body, 0)


def _ordered_key(score):
    bits = pltpu.bitcast(score, jnp.int32)
    return jnp.where(bits >= 0, bits, bits ^ jnp.int32(0x7FFFFFFF))


def _count_ge(arr, n_pairs, cand):
    cand16 = jnp.broadcast_to(cand, (TQ, LANES)).astype(jnp.int16)
    cand16 = jnp.concatenate([cand16] * (CK // LANES), axis=1)
    one, zero = jnp.int16(1), jnp.int16(0)

    def count_pair(p, cnt):
        cnt = cnt + jnp.where(arr[p, 0] >= cand16, one, zero)
        return cnt + jnp.where(arr[p, 1] >= cand16, one, zero)

    cnt = lax.fori_loop(0, n_pairs, count_pair, jnp.zeros((TQ, CK), jnp.int16))
    return jnp.sum(cnt.astype(jnp.int32).astype(F32), axis=1, keepdims=True)


def _kth_largest_i16(arr, n_pairs, need):
    def bit_step(step, thr):
        cand = thr + lax.shift_left(jnp.int32(1), 15 - step)
        return jnp.where(_count_ge(arr, n_pairs, cand) >= need, cand, thr)

    return lax.fori_loop(0, 16, bit_step, jnp.full((TQ, 1), I16_MIN, jnp.int32))


def _for_chunks(lo, hi, chunks_fn):
    n = hi - lo

    def pair(p, carry):
        chunks_fn(lo + 2 * p, 2)
        return carry

    lax.fori_loop(0, n // 2, pair, 0)

    @pl.when(n % 2 == 1)
    def _():
        chunks_fn(hi - 1, 1)


def _attn_kernel(rb_ref, qi_ref, kvw_ref, q_ref, kg_ref, kb_ref, o_ref,
                 bias_s, kin_t, k_t, v_aug, keys, hi_s, lo_s, qis, wbuf, qs, pbuf, m_s, accl_s,
                 *, top_k, seq):
    i = pl.program_id(1)
    k_off, v_off, kw_off = 0, HEAD_DIM, 2 * HEAD_DIM
    n_all = seq // CK
    lane = lax.broadcasted_iota(jnp.int32, (TQ, LANES), 1)

    @pl.when((pl.program_id(0) == 0) & (i == 0))
    def _():
        _fill_bias_tiles(rb_ref, bias_s)

    @pl.when(i == 0)
    def _():
        kraw = kvw_ref[:, kw_off:kw_off + IDX_DIM]
        mu = jnp.mean(kraw, axis=-1, keepdims=True)
        var = jnp.mean(jnp.square(kraw - mu), axis=-1, keepdims=True)
        kn = (kraw - mu) * lax.rsqrt(var + LN_EPS) * kg_ref[...] + kb_ref[...]
        kn = jnp.concatenate([kn, jnp.zeros((seq, LANES - IDX_DIM), F32)], axis=1)
        sub = lax.broadcasted_iota(jnp.int32, (HEAD_DIM, CK), 0)
        ones_rows = jnp.where(sub < 2, 1.0, 0.0).astype(BF16)
        for c in range(n_all):
            kin_t[c] = kn[c * CK:(c + 1) * CK, :].T[0:IDX_DIM].astype(BF16)
            k_t[c, 0:HEAD_DIM, :] = kvw_ref[c * CK:(c + 1) * CK, k_off:k_off + HEAD_DIM].T.astype(BF16)
            k_t[c, HEAD_DIM:2 * HEAD_DIM, :] = ones_rows
        v_aug[:, 0:HEAD_DIM] = kvw_ref[:, v_off:v_off + HEAD_DIM].astype(BF16)
        v_aug[:, HEAD_DIM:2 * HEAD_DIM] = jnp.ones((seq, HEAD_DIM), BF16)
        for h in range(N_HEADS):
            fb = jnp.full((TQ, LANES), rb_ref[_far_bucket(), h] * LOG2E, F32)
            fb_hi = fb.astype(BF16).astype(F32)
            qs[h * TQ:(h + 1) * TQ, HEAD_DIM:2 * HEAD_DIM] = jnp.where(
                lane == 0, fb_hi, jnp.where(lane == 1, fb - fb_hi, 0.0)).astype(BF16)

    q0 = pl.multiple_of(i * TQ, TQ)
    n_chunks = (q0 + TQ + CK - 1) // CK
    n_pairs = (n_chunks + 1) // 2
    rows = q0 + lax.broadcasted_iota(jnp.int32, (TQ, CK), 0)
    cols = lax.broadcasted_iota(jnp.int32, (TQ, CK), 1)

    qib = qi_ref[...]
    wi = kvw_ref[pl.ds(q0, TQ), kw_off + IDX_DIM:kw_off + IDX_DIM + IDX_HEADS]
    wi = wi * ((IDX_DIM ** -0.5) * (IDX_HEADS ** -0.5))
    for h in range(IDX_HEADS):
        qis[h * TQ:(h + 1) * TQ, :] = qib[:, h * IDX_DIM:(h + 1) * IDX_DIM]
        wbuf[h] = jnp.broadcast_to(wi[:, h:h + 1], (TQ, LANES))

    def score_chunk(c):
        kc = kin_t[c]
        acc = jnp.zeros((TQ, CK), F32)
        for g in range(IDX_HEADS // HEAD_GROUP):
            g0 = g * HEAD_GROUP * TQ
            d_g = jnp.dot(qis[g0:g0 + HEAD_GROUP * TQ, :], kc, preferred_element_type=F32)
            for hh in range(HEAD_GROUP):
                w_h = wbuf[g * HEAD_GROUP + hh]
                acc = acc + jnp.concatenate([w_h] * (CK // LANES), axis=1) * jnp.maximum(
                    d_g[hh * TQ:(hh + 1) * TQ], 0.0)
        key = jnp.where(c * CK + cols <= rows, _ordered_key(acc), jnp.int32(INT_MIN))
        keys[c] = key
        hi_s[c // 2, c % 2] = lax.shift_right_arithmetic(key, 16).astype(jnp.int16)
        lo_s[c // 2, c % 2] = ((key & 0xFFFF) + I16_MIN).astype(jnp.int16)

    def score_chunks(c, width):
        for k in range(width):
            score_chunk(c + k)

    _for_chunks(0, n_chunks, score_chunks)

    @pl.when(n_chunks % 2 == 1)
    def _():
        hi_s[n_chunks // 2, 1] = jnp.full((TQ, CK), I16_MIN, jnp.int16)
        lo_s[n_chunks // 2, 1] = jnp.full((TQ, CK), I16_MIN, jnp.int16)

    need = jnp.full((TQ, 1), float(top_k), F32)
    hi_k = _kth_largest_i16(hi_s, n_pairs, need)
    above = jnp.where(hi_k == I16_MAX, 0.0,
                      _count_ge(hi_s, n_pairs, jnp.minimum(hi_k + 1, I16_MAX)))
    hi_k16 = jnp.broadcast_to(hi_k, (TQ, LANES)).astype(jnp.int16)
    hi_k16 = jnp.concatenate([hi_k16] * (CK // LANES), axis=1)

    def keep_group(p, carry):
        for j in range(2):
            lo_s[p, j] = jnp.where(hi_s[p, j] == hi_k16, lo_s[p, j], jnp.int16(I16_MIN))
        return carry

    lax.fori_loop(0, n_pairs, keep_group, 0)
    lo_k = _kth_largest_i16(lo_s, n_pairs, need - above)
    thr = hi_k * 65536 + (lo_k - I16_MIN)

    def count_keys(pred):
        def body(c, cnt):
            hit = jnp.where(pred(c, keys[c]), 1.0, 0.0)
            for k in range(CK // LANES):
                cnt = cnt + hit[:, k * LANES:(k + 1) * LANES]
            return cnt

        cnt = lax.fori_loop(0, n_chunks, body, jnp.zeros((TQ, LANES), F32))
        return jnp.sum(cnt, axis=1, keepdims=True)

    surplus = jnp.where(thr != INT_MIN, count_keys(lambda c, k: k >= thr) - need, 0.0)

    @pl.when(jnp.max(surplus) > 0.0)
    def _():
        keep_ties = need - count_keys(lambda c, k: k > thr)
        col_bits = max(1, (seq - 1).bit_length())

        def bit_step(step, t):
            cand = t + lax.shift_left(jnp.int32(1), col_bits - 1 - step)
            before = count_keys(lambda c, k: (k == thr) & (c * CK + cols < cand))
            return jnp.where(before < keep_ties, cand, t)

        last = lax.fori_loop(0, col_bits, bit_step, jnp.zeros((TQ, 1), jnp.int32))

        def demote(c, carry):
            k = keys[c]
            drop = (k == thr) & (c * CK + cols > last) & (surplus > 0.0)
            keys[c] = jnp.where(drop, k - 1, k)
            return carry

        lax.fori_loop(0, n_chunks, demote, 0)

    thr = jnp.maximum(thr, jnp.int32(INT_MIN + 1))

    for h in range(N_HEADS):
        qs[h * TQ:(h + 1) * TQ, 0:HEAD_DIM] = q_ref[:, h * HEAD_DIM:(h + 1) * HEAD_DIM]
    m_s[...] = jnp.full(m_s.shape, 0.5 * MASK_LOGIT, F32)
    accl_s[...] = jnp.zeros(accl_s.shape, F32)

    def attn_chunk(c, near):
        kc = k_t[c]
        vc = v_aug[pl.ds(pl.multiple_of(c * CK, CK), CK), :]
        mask_add = jnp.where(keys[c] >= thr, 0.0, MASK_LOGIT)
        tile = (q0 - c * CK) // BIAS_TILE_STEP
        for g in range(N_HEADS // ATTN_HEAD_GROUP):
            g0 = g * ATTN_HEAD_GROUP * TQ
            s_g = jnp.dot(qs[g0:g0 + ATTN_HEAD_GROUP * TQ, :], kc, preferred_element_type=F32)
            alphas = []
            for hh in range(ATTN_HEAD_GROUP):
                h = g * ATTN_HEAD_GROUP + hh
                s = s_g[hh * TQ:(hh + 1) * TQ]
                if near:
                    s = s + bias_s[tile, h]
                s = s + mask_add
                m_old = m_s[h]
                m_new = jnp.maximum(m_old, jnp.max(s, axis=1, keepdims=True))
                p = jnp.exp2(s - jnp.concatenate([m_new] * (CK // LANES), axis=1))
                alphas.append(jnp.exp2(m_old - m_new))
                m_s[h] = m_new
                pbuf[h * TQ:(h + 1) * TQ, :] = p.astype(BF16)
            pv = jnp.dot(pbuf[g0:g0 + ATTN_HEAD_GROUP * TQ, :], vc, preferred_element_type=F32)
            for hh in range(ATTN_HEAD_GROUP):
                h = g * ATTN_HEAD_GROUP + hh
                accl_s[h] = (jnp.concatenate([alphas[hh], alphas[hh]], axis=1) * accl_s[h]
                             + pv[hh * TQ:(hh + 1) * TQ])

    def attn_chunks(c, width, *, near):
        for k in range(width):
            attn_chunk(c + k, near)

    n_far = jnp.maximum((q0 - N_BIAS_TILES * BIAS_TILE_STEP) // CK + 1, 0)
    _for_chunks(0, n_far, functools.partial(attn_chunks, near=False))
    _for_chunks(n_far, n_chunks, functools.partial(attn_chunks, near=True))
    for h in range(N_HEADS):
        al = accl_s[h]
        o_ref[:, h * HEAD_DIM:(h + 1) * HEAD_DIM] = (
            al[:, 0:HEAD_DIM] / al[:, HEAD_DIM:2 * HEAD_DIM]).astype(o_ref.dtype)


def _sparse_attention(qi, kvw, q, rel_bias, knorm_g, knorm_b, top_k):
    B, S, q_w = q.shape
    qi_w = qi.shape[2]
    kvw_w = kvw.shape[2]
    n_all = S // CK
    assert n_all % 2 == 0
    kern = functools.partial(_attn_kernel, top_k=top_k, seq=S)
    return pl.pallas_call(
        kern,
        grid=(B, S // TQ),
        in_specs=[pl.BlockSpec(memory_space=pltpu.SMEM),
                  pl.BlockSpec((None, TQ, qi_w), lambda b, i: (b, i, 0)),
                  pl.BlockSpec((None, S, kvw_w), lambda b, i: (b, 0, 0)),
                  pl.BlockSpec((None, TQ, q_w), lambda b, i: (b, i, 0)),
                  pl.BlockSpec((1, IDX_DIM), lambda b, i: (0, 0)),
                  pl.BlockSpec((1, IDX_DIM), lambda b, i: (0, 0))],
        out_specs=pl.BlockSpec((None, TQ, q_w), lambda b, i: (b, i, 0)),
        out_shape=jax.ShapeDtypeStruct((B, S, q_w), BF16),
        scratch_shapes=[pltpu.VMEM((N_BIAS_TILES, N_HEADS, TQ, CK), F32),
                        pltpu.VMEM((n_all, IDX_DIM, CK), BF16),
                        pltpu.VMEM((n_all, 2 * HEAD_DIM, CK), BF16),
                        pltpu.VMEM((S, 2 * HEAD_DIM), BF16),
                        pltpu.VMEM((n_all, TQ, CK), jnp.int32),
                        pltpu.VMEM((n_all // 2, 2, TQ, CK), jnp.int16),
                        pltpu.VMEM((n_all // 2, 2, TQ, CK), jnp.int16),
                        pltpu.VMEM((IDX_HEADS * TQ, IDX_DIM), BF16),
                        pltpu.VMEM((IDX_HEADS, TQ, LANES), F32),
                        pltpu.VMEM((N_HEADS * TQ, 2 * HEAD_DIM), BF16),
                        pltpu.VMEM((N_HEADS * TQ, CK), BF16),
                        pltpu.VMEM((N_HEADS, TQ, LANES), F32),
                        pltpu.VMEM((N_HEADS, TQ, 2 * HEAD_DIM), F32)],
        compiler_params=_params(2),
        name="dsa_attention",
    )(rel_bias, qi, kvw, q, knorm_g.reshape(1, IDX_DIM), knorm_b.reshape(1, IDX_DIM))


def _merge_kernel(yl_ref, ya_ref, wl_ref, wa_ref, gl_ref, ga_ref, o_ref):
    pl_ = jnp.dot(yl_ref[...], wl_ref[...], preferred_element_type=F32)
    pa_ = jnp.dot(ya_ref[...], wa_ref[...], preferred_element_type=F32)
    merged = (_sigmoid(gl_ref[...].astype(F32)) * pl_
              + _sigmoid(ga_ref[...].astype(F32)) * pa_)
    o_ref[...] = merged.astype(o_ref.dtype)


def _gated_merge(y_lru, y_att, w_l, w_a, gates, tm, tn):
    M, D = y_lru.shape
    N = w_l.shape[1]
    nj = N // tn
    return pl.pallas_call(
        _merge_kernel,
        grid=(M // tm, nj),
        in_specs=[pl.BlockSpec((tm, D), lambda i, j: (i, 0)),
                  pl.BlockSpec((tm, D), lambda i, j: (i, 0)),
                  pl.BlockSpec((D, tn), lambda i, j: (0, j)),
                  pl.BlockSpec((D, tn), lambda i, j: (0, j)),
                  pl.BlockSpec((tm, tn), lambda i, j: (i, j)),
                  pl.BlockSpec((tm, tn), lambda i, j: (i, nj + j))],
        out_specs=pl.BlockSpec((tm, tn), lambda i, j: (i, j)),
        out_shape=jax.ShapeDtypeStruct((M, N), BF16),
        compiler_params=_params(2),
        name="gated_merge",
    )(y_lru, y_att, w_l, w_a, gates, gates)


def _layer_norm_rows(v, g, b):
    mu = jnp.mean(v, axis=-1, keepdims=True)
    var = jnp.mean(jnp.square(v - mu), axis=-1, keepdims=True)
    return (v - mu) * lax.rsqrt(var + LN_EPS) * g + b


def _outproj_ln_kernel(m_ref, x_ref, w_ref, g_ref, b_ref, o32_ref, o16_ref):
    proj = jnp.dot(m_ref[...], w_ref[...], preferred_element_type=F32)
    h = _layer_norm_rows(ALPHA * x_ref[...] + proj, g_ref[...], b_ref[...])
    o32_ref[...] = h
    o16_ref[...] = h.astype(BF16)


def _outproj_ln(merged, x2d, w_out, g, b, tm):
    M, D = x2d.shape
    return pl.pallas_call(
        _outproj_ln_kernel,
        grid=(M // tm,),
        in_specs=[pl.BlockSpec((tm, D), lambda i: (i, 0)),
                  pl.BlockSpec((tm, D), lambda i: (i, 0)),
                  pl.BlockSpec((D, D), lambda i: (0, 0)),
                  pl.BlockSpec((1, D), lambda i: (0, 0)),
                  pl.BlockSpec((1, D), lambda i: (0, 0))],
        out_specs=[pl.BlockSpec((tm, D), lambda i: (i, 0)),
                   pl.BlockSpec((tm, D), lambda i: (i, 0))],
        out_shape=[jax.ShapeDtypeStruct((M, D), F32),
                   jax.ShapeDtypeStruct((M, D), BF16)],
        compiler_params=_params(1),
        name="outproj_ln1",
    )(merged, x2d, w_out, g.reshape(1, D), b.reshape(1, D))


def _ffn_up_kernel(halo_ref, h_ref, wg_ref, wv_ref, cwg_ref, cwv_ref, cbg_ref, cbv_ref, o_ref,
                   wg16, wv16, *, tm, conv_w, tiles_per_seq):
    i = pl.program_id(1)

    @pl.when(i == 0)
    def _():
        wg16[...] = wg_ref[...].astype(BF16)
        wv16[...] = wv_ref[...].astype(BF16)

    keep = jnp.where(i % tiles_per_seq > 0, 1.0, 0.0).astype(BF16)
    lhs = jnp.concatenate([halo_ref[...] * keep, h_ref[...]], axis=0)

    def conv_half(lhs_rows, n_rows, w_ref, cw_ref, cb_ref):
        up = jnp.dot(lhs_rows, w_ref[...], preferred_element_type=F32)
        y = cb_ref[...] + cw_ref[conv_w - 1:conv_w, :] * up[SUBLANES:SUBLANES + n_rows]
        for j in range(conv_w - 1):
            back = pltpu.roll(up, conv_w - 1 - j, axis=0)
            y = y + cw_ref[j:j + 1, :] * back[SUBLANES:SUBLANES + n_rows]
        return y

    yg = conv_half(lhs, tm, wg16, cwg_ref, cbg_ref)
    yv = conv_half(lhs, tm, wv16, cwv_ref, cbv_ref)
    o_ref[...] = (_gelu_tanh(yg) * yv).astype(o_ref.dtype)


def _ffn_up(h16, w_up, conv_w, conv_b, S, tm, tn):
    M, D = h16.shape
    d_ff = w_up.shape[1] // 2
    width = conv_w.shape[0]
    assert width - 1 <= SUBLANES and S % tm == 0
    nj = d_ff // tn
    kern = functools.partial(_ffn_up_kernel, tm=tm, conv_w=width, tiles_per_seq=S // tm)
    halo_blocks = tm // SUBLANES
    cb2 = conv_b.reshape(1, 2 * d_ff)
    return pl.pallas_call(
        kern,
        grid=(nj, M // tm),
        in_specs=[pl.BlockSpec((SUBLANES, D), lambda j, i: (jnp.maximum(i * halo_blocks - 1, 0), 0)),
                  pl.BlockSpec((tm, D), lambda j, i: (i, 0)),
                  pl.BlockSpec((D, tn), lambda j, i: (0, j)),
                  pl.BlockSpec((D, tn), lambda j, i: (0, nj + j)),
                  pl.BlockSpec((width, tn), lambda j, i: (0, j)),
                  pl.BlockSpec((width, tn), lambda j, i: (0, nj + j)),
                  pl.BlockSpec((1, tn), lambda j, i: (0, j)),
                  pl.BlockSpec((1, tn), lambda j, i: (0, nj + j))],
        out_specs=pl.BlockSpec((tm, tn), lambda j, i: (i, j)),
        out_shape=jax.ShapeDtypeStruct((M, d_ff), BF16),
        scratch_shapes=[pltpu.VMEM((D, tn), BF16), pltpu.VMEM((D, tn), BF16)],
        compiler_params=_params(2),
        name="ffn_up_conv_geglu",
    )(h16, h16, w_up, w_up, conv_w, conv_w, cb2, cb2)


def _ffn_down_kernel(a_ref, h_ref, w_ref, g_ref, b_ref, o_ref):
    k = pl.program_id(1)

    @pl.when(k == 0)
    def _():
        o_ref[...] = jnp.dot(a_ref[...], w_ref[...], preferred_element_type=F32)

    @pl.when(k > 0)
    def _():
        o_ref[...] += jnp.dot(a_ref[...], w_ref[...], preferred_element_type=F32)

    @pl.when(k == pl.num_programs(1) - 1)
    def _():
        o_ref[...] = _layer_norm_rows(ALPHA * h_ref[...] + o_ref[...], g_ref[...], b_ref[...])


def _ffn_down_ln(act, h32, w_down, g, b, tm, tk):
    d_ff, D = w_down.shape
    M = act.shape[0]
    return pl.pallas_call(
        _ffn_down_kernel,
        grid=(M // tm, d_ff // tk),
        in_specs=[pl.BlockSpec((tm, tk), lambda i, k: (i, k)),
                  pl.BlockSpec((tm, D), lambda i, k: (i, 0)),
                  pl.BlockSpec((tk, D), lambda i, k: (k, 0)),
                  pl.BlockSpec((1, D), lambda i, k: (0, 0)),
                  pl.BlockSpec((1, D), lambda i, k: (0, 0))],
        out_specs=pl.BlockSpec((tm, D), lambda i, k: (i, 0)),
        out_shape=jax.ShapeDtypeStruct((M, D), F32),
        compiler_params=_params(2),
        name="ffn_down_ln2",
    )(act, h32, w_down, g.reshape(1, D), b.reshape(1, D))


def kernel(x, w_in, lru_conv_w, lru_conv_b, lru_gate_a_w, lru_gate_a_b, lru_gate_x_w, lru_gate_x_b,
           lru_lambda, idx_knorm_g, idx_knorm_b, rel_bias, w_proj_lru, w_proj_attn, w_out,
           ln1_g, ln1_b, ffn_w_up, ffn_conv_w, ffn_conv_b, ffn_w_down, ln2_g, ln2_b):
    B, S, D = x.shape
    assert B == SUBLANES and S % CK == 0 and w_in.shape[0] == DEPTH
    top_k = min(TOPK_MAX, S // 4)
    d_rnn = lru_conv_w.shape[-1]
    q_w = N_HEADS * HEAD_DIM
    qi_w = IDX_HEADS * IDX_DIM
    splits = (d_rnn, d_rnn, q_w, HEAD_DIM, HEAD_DIM, qi_w, IDX_DIM, IDX_HEADS, D, D)
    offs = [0]
    for s_ in splits:
        offs.append(offs[-1] + s_)
    M = B * S

    l = 0
    w = w_in[l].astype(BF16)
    x2d = x.reshape(M, D)
    pad = LANES - IDX_DIM - IDX_HEADS
    w_kvw = jnp.concatenate([w[:, offs[3]:offs[5]], w[:, offs[6]:offs[8]],
                             jnp.zeros((D, pad), BF16)], axis=1)
    w_qi = w[:, offs[5]:offs[6]]
    w_g = w[:, offs[8]:offs[10]]

    tm_in = min(M, 2048)
    lxg, x16 = _project_and_cast(x2d, w, offs[2], BF16, min(M, 1024), 1024)
    lxg = lxg.reshape(B, S, 2 * d_rnn)
    q = _project(x16, w, BF16, tm_in, 1024, out_scale=(HEAD_DIM ** -0.5) * LOG2E,
                 col0=offs[2], n_cols=q_w).reshape(B, S, q_w)
    kvw = _project(x16, w_kvw, F32, tm_in, w_kvw.shape[1]).reshape(B, S, w_kvw.shape[1])
    qi = _project(x16, w_qi, BF16, tm_in, 1024).reshape(B, S, qi_w)
    gates = _project(x16, w_g, BF16, tm_in, 1024)

    y_lru = _rg_lru(lxg, lru_conv_w[l], lru_conv_b[l], lru_gate_a_w[l], lru_gate_a_b[l],
                    lru_gate_x_w[l], lru_gate_x_b[l], lru_lambda[l], S, B, tc=min(S, 512), cbw=256)

    y_att = _sparse_attention(qi, kvw, q, rel_bias, idx_knorm_g[l], idx_knorm_b[l], top_k)

    merged = _gated_merge(y_lru.reshape(M, d_rnn), y_att.reshape(M, q_w),
                          w_proj_lru[l].astype(BF16), w_proj_attn[l].astype(BF16),
                          gates, tm=min(M, 1024), tn=512)
    h32, h16 = _outproj_ln(merged, x2d, w_out[l].astype(BF16), ln1_g[l], ln1_b[l], tm=512)
    act = _ffn_up(h16, ffn_w_up[l], ffn_conv_w[l], ffn_conv_b[l], S,
                  tm=min(S, 1024), tn=768)
    out = _ffn_down_ln(act, h32, ffn_w_down[l].astype(BF16), ln2_g[l], ln2_b[l],
                       tm=min(M, 1024), tk=1024)
    return out.reshape(B, S, D)
```

```python
import functools
import math

import jax
import jax.numpy as jnp
from jax import lax
from jax.experimental import pallas as pl
from jax.experimental.pallas import tpu as pltpu

N_HEADS = 16
HEAD_DIM = 128
IDX_HEADS = 16
IDX_DIM = 64
TOPK_MAX = 256
LRU_BLOCK = 128
LRU_C = 8.0
REL_BUCKETS = 32
REL_MAX_DIST = 128
LN_EPS = 1e-5
DEPTH = 1
ALPHA = (2.0 * DEPTH) ** 0.25

LANES = 128
SUBLANES = 8
VMEM_LIMIT_BYTES = 56 * 1024 * 1024

TQ = 256
CK = 256
HEAD_GROUP = 16
ATTN_HEAD_GROUP = 16
BIAS_TILE_STEP = 256
N_BIAS_TILES = 2
assert TQ % BIAS_TILE_STEP == 0 and CK % BIAS_TILE_STEP == 0
assert N_BIAS_TILES * BIAS_TILE_STEP - (CK - 1) >= REL_MAX_DIST

INT_MIN = -(2 ** 31)
MASK_LOGIT = -2e30
LOG2E = math.log2(math.e)
F32 = jnp.float32
BF16 = jnp.bfloat16


def _params(n_axes):
    return pltpu.CompilerParams(dimension_semantics=("arbitrary",) * n_axes,
                                vmem_limit_bytes=VMEM_LIMIT_BYTES)


def _matmul_kernel(x_ref, w_ref, o_ref, *, out_scale):
    acc = jnp.dot(x_ref[...], w_ref[...], preferred_element_type=F32)
    if out_scale != 1.0:
        acc = acc * out_scale
    o_ref[...] = acc.astype(o_ref.dtype)


def _project(x2d, w, out_dtype, tm, tn, out_scale=1.0):
    M, K = x2d.shape
    N = w.shape[1]
    return pl.pallas_call(
        functools.partial(_matmul_kernel, out_scale=out_scale),
        grid=(M // tm, N // tn),
        in_specs=[pl.BlockSpec((tm, K), lambda i, j: (i, 0)),
                  pl.BlockSpec((K, tn), lambda i, j: (0, j))],
        out_specs=pl.BlockSpec((tm, tn), lambda i, j: (i, j)),
        out_shape=jax.ShapeDtypeStruct((M, N), out_dtype),
        compiler_params=_params(2),
        name="in_proj",
    )(x2d, w)


def _matmul_cast_kernel(x_ref, w_ref, o_ref, x16_ref):
    @pl.when(pl.program_id(1) == 0)
    def _():
        x16_ref[...] = x_ref[...].astype(BF16)

    o_ref[...] = jnp.dot(x16_ref[...], w_ref[...], preferred_element_type=F32).astype(o_ref.dtype)


def _project_and_cast(x2d, w, out_dtype, tm, tn):
    M, K = x2d.shape
    N = w.shape[1]
    return pl.pallas_call(
        _matmul_cast_kernel,
        grid=(M // tm, N // tn),
        in_specs=[pl.BlockSpec((tm, K), lambda i, j: (i, 0)),
                  pl.BlockSpec((K, tn), lambda i, j: (0, j))],
        out_specs=[pl.BlockSpec((tm, tn), lambda i, j: (i, j)),
                   pl.BlockSpec((tm, K), lambda i, j: (i, 0))],
        out_shape=[jax.ShapeDtypeStruct((M, N), out_dtype),
                   jax.ShapeDtypeStruct((M, K), BF16)],
        compiler_params=_params(2),
        name="in_proj_cast",
    )(x2d, w)


def _gelu_tanh(x):
    c = math.sqrt(2.0 / math.pi)
    half = 0.5 * x
    return half + half * jnp.tanh(x * (c + (c * 0.044715) * (x * x)))


def _sigmoid(x):
    return 0.5 * jnp.tanh(0.5 * x) + 0.5


def _lru_kernel(lx_ref, lg_ref, cw_ref, cb_ref, wa_ref, ba_ref, wx_ref, bx_ref, lam_ref, o_ref,
                xs, a_s, u_s, h_s, *, tc, cb_width, conv_w):
    ti = pl.program_id(1)
    halo_rows = (conv_w - 1) * SUBLANES
    n_slab = cb_width // LANES
    rows = tc * SUBLANES
    slab = lambda v, s: v[:, s * LANES:(s + 1) * LANES]

    @pl.when(ti == 0)
    def _():
        xs[:, 0:halo_rows, :] = jnp.zeros((n_slab, halo_rows, LANES), F32)
        h_s[...] = jnp.zeros(h_s.shape, F32)

    for b in range(SUBLANES):
        for s in range(n_slab):
            xs[s, pl.ds(halo_rows + b, tc, stride=SUBLANES), :] = (
                lx_ref[b, :, s * LANES:(s + 1) * LANES].astype(F32))
    y_parts = []
    for s in range(n_slab):
        cs = slice(s * LANES, (s + 1) * LANES)
        y = cb_ref[:, cs] + cw_ref[0:1, cs] * xs[s, 0:rows, :]
        for j in range(1, conv_w):
            y = y + cw_ref[j:j + 1, cs] * xs[s, j * SUBLANES:j * SUBLANES + rows, :]
        y_parts.append(y)
        xs[s, 0:halo_rows, :] = xs[s, rows:rows + halo_rows, :]
    y2 = jnp.concatenate(y_parts, axis=1)
    yb = y2.astype(BF16)
    r_parts, i_parts = [], []
    for n in range(cb_width // LRU_BLOCK):
        blk = yb[:, n * LRU_BLOCK:(n + 1) * LRU_BLOCK]
        r_parts.append(jnp.dot(blk, wa_ref[n], preferred_element_type=F32))
        i_parts.append(jnp.dot(blk, wx_ref[n], preferred_element_type=F32))
    r = _sigmoid(jnp.concatenate(r_parts, axis=1) + ba_ref[...])
    g_in = _sigmoid(jnp.concatenate(i_parts, axis=1) + bx_ref[...])

    z = -lam_ref[...]
    softplus = jnp.maximum(z, 0.0) + jnp.log1p(jnp.exp(-jnp.abs(z)))
    rate = (-LRU_C) * softplus
    log_a = rate * r
    a = jnp.exp2((rate * LOG2E) * r)
    mult = jnp.sqrt(-jnp.tanh(log_a) * (a * a + 1.0))
    gx = g_in * y2
    u = mult * gx
    for s in range(n_slab):
        a_s[s] = slab(a, s)
        u_s[s] = slab(u, s)

    @pl.when(ti == 0)
    def _():
        for s in range(n_slab):
            u_s[s, 0:SUBLANES, :] = slab(gx, s)[0:SUBLANES]

    def step(t, h):
        r0 = pl.multiple_of(t * SUBLANES, SUBLANES)
        new = []
        for s in range(n_slab):
            hs = a_s[s, pl.ds(r0, SUBLANES), :] * h[s] + u_s[s, pl.ds(r0, SUBLANES), :]
            u_s[s, pl.ds(r0, SUBLANES), :] = hs
            new.append(hs)
        return tuple(new)

    h_fin = lax.fori_loop(0, tc, step, tuple(h_s[s] for s in range(n_slab)), unroll=8)
    for s in range(n_slab):
        h_s[s] = h_fin[s]
    for b in range(SUBLANES):
        for s in range(n_slab):
            cs = slice(s * LANES, (s + 1) * LANES)
            h_b = u_s[s, pl.ds(b, tc, stride=SUBLANES), :]
            o_ref[b, :, cs] = (_gelu_tanh(lg_ref[b, :, cs].astype(F32)) * h_b).astype(o_ref.dtype)


def _rg_lru(lxg, conv_w, conv_b, gate_a_w, gate_a_b, gate_x_w, gate_x_b, lam, S, B, tc, cbw):
    d_rnn = conv_w.shape[1]
    width = conv_w.shape[0]
    ncb = d_rnn // cbw
    nblk = cbw // LRU_BLOCK
    n_slab = cbw // LANES
    row = lambda v: v.reshape(1, d_rnn)
    kern = functools.partial(_lru_kernel, tc=tc, cb_width=cbw, conv_w=width)
    vec_spec = pl.BlockSpec((1, cbw), lambda n, t: (0, n))
    return pl.pallas_call(
        kern,
        grid=(ncb, S // tc),
        in_specs=[pl.BlockSpec((B, tc, cbw), lambda n, t: (0, t, n)),
                  pl.BlockSpec((B, tc, cbw), lambda n, t: (0, t, ncb + n)),
                  pl.BlockSpec((width, cbw), lambda n, t: (0, n)),
                  vec_spec,
                  pl.BlockSpec((nblk, LRU_BLOCK, LRU_BLOCK), lambda n, t: (n, 0, 0)),
                  vec_spec,
                  pl.BlockSpec((nblk, LRU_BLOCK, LRU_BLOCK), lambda n, t: (n, 0, 0)),
                  vec_spec,
                  vec_spec],
        out_specs=pl.BlockSpec((B, tc, cbw), lambda n, t: (0, t, n)),
        out_shape=jax.ShapeDtypeStruct((B, S, d_rnn), BF16),
        scratch_shapes=[pltpu.VMEM((n_slab, (tc + width - 1) * B, LANES), F32),
                        pltpu.VMEM((n_slab, tc * B, LANES), F32),
                        pltpu.VMEM((n_slab, tc * B, LANES), F32),
                        pltpu.VMEM((n_slab, B, LANES), F32)],
        compiler_params=_params(2),
        name="rg_lru",
    )(lxg, lxg, conv_w, row(conv_b), gate_a_w.astype(BF16), row(gate_a_b),
      gate_x_w.astype(BF16), row(gate_x_b), row(lam))


I16_MIN = -(2 ** 15)
I16_MAX = 2 ** 15 - 1


def _far_bucket():
    max_exact = REL_BUCKETS // 2
    large = max_exact + int(math.log(REL_MAX_DIST / max_exact) / math.log(REL_MAX_DIST / max_exact)
                            * (REL_BUCKETS - max_exact))
    return min(large, REL_BUCKETS - 1)


def _fill_bias_tiles(rb_ref, bias_s):
    rows = lax.broadcasted_iota(jnp.int32, (TQ, CK), 0)
    cols = lax.broadcasted_iota(jnp.int32, (TQ, CK), 1)
    max_exact = REL_BUCKETS // 2

    def tile_body(n, carry):
        d = n // N_HEADS
        h = n % N_HEADS
        rel = d * BIAS_TILE_STEP + rows - cols
        nf = jnp.maximum(rel, 1).astype(F32)
        large = max_exact + (jnp.log(nf / max_exact) / math.log(REL_MAX_DIST / max_exact)
                             * (REL_BUCKETS - max_exact)).astype(jnp.int32)
        large = jnp.minimum(large, REL_BUCKETS - 1)
        bucket = jnp.where(rel < max_exact, rel, large)
        acc = jnp.zeros((TQ, CK), F32)
        for b in range(REL_BUCKETS):
            acc = jnp.where(bucket == b, rb_ref[b, h], acc)
        bias_s[d, h] = (acc - rb_ref[_far_bucket(), h]) * LOG2E
        return carry

    lax.fori_loop(0, N_BIAS_TILES * N_HEADS, tile_body, 0)


def _ordered_key(score):
    bits = pltpu.bitcast(score, jnp.int32)
    return jnp.where(bits >= 0, bits, bits ^ jnp.int32(0x7FFFFFFF))


def _count_ge(arr, n_pairs, cand):
    cand16 = jnp.broadcast_to(cand, (TQ, LANES)).astype(jnp.int16)
    cand16 = jnp.concatenate([cand16] * (CK // LANES), axis=1)
    one, zero = jnp.int16(1), jnp.int16(0)

    def count_pair(p, cnt):
        cnt = cnt + jnp.where(arr[p, 0] >= cand16, one, zero)
        return cnt + jnp.where(arr[p, 1] >= cand16, one, zero)

    cnt = lax.fori_loop(0, n_pairs, count_pair, jnp.zeros((TQ, CK), jnp.int16))
    return jnp.sum(cnt.astype(jnp.int32).astype(F32), axis=1, keepdims=True)


def _kth_largest_i16(arr, n_pairs, need):
    def bit_step(step, thr):
        cand = thr + lax.shift_left(jnp.int32(1), 15 - step)
        return jnp.where(_count_ge(arr, n_pairs, cand) >= need, cand, thr)

    return lax.fori_loop(0, 16, bit_step, jnp.full((TQ, 1), I16_MIN, jnp.int32))


def _for_chunks(lo, hi, chunks_fn):
    n = hi - lo

    def pair(p, carry):
        chunks_fn(lo + 2 * p, 2)
        return carry

    lax.fori_loop(0, n // 2, pair, 0)

    @pl.when(n % 2 == 1)
    def _():
        chunks_fn(hi - 1, 1)


def _attn_kernel(rb_ref, qi_ref, kvw_ref, q_ref, kg_ref, kb_ref, o_ref,
                 bias_s, kin_t, k_t, v_aug, keys, hi_s, lo_s, thr_s, qis, wbuf, qs, pbuf, m_s, accl_s,
                 *, top_k, seq):
    i = pl.program_id(1)
    k_off, v_off, kw_off = 0, HEAD_DIM, 2 * HEAD_DIM
    n_all = seq // CK
    lane = lax.broadcasted_iota(jnp.int32, (TQ, LANES), 1)

    @pl.when((pl.program_id(0) == 0) & (i == 0))
    def _():
        _fill_bias_tiles(rb_ref, bias_s)

    @pl.when(i == 0)
    def _():
        kraw = kvw_ref[:, kw_off:kw_off + IDX_DIM]
        mu = jnp.mean(kraw, axis=-1, keepdims=True)
        var = jnp.mean(jnp.square(kraw - mu), axis=-1, keepdims=True)
        kn = (kraw - mu) * lax.rsqrt(var + LN_EPS) * kg_ref[...] + kb_ref[...]
        kn = jnp.concatenate([kn, jnp.zeros((seq, LANES - IDX_DIM), F32)], axis=1)
        sub = lax.broadcasted_iota(jnp.int32, (HEAD_DIM, CK), 0)
        ones_rows = jnp.where(sub < 2, 1.0, 0.0).astype(BF16)
        for c in range(n_all):
            kin_t[c] = kn[c * CK:(c + 1) * CK, :].T[0:IDX_DIM].astype(BF16)
            k_t[c, 0:HEAD_DIM, :] = kvw_ref[c * CK:(c + 1) * CK, k_off:k_off + HEAD_DIM].T.astype(BF16)
            k_t[c, HEAD_DIM:2 * HEAD_DIM, :] = ones_rows
        v_aug[:, 0:HEAD_DIM] = kvw_ref[:, v_off:v_off + HEAD_DIM].astype(BF16)
        v_aug[:, HEAD_DIM:2 * HEAD_DIM] = jnp.ones((seq, HEAD_DIM), BF16)
        for h in range(N_HEADS):
            fb = jnp.full((TQ, LANES), rb_ref[_far_bucket(), h] * LOG2E, F32)
            fb_hi = fb.astype(BF16).astype(F32)
            qs[h * TQ:(h + 1) * TQ, HEAD_DIM:2 * HEAD_DIM] = jnp.where(
                lane == 0, fb_hi, jnp.where(lane == 1, fb - fb_hi, 0.0)).astype(BF16)

    q0 = pl.multiple_of(i * TQ, TQ)
    n_chunks = (q0 + TQ + CK - 1) // CK
    n_pairs = (n_chunks + 1) // 2
    rows = q0 + lax.broadcasted_iota(jnp.int32, (TQ, CK), 0)
    cols = lax.broadcasted_iota(jnp.int32, (TQ, CK), 1)

    qib = qi_ref[...]
    wi = kvw_ref[pl.ds(q0, TQ), kw_off + IDX_DIM:kw_off + IDX_DIM + IDX_HEADS]
    wi = wi * ((IDX_DIM ** -0.5) * (IDX_HEADS ** -0.5))
    for h in range(IDX_HEADS):
        qis[h * TQ:(h + 1) * TQ, :] = qib[:, h * IDX_DIM:(h + 1) * IDX_DIM]
        wbuf[h] = jnp.broadcast_to(wi[:, h:h + 1], (TQ, LANES))

    def score_chunk(c):
        kc = kin_t[c]
        acc = jnp.zeros((TQ, CK), F32)
        for g in range(IDX_HEADS // HEAD_GROUP):
            g0 = g * HEAD_GROUP * TQ
            d_g = jnp.dot(qis[g0:g0 + HEAD_GROUP * TQ, :], kc, preferred_element_type=F32)
            for hh in range(HEAD_GROUP):
                w_h = wbuf[g * HEAD_GROUP + hh]
                acc = acc + jnp.concatenate([w_h] * (CK // LANES), axis=1) * jnp.maximum(
                    d_g[hh * TQ:(hh + 1) * TQ], 0.0)
        key = jnp.where(c * CK + cols <= rows, _ordered_key(acc), jnp.int32(INT_MIN))
        keys[c] = key
        hi_s[c // 2, c % 2] = lax.shift_right_arithmetic(key, 16).astype(jnp.int16)
        lo_s[c // 2, c % 2] = ((key & 0xFFFF) + I16_MIN).astype(jnp.int16)

    def score_chunks(c, width):
        for k in range(width):
            score_chunk(c + k)

    _for_chunks(0, n_chunks, score_chunks)

    @pl.when(n_chunks % 2 == 1)
    def _():
        hi_s[n_chunks // 2, 1] = jnp.full((TQ, CK), I16_MIN, jnp.int16)
        lo_s[n_chunks // 2, 1] = jnp.full((TQ, CK), I16_MIN, jnp.int16)

    thr_s[...] = jnp.full(thr_s.shape, INT_MIN + 1, jnp.int32)

    @pl.when(q0 + TQ > top_k)
    def _():
        need = jnp.full((TQ, 1), float(top_k), F32)
        hi_k = _kth_largest_i16(hi_s, n_pairs, need)
        above = jnp.where(hi_k == I16_MAX, 0.0,
                          _count_ge(hi_s, n_pairs, jnp.minimum(hi_k + 1, I16_MAX)))
        hi_k16 = jnp.broadcast_to(hi_k, (TQ, LANES)).astype(jnp.int16)
        hi_k16 = jnp.concatenate([hi_k16] * (CK // LANES), axis=1)

        def keep_group(p, carry):
            for j in range(2):
                lo_s[p, j] = jnp.where(hi_s[p, j] == hi_k16, lo_s[p, j], jnp.int16(I16_MIN))
            return carry

        lax.fori_loop(0, n_pairs, keep_group, 0)
        lo_k = _kth_largest_i16(lo_s, n_pairs, need - above)
        thr = hi_k * 65536 + (lo_k - I16_MIN)

        def count_keys(pred):
            def body(c, cnt):
                hit = jnp.where(pred(c, keys[c]), 1.0, 0.0)
                for k in range(CK // LANES):
                    cnt = cnt + hit[:, k * LANES:(k + 1) * LANES]
                return cnt

            cnt = lax.fori_loop(0, n_chunks, body, jnp.zeros((TQ, LANES), F32))
            return jnp.sum(cnt, axis=1, keepdims=True)

        surplus = jnp.where(thr != INT_MIN, count_keys(lambda c, k: k >= thr) - need, 0.0)

        @pl.when(jnp.max(surplus) > 0.0)
        def _():
            keep_ties = need - count_keys(lambda c, k: k > thr)
            col_bits = max(1, (seq - 1).bit_length())

            def bit_step(step, t):
                cand = t + lax.shift_left(jnp.int32(1), col_bits - 1 - step)
                before = count_keys(lambda c, k: (k == thr) & (c * CK + cols < cand))
                return jnp.where(before < keep_ties, cand, t)

            last = lax.fori_loop(0, col_bits, bit_step, jnp.zeros((TQ, 1), jnp.int32))

            def demote(c, carry):
                k = keys[c]
                drop = (k == thr) & (c * CK + cols > last) & (surplus > 0.0)
                keys[c] = jnp.where(drop, k - 1, k)
                return carry

            lax.fori_loop(0, n_chunks, demote, 0)

        thr_s[...] = jnp.broadcast_to(jnp.maximum(thr, jnp.int32(INT_MIN + 1)), thr_s.shape)

    thr = thr_s[:, 0:1]

    for h in range(N_HEADS):
        qs[h * TQ:(h + 1) * TQ, 0:HEAD_DIM] = q_ref[:, h * HEAD_DIM:(h + 1) * HEAD_DIM]
    m_s[...] = jnp.full(m_s.shape, 0.5 * MASK_LOGIT, F32)
    accl_s[...] = jnp.zeros(accl_s.shape, F32)

    def attn_chunk(c, near):
        kc = k_t[c]
        vc = v_aug[pl.ds(pl.multiple_of(c * CK, CK), CK), :]
        mask_add = jnp.where(keys[c] >= thr, 0.0, MASK_LOGIT)
        tile = (q0 - c * CK) // BIAS_TILE_STEP
        for g in range(N_HEADS // ATTN_HEAD_GROUP):
            g0 = g * ATTN_HEAD_GROUP * TQ
            s_g = jnp.dot(qs[g0:g0 + ATTN_HEAD_GROUP * TQ, :], kc, preferred_element_type=F32)
            alphas = []
            for hh in range(ATTN_HEAD_GROUP):
                h = g * ATTN_HEAD_GROUP + hh
                s = s_g[hh * TQ:(hh + 1) * TQ]
                if near:
                    s = s + bias_s[tile, h]
                s = s + mask_add
                m_old = m_s[h]
                m_new = jnp.maximum(m_old, jnp.max(s, axis=1, keepdims=True))
                p = jnp.exp2(s - jnp.concatenate([m_new] * (CK // LANES), axis=1))
                alphas.append(jnp.exp2(m_old - m_new))
                m_s[h] = m_new
                pbuf[h * TQ:(h + 1) * TQ, :] = p.astype(BF16)
            pv = jnp.dot(pbuf[g0:g0 + ATTN_HEAD_GROUP * TQ, :], vc, preferred_element_type=F32)
            for hh in range(ATTN_HEAD_GROUP):
                h = g * ATTN_HEAD_GROUP + hh
                accl_s[h] = (jnp.concatenate([alphas[hh], alphas[hh]], axis=1) * accl_s[h]
                             + pv[hh * TQ:(hh + 1) * TQ])

    def attn_chunks(c, width, *, near):
        for k in range(width):
            attn_chunk(c + k, near)

    n_far = jnp.maximum((q0 - N_BIAS_TILES * BIAS_TILE_STEP) // CK + 1, 0)
    _for_chunks(0, n_far, functools.partial(attn_chunks, near=False))
    _for_chunks(n_far, n_chunks, functools.partial(attn_chunks, near=True))
    for h in range(N_HEADS):
        al = accl_s[h]
        o_ref[:, h * HEAD_DIM:(h + 1) * HEAD_DIM] = (
            al[:, 0:HEAD_DIM] / al[:, HEAD_DIM:2 * HEAD_DIM]).astype(o_ref.dtype)


def _sparse_attention(qi, kvw, q, rel_bias, knorm_g, knorm_b, top_k):
    B, S, q_w = q.shape
    qi_w = qi.shape[2]
    kvw_w = kvw.shape[2]
    n_all = S // CK
    assert n_all % 2 == 0
    kern = functools.partial(_attn_kernel, top_k=top_k, seq=S)
    return pl.pallas_call(
        kern,
        grid=(B, S // TQ),
        in_specs=[pl.BlockSpec(memory_space=pltpu.SMEM),
                  pl.BlockSpec((None, TQ, qi_w), lambda b, i: (b, i, 0)),
                  pl.BlockSpec((None, S, kvw_w), lambda b, i: (b, 0, 0)),
                  pl.BlockSpec((None, TQ, q_w), lambda b, i: (b, i, 0)),
                  pl.BlockSpec((1, IDX_DIM), lambda b, i: (0, 0)),
                  pl.BlockSpec((1, IDX_DIM), lambda b, i: (0, 0))],
        out_specs=pl.BlockSpec((None, TQ, q_w), lambda b, i: (b, i, 0)),
        out_shape=jax.ShapeDtypeStruct((B, S, q_w), BF16),
        scratch_shapes=[pltpu.VMEM((N_BIAS_TILES, N_HEADS, TQ, CK), F32),
                        pltpu.VMEM((n_all, IDX_DIM, CK), BF16),
                        pltpu.VMEM((n_all, 2 * HEAD_DIM, CK), BF16),
                        pltpu.VMEM((S, 2 * HEAD_DIM), BF16),
                        pltpu.VMEM((n_all, TQ, CK), jnp.int32),
                        pltpu.VMEM((n_all // 2, 2, TQ, CK), jnp.int16),
                        pltpu.VMEM((n_all // 2, 2, TQ, CK), jnp.int16),
                        pltpu.VMEM((TQ, LANES), jnp.int32),
                        pltpu.VMEM((IDX_HEADS * TQ, IDX_DIM), BF16),
                        pltpu.VMEM((IDX_HEADS, TQ, LANES), F32),
                        pltpu.VMEM((N_HEADS * TQ, 2 * HEAD_DIM), BF16),
                        pltpu.VMEM((N_HEADS * TQ, CK), BF16),
                        pltpu.VMEM((N_HEADS, TQ, LANES), F32),
                        pltpu.VMEM((N_HEADS, TQ, 2 * HEAD_DIM), F32)],
        compiler_params=_params(2),
        name="dsa_attention",
    )(rel_bias, qi, kvw, q, knorm_g.reshape(1, IDX_DIM), knorm_b.reshape(1, IDX_DIM))


def _merge_kernel(yl_ref, ya_ref, wl_ref, wa_ref, gl_ref, ga_ref, o_ref):
    pl_ = jnp.dot(yl_ref[...], wl_ref[...], preferred_element_type=F32)
    pa_ = jnp.dot(ya_ref[...], wa_ref[...], preferred_element_type=F32)
    merged = (_sigmoid(gl_ref[...].astype(F32)) * pl_
              + _sigmoid(ga_ref[...].astype(F32)) * pa_)
    o_ref[...] = merged.astype(o_ref.dtype)


def _gated_merge(y_lru, y_att, w_l, w_a, gates, tm, tn):
    M, D = y_lru.shape
    N = w_l.shape[1]
    nj = N // tn
    return pl.pallas_call(
        _merge_kernel,
        grid=(M // tm, nj),
        in_specs=[pl.BlockSpec((tm, D), lambda i, j: (i, 0)),
                  pl.BlockSpec((tm, D), lambda i, j: (i, 0)),
                  pl.BlockSpec((D, tn), lambda i, j: (0, j)),
                  pl.BlockSpec((D, tn), lambda i, j: (0, j)),
                  pl.BlockSpec((tm, tn), lambda i, j: (i, j)),
                  pl.BlockSpec((tm, tn), lambda i, j: (i, nj + j))],
        out_specs=pl.BlockSpec((tm, tn), lambda i, j: (i, j)),
        out_shape=jax.ShapeDtypeStruct((M, N), BF16),
        compiler_params=_params(2),
        name="gated_merge",
    )(y_lru, y_att, w_l, w_a, gates, gates)


def _layer_norm_rows(v, g, b):
    mu = jnp.mean(v, axis=-1, keepdims=True)
    var = jnp.mean(jnp.square(v - mu), axis=-1, keepdims=True)
    return (v - mu) * lax.rsqrt(var + LN_EPS) * g + b


def _outproj_ln_kernel(m_ref, x_ref, w_ref, g_ref, b_ref, o32_ref, o16_ref):
    proj = jnp.dot(m_ref[...], w_ref[...], preferred_element_type=F32)
    h = _layer_norm_rows(ALPHA * x_ref[...] + proj, g_ref[...], b_ref[...])
    o32_ref[...] = h
    o16_ref[...] = h.astype(BF16)


def _outproj_ln(merged, x2d, w_out, g, b, tm):
    M, D = x2d.shape
    return pl.pallas_call(
        _outproj_ln_kernel,
        grid=(M // tm,),
        in_specs=[pl.BlockSpec((tm, D), lambda i: (i, 0)),
                  pl.BlockSpec((tm, D), lambda i: (i, 0)),
                  pl.BlockSpec((D, D), lambda i: (0, 0)),
                  pl.BlockSpec((1, D), lambda i: (0, 0)),
                  pl.BlockSpec((1, D), lambda i: (0, 0))],
        out_specs=[pl.BlockSpec((tm, D), lambda i: (i, 0)),
                   pl.BlockSpec((tm, D), lambda i: (i, 0))],
        out_shape=[jax.ShapeDtypeStruct((M, D), F32),
                   jax.ShapeDtypeStruct((M, D), BF16)],
        compiler_params=_params(1),
        name="outproj_ln1",
    )(merged, x2d, w_out, g.reshape(1, D), b.reshape(1, D))


def _ffn_up_kernel(halo_ref, h_ref, wg_ref, wv_ref, cwg_ref, cwv_ref, cbg_ref, cbv_ref, o_ref,
                   wg16, wv16, *, tm, conv_w, tiles_per_seq):
    i = pl.program_id(1)

    @pl.when(i == 0)
    def _():
        wg16[...] = wg_ref[...].astype(BF16)
        wv16[...] = wv_ref[...].astype(BF16)

    keep = jnp.where(i % tiles_per_seq > 0, 1.0, 0.0).astype(BF16)
    lhs = jnp.concatenate([halo_ref[...] * keep, h_ref[...]], axis=0)

    def conv_half(lhs_rows, n_rows, w_ref, cw_ref, cb_ref):
        up = jnp.dot(lhs_rows, w_ref[...], preferred_element_type=F32)
        y = cb_ref[...] + cw_ref[conv_w - 1:conv_w, :] * up[SUBLANES:SUBLANES + n_rows]
        for j in range(conv_w - 1):
            back = pltpu.roll(up, conv_w - 1 - j, axis=0)
            y = y + cw_ref[j:j + 1, :] * back[SUBLANES:SUBLANES + n_rows]
        return y

    yg = conv_half(lhs, tm, wg16, cwg_ref, cbg_ref)
    yv = conv_half(lhs, tm, wv16, cwv_ref, cbv_ref)
    o_ref[...] = (_gelu_tanh(yg) * yv).astype(o_ref.dtype)


def _ffn_up(h16, w_up, conv_w, conv_b, S, tm, tn):
    M, D = h16.shape
    d_ff = w_up.shape[1] // 2
    width = conv_w.shape[0]
    assert width - 1 <= SUBLANES and S % tm == 0
    nj = d_ff // tn
    kern = functools.partial(_ffn_up_kernel, tm=tm, conv_w=width, tiles_per_seq=S // tm)
    halo_blocks = tm // SUBLANES
    cb2 = conv_b.reshape(1, 2 * d_ff)
    return pl.pallas_call(
        kern,
        grid=(nj, M // tm),
        in_specs=[pl.BlockSpec((SUBLANES, D), lambda j, i: (jnp.maximum(i * halo_blocks - 1, 0), 0)),
                  pl.BlockSpec((tm, D), lambda j, i: (i, 0)),
                  pl.BlockSpec((D, tn), lambda j, i: (0, j)),
                  pl.BlockSpec((D, tn), lambda j, i: (0, nj + j)),
                  pl.BlockSpec((width, tn), lambda j, i: (0, j)),
                  pl.BlockSpec((width, tn), lambda j, i: (0, nj + j)),
                  pl.BlockSpec((1, tn), lambda j, i: (0, j)),
                  pl.BlockSpec((1, tn), lambda j, i: (0, nj + j))],
        out_specs=pl.BlockSpec((tm, tn), lambda j, i: (i, j)),
        out_shape=jax.ShapeDtypeStruct((M, d_ff), BF16),
        scratch_shapes=[pltpu.VMEM((D, tn), BF16), pltpu.VMEM((D, tn), BF16)],
        compiler_params=_params(2),
        name="ffn_up_conv_geglu",
    )(h16, h16, w_up, w_up, conv_w, conv_w, cb2, cb2)


def _ffn_down_kernel(a_ref, h_ref, w_ref, g_ref, b_ref, o_ref):
    k = pl.program_id(1)

    @pl.when(k == 0)
    def _():
        o_ref[...] = jnp.dot(a_ref[...], w_ref[...], preferred_element_type=F32)

    @pl.when(k > 0)
    def _():
        o_ref[...] += jnp.dot(a_ref[...], w_ref[...], preferred_element_type=F32)

    @pl.when(k == pl.num_programs(1) - 1)
    def _():
        o_ref[...] = _layer_norm_rows(ALPHA * h_ref[...] + o_ref[...], g_ref[...], b_ref[...])


def _ffn_down_ln(act, h32, w_down, g, b, tm, tk):
    d_ff, D = w_down.shape
    M = act.shape[0]
    return pl.pallas_call(
        _ffn_down_kernel,
        grid=(M // tm, d_ff // tk),
        in_specs=[pl.BlockSpec((tm, tk), lambda i, k: (i, k)),
                  pl.BlockSpec((tm, D), lambda i, k: (i, 0)),
                  pl.BlockSpec((tk, D), lambda i, k: (k, 0)),
                  pl.BlockSpec((1, D), lambda i, k: (0, 0)),
                  pl.BlockSpec((1, D), lambda i, k: (0, 0))],
        out_specs=pl.BlockSpec((tm, D), lambda i, k: (i, 0)),
        out_shape=jax.ShapeDtypeStruct((M, D), F32),
        compiler_params=_params(2),
        name="ffn_down_ln2",
    )(act, h32, w_down, g.reshape(1, D), b.reshape(1, D))


def kernel(x, w_in, lru_conv_w, lru_conv_b, lru_gate_a_w, lru_gate_a_b, lru_gate_x_w, lru_gate_x_b,
           lru_lambda, idx_knorm_g, idx_knorm_b, rel_bias, w_proj_lru, w_proj_attn, w_out,
           ln1_g, ln1_b, ffn_w_up, ffn_conv_w, ffn_conv_b, ffn_w_down, ln2_g, ln2_b):
    B, S, D = x.shape
    assert B == SUBLANES and S % CK == 0 and w_in.shape[0] == DEPTH
    top_k = min(TOPK_MAX, S // 4)
    d_rnn = lru_conv_w.shape[-1]
    q_w = N_HEADS * HEAD_DIM
    qi_w = IDX_HEADS * IDX_DIM
    splits = (d_rnn, d_rnn, q_w, HEAD_DIM, HEAD_DIM, qi_w, IDX_DIM, IDX_HEADS, D, D)
    offs = [0]
    for s_ in splits:
        offs.append(offs[-1] + s_)
    M = B * S

    l = 0
    w = w_in[l].astype(BF16)
    x2d = x.reshape(M, D)
    w_lru = w[:, offs[0]:offs[2]]
    w_q = w[:, offs[2]:offs[3]]
    pad = LANES - IDX_DIM - IDX_HEADS
    w_kvw = jnp.concatenate([w[:, offs[3]:offs[5]], w[:, offs[6]:offs[8]],
                             jnp.zeros((D, pad), BF16)], axis=1)
    w_qi = w[:, offs[5]:offs[6]]
    w_g = w[:, offs[8]:offs[10]]

    tm_in = min(M, 2048)
    lxg, x16 = _project_and_cast(x2d, w_lru, BF16, min(M, 1024), 1024)
    lxg = lxg.reshape(B, S, 2 * d_rnn)
    q = _project(x16, w_q, BF16, tm_in, 1024,
                 out_scale=(HEAD_DIM ** -0.5) * LOG2E).reshape(B, S, q_w)
    kvw = _project(x16, w_kvw, F32, tm_in, w_kvw.shape[1]).reshape(B, S, w_kvw.shape[1])
    qi = _project(x16, w_qi, BF16, tm_in, 1024).reshape(B, S, qi_w)
    gates = _project(x16, w_g, BF16, tm_in, 1024)

    y_lru = _rg_lru(lxg, lru_conv_w[l], lru_conv_b[l], lru_gate_a_w[l], lru_gate_a_b[l],
                    lru_gate_x_w[l], lru_gate_x_b[l], lru_lambda[l], S, B, tc=min(S, 512), cbw=256)

    y_att = _sparse_attention(qi, kvw, q, rel_bias, idx_knorm_g[l], idx_knorm_b[l], top_k)

    merged = _gated_merge(y_lru.reshape(M, d_rnn), y_att.reshape(M, q_w),
                          w_proj_lru[l].astype(BF16), w_proj_attn[l].astype(BF16),
                          gates, tm=min(M, 1024), tn=512)
    h32, h16 = _outproj_ln(merged, x2d, w_out[l].astype(BF16), ln1_g[l], ln1_b[l], tm=512)
    act = _ffn_up(h16, ffn_w_up[l], ffn_conv_w[l], ffn_conv_b[l], S,
                  tm=min(S, 1024), tn=768)
    out = _ffn_down_ln(act, h32, ffn_w_down[l].astype(BF16), ln2_g[l], ln2_b[l],
                       tm=min(M, 1024), tk=1024)
    return out.reshape(B, S, D)
```

```python
import functools
import math

import jax
import jax.numpy as jnp
from jax import lax
from jax.experimental import pallas as pl
from jax.experimental.pallas import tpu as pltpu

N_HEADS = 16
HEAD_DIM = 128
IDX_HEADS = 16
IDX_DIM = 64
TOPK_MAX = 256
LRU_BLOCK = 128
LRU_C = 8.0
REL_BUCKETS = 32
REL_MAX_DIST = 128
LN_EPS = 1e-5
DEPTH = 1
ALPHA = (2.0 * DEPTH) ** 0.25

LANES = 128
SUBLANES = 8
VMEM_LIMIT_BYTES = 56 * 1024 * 1024

TQ = 256
CK = 256
HEAD_GROUP = 16
ATTN_HEAD_GROUP = 16
BIAS_TILE_STEP = 256
N_BIAS_TILES = 2
assert TQ % BIAS_TILE_STEP == 0 and CK % BIAS_TILE_STEP == 0
assert N_BIAS_TILES * BIAS_TILE_STEP - (CK - 1) >= REL_MAX_DIST

INT_MIN = -(2 ** 31)
MASK_LOGIT = -2e30
LOG2E = math.log2(math.e)
F32 = jnp.float32
BF16 = jnp.bfloat16


def _params(n_axes):
    return pltpu.CompilerParams(dimension_semantics=("arbitrary",) * n_axes,
                                vmem_limit_bytes=VMEM_LIMIT_BYTES)


def _matmul_kernel(x_ref, w_ref, o_ref, *, out_scale):
    acc = jnp.dot(x_ref[...], w_ref[...], preferred_element_type=F32)
    if out_scale != 1.0:
        acc = acc * out_scale
    o_ref[...] = acc.astype(o_ref.dtype)


def _project(x2d, w, out_dtype, tm, tn, out_scale=1.0):
    M, K = x2d.shape
    N = w.shape[1]
    return pl.pallas_call(
        functools.partial(_matmul_kernel, out_scale=out_scale),
        grid=(M // tm, N // tn),
        in_specs=[pl.BlockSpec((tm, K), lambda i, j: (i, 0)),
                  pl.BlockSpec((K, tn), lambda i, j: (0, j))],
        out_specs=pl.BlockSpec((tm, tn), lambda i, j: (i, j)),
        out_shape=jax.ShapeDtypeStruct((M, N), out_dtype),
        compiler_params=_params(2),
        name="in_proj",
    )(x2d, w)


def _matmul_cast_kernel(x_ref, w_ref, o_ref, x16_ref):
    @pl.when(pl.program_id(1) == 0)
    def _():
        x16_ref[...] = x_ref[...].astype(BF16)

    o_ref[...] = jnp.dot(x16_ref[...], w_ref[...], preferred_element_type=F32).astype(o_ref.dtype)


def _project_and_cast(x2d, w, out_dtype, tm, tn):
    M, K = x2d.shape
    N = w.shape[1]
    return pl.pallas_call(
        _matmul_cast_kernel,
        grid=(M // tm, N // tn),
        in_specs=[pl.BlockSpec((tm, K), lambda i, j: (i, 0)),
                  pl.BlockSpec((K, tn), lambda i, j: (0, j))],
        out_specs=[pl.BlockSpec((tm, tn), lambda i, j: (i, j)),
                   pl.BlockSpec((tm, K), lambda i, j: (i, 0))],
        out_shape=[jax.ShapeDtypeStruct((M, N), out_dtype),
                   jax.ShapeDtypeStruct((M, K), BF16)],
        compiler_params=_params(2),
        name="in_proj_cast",
    )(x2d, w)


def _gelu_tanh(x):
    c = math.sqrt(2.0 / math.pi)
    half = 0.5 * x
    return half + half * jnp.tanh(x * (c + (c * 0.044715) * (x * x)))


def _sigmoid(x):
    return 0.5 * jnp.tanh(0.5 * x) + 0.5


def _lru_kernel(lx_ref, lg_ref, cw_ref, cb_ref, wa_ref, ba_ref, wx_ref, bx_ref, lam_ref, o_ref,
                xs, a_s, u_s, h_s, *, tc, cb_width, conv_w):
    ti = pl.program_id(1)
    halo_rows = (conv_w - 1) * SUBLANES
    n_slab = cb_width // LANES
    rows = tc * SUBLANES
    slab = lambda v, s: v[:, s * LANES:(s + 1) * LANES]

    @pl.when(ti == 0)
    def _():
        xs[:, 0:halo_rows, :] = jnp.zeros((n_slab, halo_rows, LANES), F32)
        h_s[...] = jnp.zeros(h_s.shape, F32)

    for b in range(SUBLANES):
        for s in range(n_slab):
            xs[s, pl.ds(halo_rows + b, tc, stride=SUBLANES), :] = (
                lx_ref[b, :, s * LANES:(s + 1) * LANES].astype(F32))
    y_parts = []
    for s in range(n_slab):
        cs = slice(s * LANES, (s + 1) * LANES)
        y = cb_ref[:, cs] + cw_ref[0:1, cs] * xs[s, 0:rows, :]
        for j in range(1, conv_w):
            y = y + cw_ref[j:j + 1, cs] * xs[s, j * SUBLANES:j * SUBLANES + rows, :]
        y_parts.append(y)
        xs[s, 0:halo_rows, :] = xs[s, rows:rows + halo_rows, :]
    y2 = jnp.concatenate(y_parts, axis=1)
    yb = y2.astype(BF16)
    r_parts, i_parts = [], []
    for n in range(cb_width // LRU_BLOCK):
        blk = yb[:, n * LRU_BLOCK:(n + 1) * LRU_BLOCK]
        r_parts.append(jnp.dot(blk, wa_ref[n], preferred_element_type=F32))
        i_parts.append(jnp.dot(blk, wx_ref[n], preferred_element_type=F32))
    r = _sigmoid(jnp.concatenate(r_parts, axis=1) + ba_ref[...])
    g_in = _sigmoid(jnp.concatenate(i_parts, axis=1) + bx_ref[...])

    z = -lam_ref[...]
    softplus = jnp.maximum(z, 0.0) + jnp.log1p(jnp.exp(-jnp.abs(z)))
    rate = (-LRU_C) * softplus
    log_a = rate * r
    a = jnp.exp2((rate * LOG2E) * r)
    mult = jnp.sqrt(-jnp.tanh(log_a) * (a * a + 1.0))
    gx = g_in * y2
    u = mult * gx
    for s in range(n_slab):
        a_s[s] = slab(a, s)
        u_s[s] = slab(u, s)

    @pl.when(ti == 0)
    def _():
        for s in range(n_slab):
            u_s[s, 0:SUBLANES, :] = slab(gx, s)[0:SUBLANES]

    def step(t, h):
        r0 = pl.multiple_of(t * SUBLANES, SUBLANES)
        new = []
        for s in range(n_slab):
            hs = a_s[s, pl.ds(r0, SUBLANES), :] * h[s] + u_s[s, pl.ds(r0, SUBLANES), :]
            u_s[s, pl.ds(r0, SUBLANES), :] = hs
            new.append(hs)
        return tuple(new)

    h_fin = lax.fori_loop(0, tc, step, tuple(h_s[s] for s in range(n_slab)), unroll=8)
    for s in range(n_slab):
        h_s[s] = h_fin[s]
    for b in range(SUBLANES):
        for s in range(n_slab):
            cs = slice(s * LANES, (s + 1) * LANES)
            h_b = u_s[s, pl.ds(b, tc, stride=SUBLANES), :]
            o_ref[b, :, cs] = (_gelu_tanh(lg_ref[b, :, cs].astype(F32)) * h_b).astype(o_ref.dtype)


def _rg_lru(lxg, conv_w, conv_b, gate_a_w, gate_a_b, gate_x_w, gate_x_b, lam, S, B, tc, cbw):
    d_rnn = conv_w.shape[1]
    width = conv_w.shape[0]
    ncb = d_rnn // cbw
    nblk = cbw // LRU_BLOCK
    n_slab = cbw // LANES
    row = lambda v: v.reshape(1, d_rnn)
    kern = functools.partial(_lru_kernel, tc=tc, cb_width=cbw, conv_w=width)
    vec_spec = pl.BlockSpec((1, cbw), lambda n, t: (0, n))
    return pl.pallas_call(
        kern,
        grid=(ncb, S // tc),
        in_specs=[pl.BlockSpec((B, tc, cbw), lambda n, t: (0, t, n)),
                  pl.BlockSpec((B, tc, cbw), lambda n, t: (0, t, ncb + n)),
                  pl.BlockSpec((width, cbw), lambda n, t: (0, n)),
                  vec_spec,
                  pl.BlockSpec((nblk, LRU_BLOCK, LRU_BLOCK), lambda n, t: (n, 0, 0)),
                  vec_spec,
                  pl.BlockSpec((nblk, LRU_BLOCK, LRU_BLOCK), lambda n, t: (n, 0, 0)),
                  vec_spec,
                  vec_spec],
        out_specs=pl.BlockSpec((B, tc, cbw), lambda n, t: (0, t, n)),
        out_shape=jax.ShapeDtypeStruct((B, S, d_rnn), BF16),
        scratch_shapes=[pltpu.VMEM((n_slab, (tc + width - 1) * B, LANES), F32),
                        pltpu.VMEM((n_slab, tc * B, LANES), F32),
                        pltpu.VMEM((n_slab, tc * B, LANES), F32),
                        pltpu.VMEM((n_slab, B, LANES), F32)],
        compiler_params=_params(2),
        name="rg_lru",
    )(lxg, lxg, conv_w, row(conv_b), gate_a_w.astype(BF16), row(gate_a_b),
      gate_x_w.astype(BF16), row(gate_x_b), row(lam))


I16_MIN = -(2 ** 15)
I16_MAX = 2 ** 15 - 1


def _far_bucket():
    max_exact = REL_BUCKETS // 2
    large = max_exact + int(math.log(REL_MAX_DIST / max_exact) / math.log(REL_MAX_DIST / max_exact)
                            * (REL_BUCKETS - max_exact))
    return min(large, REL_BUCKETS - 1)


def _fill_bias_tiles(rb_ref, bias_s):
    rows = lax.broadcasted_iota(jnp.int32, (TQ, CK), 0)
    cols = lax.broadcasted_iota(jnp.int32, (TQ, CK), 1)
    max_exact = REL_BUCKETS // 2

    def tile_body(n, carry):
        d = n // N_HEADS
        h = n % N_HEADS
        rel = d * BIAS_TILE_STEP + rows - cols
        nf = jnp.maximum(rel, 1).astype(F32)
        large = max_exact + (jnp.log(nf / max_exact) / math.log(REL_MAX_DIST / max_exact)
                             * (REL_BUCKETS - max_exact)).astype(jnp.int32)
        large = jnp.minimum(large, REL_BUCKETS - 1)
        bucket = jnp.where(rel < max_exact, rel, large)
        acc = jnp.zeros((TQ, CK), F32)
        for b in range(REL_BUCKETS):
            acc = jnp.where(bucket == b, rb_ref[b, h], acc)
        bias_s[d, h] = (acc - rb_ref[_far_bucket(), h]) * LOG2E
        return carry

    lax.fori_loop(0, N_BIAS_TILES * N_HEADS, tile_body, 0)


def _ordered_key(score):
    bits = pltpu.bitcast(score, jnp.int32)
    return jnp.where(bits >= 0, bits, bits ^ jnp.int32(0x7FFFFFFF))


def _count_ge(arr, n_pairs, cand):
    cand16 = jnp.broadcast_to(cand, (TQ, LANES)).astype(jnp.int16)
    cand16 = jnp.concatenate([cand16] * (CK // LANES), axis=1)
    one, zero = jnp.int16(1), jnp.int16(0)

    def count_pair(p, cnt):
        cnt = cnt + jnp.where(arr[p, 0] >= cand16, one, zero)
        return cnt + jnp.where(arr[p, 1] >= cand16, one, zero)

    cnt = lax.fori_loop(0, n_pairs, count_pair, jnp.zeros((TQ, CK), jnp.int16))
    return jnp.sum(cnt.astype(jnp.int32).astype(F32), axis=1, keepdims=True)


def _kth_largest_i16(arr, n_pairs, need):
    def bit_step(step, thr):
        cand = thr + lax.shift_left(jnp.int32(1), 15 - step)
        return jnp.where(_count_ge(arr, n_pairs, cand) >= need, cand, thr)

    return lax.fori_loop(0, 16, bit_step, jnp.full((TQ, 1), I16_MIN, jnp.int32))


def _for_chunks(lo, hi, chunks_fn):
    n = hi - lo

    def pair(p, carry):
        chunks_fn(lo + 2 * p, 2)
        return carry

    lax.fori_loop(0, n // 2, pair, 0)

    @pl.when(n % 2 == 1)
    def _():
        chunks_fn(hi - 1, 1)


def _attn_kernel(rb_ref, qi_ref, kvw_ref, q_ref, kg_ref, kb_ref, o_ref,
                 bias_s, kin_t, k_t, v_aug, keys, hi_s, lo_s, thr_s, qis, wbuf, qs, pbuf, m_s, accl_s,
                 *, top_k, seq):
    i = pl.program_id(1)
    k_off, v_off, kw_off = 0, HEAD_DIM, 2 * HEAD_DIM
    n_all = seq // CK
    lane = lax.broadcasted_iota(jnp.int32, (TQ, LANES), 1)

    @pl.when((pl.program_id(0) == 0) & (i == 0))
    def _():
        _fill_bias_tiles(rb_ref, bias_s)

    @pl.when(i == 0)
    def _():
        kraw = kvw_ref[:, kw_off:kw_off + IDX_DIM]
        mu = jnp.mean(kraw, axis=-1, keepdims=True)
        var = jnp.mean(jnp.square(kraw - mu), axis=-1, keepdims=True)
        kn = (kraw - mu) * lax.rsqrt(var + LN_EPS) * kg_ref[...] + kb_ref[...]
        kn = jnp.concatenate([kn, jnp.zeros((seq, LANES - IDX_DIM), F32)], axis=1)
        sub = lax.broadcasted_iota(jnp.int32, (HEAD_DIM, CK), 0)
        ones_rows = jnp.where(sub < 2, 1.0, 0.0).astype(BF16)
        for c in range(n_all):
            kin_t[c] = kn[c * CK:(c + 1) * CK, :].T[0:IDX_DIM].astype(BF16)
            k_t[c, 0:HEAD_DIM, :] = kvw_ref[c * CK:(c + 1) * CK, k_off:k_off + HEAD_DIM].T.astype(BF16)
            k_t[c, HEAD_DIM:2 * HEAD_DIM, :] = ones_rows
        v_aug[:, 0:HEAD_DIM] = kvw_ref[:, v_off:v_off + HEAD_DIM].astype(BF16)
        v_aug[:, HEAD_DIM:2 * HEAD_DIM] = jnp.ones((seq, HEAD_DIM), BF16)
        for h in range(N_HEADS):
            fb = jnp.full((TQ, LANES), rb_ref[_far_bucket(), h] * LOG2E, F32)
            fb_hi = fb.astype(BF16).astype(F32)
            qs[h * TQ:(h + 1) * TQ, HEAD_DIM:2 * HEAD_DIM] = jnp.where(
                lane == 0, fb_hi, jnp.where(lane == 1, fb - fb_hi, 0.0)).astype(BF16)

    q0 = pl.multiple_of(i * TQ, TQ)
    n_chunks = (q0 + TQ + CK - 1) // CK
    n_pairs = (n_chunks + 1) // 2
    rows = q0 + lax.broadcasted_iota(jnp.int32, (TQ, CK), 0)
    cols = lax.broadcasted_iota(jnp.int32, (TQ, CK), 1)

    qib = qi_ref[...]
    wi = kvw_ref[pl.ds(q0, TQ), kw_off + IDX_DIM:kw_off + IDX_DIM + IDX_HEADS]
    wi = wi * ((IDX_DIM ** -0.5) * (IDX_HEADS ** -0.5))
    for h in range(IDX_HEADS):
        qis[h * TQ:(h + 1) * TQ, :] = qib[:, h * IDX_DIM:(h + 1) * IDX_DIM]
        wbuf[h] = jnp.broadcast_to(wi[:, h:h + 1], (TQ, LANES))

    def score_chunk(c):
        kc = kin_t[c]
        acc = jnp.zeros((TQ, CK), F32)
        for g in range(IDX_HEADS // HEAD_GROUP):
            g0 = g * HEAD_GROUP * TQ
            d_g = jnp.dot(qis[g0:g0 + HEAD_GROUP * TQ, :], kc, preferred_element_type=F32)
            for hh in range(HEAD_GROUP):
                w_h = wbuf[g * HEAD_GROUP + hh]
                acc = acc + jnp.concatenate([w_h] * (CK // LANES), axis=1) * jnp.maximum(
                    d_g[hh * TQ:(hh + 1) * TQ], 0.0)
        key = jnp.where(c * CK + cols <= rows, _ordered_key(acc), jnp.int32(INT_MIN))
        keys[c] = key
        hi_s[c // 2, c % 2] = lax.shift_right_arithmetic(key, 16).astype(jnp.int16)
        lo_s[c // 2, c % 2] = ((key & 0xFFFF) + I16_MIN).astype(jnp.int16)

    def score_chunks(c, width):
        for k in range(width):
            score_chunk(c + k)

    _for_chunks(0, n_chunks, score_chunks)

    @pl.when(n_chunks % 2 == 1)
    def _():
        hi_s[n_chunks // 2, 1] = jnp.full((TQ, CK), I16_MIN, jnp.int16)
        lo_s[n_chunks // 2, 1] = jnp.full((TQ, CK), I16_MIN, jnp.int16)

    thr_s[...] = jnp.full(thr_s.shape, INT_MIN + 1, jnp.int32)

    @pl.when(q0 + TQ > top_k)
    def _():
        need = jnp.full((TQ, 1), float(top_k), F32)
        hi_k = _kth_largest_i16(hi_s, n_pairs, need)
        above = jnp.where(hi_k == I16_MAX, 0.0,
                          _count_ge(hi_s, n_pairs, jnp.minimum(hi_k + 1, I16_MAX)))
        hi_k16 = jnp.broadcast_to(hi_k, (TQ, LANES)).astype(jnp.int16)
        hi_k16 = jnp.concatenate([hi_k16] * (CK // LANES), axis=1)

        def keep_group(p, carry):
            for j in range(2):
                lo_s[p, j] = jnp.where(hi_s[p, j] == hi_k16, lo_s[p, j], jnp.int16(I16_MIN))
            return carry

        lax.fori_loop(0, n_pairs, keep_group, 0)
        lo_k = _kth_largest_i16(lo_s, n_pairs, need - above)
        thr = hi_k * 65536 + (lo_k - I16_MIN)

        def count_keys(pred):
            def body(c, cnt):
                hit = jnp.where(pred(c, keys[c]), 1.0, 0.0)
                for k in range(CK // LANES):
                    cnt = cnt + hit[:, k * LANES:(k + 1) * LANES]
                return cnt

            cnt = lax.fori_loop(0, n_chunks, body, jnp.zeros((TQ, LANES), F32))
            return jnp.sum(cnt, axis=1, keepdims=True)

        surplus = jnp.where(thr != INT_MIN, count_keys(lambda c, k: k >= thr) - need, 0.0)

        @pl.when(jnp.max(surplus) > 0.0)
        def _():
            keep_ties = need - count_keys(lambda c, k: k > thr)
            col_bits = max(1, (seq - 1).bit_length())

            def bit_step(step, t):
                cand = t + lax.shift_left(jnp.int32(1), col_bits - 1 - step)
                before = count_keys(lambda c, k: (k == thr) & (c * CK + cols < cand))
                return jnp.where(before < keep_ties, cand, t)

            last = lax.fori_loop(0, col_bits, bit_step, jnp.zeros((TQ, 1), jnp.int32))

            def demote(c, carry):
                k = keys[c]
                drop = (k == thr) & (c * CK + cols > last) & (surplus > 0.0)
                keys[c] = jnp.where(drop, k - 1, k)
                return carry

            lax.fori_loop(0, n_chunks, demote, 0)

        thr_s[...] = jnp.broadcast_to(jnp.maximum(thr, jnp.int32(INT_MIN + 1)), thr_s.shape)

    thr = thr_s[:, 0:1]

    for h in range(N_HEADS):
        qs[h * TQ:(h + 1) * TQ, 0:HEAD_DIM] = q_ref[:, h * HEAD_DIM:(h + 1) * HEAD_DIM]
    m_s[...] = jnp.full(m_s.shape, 0.5 * MASK_LOGIT, F32)
    accl_s[...] = jnp.zeros(accl_s.shape, F32)

    def attn_chunk(c, near):
        kc = k_t[c]
        vc = v_aug[pl.ds(pl.multiple_of(c * CK, CK), CK), :]
        mask_add = jnp.where(keys[c] >= thr, 0.0, MASK_LOGIT)
        tile = (q0 - c * CK) // BIAS_TILE_STEP
        for g in range(N_HEADS // ATTN_HEAD_GROUP):
            g0 = g * ATTN_HEAD_GROUP * TQ
            s_g = jnp.dot(qs[g0:g0 + ATTN_HEAD_GROUP * TQ, :], kc, preferred_element_type=F32)
            alphas = []
            for hh in range(ATTN_HEAD_GROUP):
                h = g * ATTN_HEAD_GROUP + hh
                s = s_g[hh * TQ:(hh + 1) * TQ]
                if near:
                    s = s + bias_s[tile, h]
                s = s + mask_add
                m_old = m_s[h]
                m_new = jnp.maximum(m_old, jnp.max(s, axis=1, keepdims=True))
                p = jnp.exp2(s - jnp.concatenate([m_new] * (CK // LANES), axis=1))
                alphas.append(jnp.exp2(m_old - m_new))
                m_s[h] = m_new
                pbuf[h * TQ:(h + 1) * TQ, :] = p.astype(BF16)
            pv = jnp.dot(pbuf[g0:g0 + ATTN_HEAD_GROUP * TQ, :], vc, preferred_element_type=F32)
            for hh in range(ATTN_HEAD_GROUP):
                h = g * ATTN_HEAD_GROUP + hh
                accl_s[h] = (jnp.concatenate([alphas[hh], alphas[hh]], axis=1) * accl_s[h]
                             + pv[hh * TQ:(hh + 1) * TQ])

    def attn_chunks(c, width, *, near):
        for k in range(width):
            attn_chunk(c + k, near)

    n_far = jnp.maximum((q0 - N_BIAS_TILES * BIAS_TILE_STEP) // CK + 1, 0)
    _for_chunks(0, n_far, functools.partial(attn_chunks, near=False))
    _for_chunks(n_far, n_chunks, functools.partial(attn_chunks, near=True))
    for h in range(N_HEADS):
        al = accl_s[h]
        o_ref[:, h * HEAD_DIM:(h + 1) * HEAD_DIM] = (
            al[:, 0:HEAD_DIM] / al[:, HEAD_DIM:2 * HEAD_DIM]).astype(o_ref.dtype)


def _sparse_attention(qi, kvw, q, rel_bias, knorm_g, knorm_b, top_k):
    B, S, q_w = q.shape
    qi_w = qi.shape[2]
    kvw_w = kvw.shape[2]
    n_all = S // CK
    assert n_all % 2 == 0
    kern = functools.partial(_attn_kernel, top_k=top_k, seq=S)
    return pl.pallas_call(
        kern,
        grid=(B, S // TQ),
        in_specs=[pl.BlockSpec(memory_space=pltpu.SMEM),
                  pl.BlockSpec((None, TQ, qi_w), lambda b, i: (b, i, 0)),
                  pl.BlockSpec((None, S, kvw_w), lambda b, i: (b, 0, 0)),
                  pl.BlockSpec((None, TQ, q_w), lambda b, i: (b, i, 0)),
                  pl.BlockSpec((1, IDX_DIM), lambda b, i: (0, 0)),
                  pl.BlockSpec((1, IDX_DIM), lambda b, i: (0, 0))],
        out_specs=pl.BlockSpec((None, TQ, q_w), lambda b, i: (b, i, 0)),
        out_shape=jax.ShapeDtypeStruct((B, S, q_w), BF16),
        scratch_shapes=[pltpu.VMEM((N_BIAS_TILES, N_HEADS, TQ, CK), F32),
                        pltpu.VMEM((n_all, IDX_DIM, CK), BF16),
                        pltpu.VMEM((n_all, 2 * HEAD_DIM, CK), BF16),
                        pltpu.VMEM((S, 2 * HEAD_DIM), BF16),
                        pltpu.VMEM((n_all, TQ, CK), jnp.int32),
                        pltpu.VMEM((n_all // 2, 2, TQ, CK), jnp.int16),
                        pltpu.VMEM((n_all // 2, 2, TQ, CK), jnp.int16),
                        pltpu.VMEM((TQ, LANES), jnp.int32),
                        pltpu.VMEM((IDX_HEADS * TQ, IDX_DIM), BF16),
                        pltpu.VMEM((IDX_HEADS, TQ, LANES), F32),
                        pltpu.VMEM((N_HEADS * TQ, 2 * HEAD_DIM), BF16),
                        pltpu.VMEM((N_HEADS * TQ, CK), BF16),
                        pltpu.VMEM((N_HEADS, TQ, LANES), F32),
                        pltpu.VMEM((N_HEADS, TQ, 2 * HEAD_DIM), F32)],
        compiler_params=_params(2),
        name="dsa_attention",
    )(rel_bias, qi, kvw, q, knorm_g.reshape(1, IDX_DIM), knorm_b.reshape(1, IDX_DIM))


def _merge_kernel(yl_ref, ya_ref, wl_ref, wa_ref, gl_ref, ga_ref, o_ref):
    pl_ = jnp.dot(yl_ref[...], wl_ref[...], preferred_element_type=F32)
    pa_ = jnp.dot(ya_ref[...], wa_ref[...], preferred_element_type=F32)
    merged = (_sigmoid(gl_ref[...].astype(F32)) * pl_
              + _sigmoid(ga_ref[...].astype(F32)) * pa_)
    o_ref[...] = merged.astype(o_ref.dtype)


def _gated_merge(y_lru, y_att, w_l, w_a, gates, tm, tn):
    M, D = y_lru.shape
    N = w_l.shape[1]
    nj = N // tn
    return pl.pallas_call(
        _merge_kernel,
        grid=(M // tm, nj),
        in_specs=[pl.BlockSpec((tm, D), lambda i, j: (i, 0)),
                  pl.BlockSpec((tm, D), lambda i, j: (i, 0)),
                  pl.BlockSpec((D, tn), lambda i, j: (0, j)),
                  pl.BlockSpec((D, tn), lambda i, j: (0, j)),
                  pl.BlockSpec((tm, tn), lambda i, j: (i, j)),
                  pl.BlockSpec((tm, tn), lambda i, j: (i, nj + j))],
        out_specs=pl.BlockSpec((tm, tn), lambda i, j: (i, j)),
        out_shape=jax.ShapeDtypeStruct((M, N), BF16),
        compiler_params=_params(2),
        name="gated_merge",
    )(y_lru, y_att, w_l, w_a, gates, gates)


def _layer_norm_rows(v, g, b):
    mu = jnp.mean(v, axis=-1, keepdims=True)
    var = jnp.mean(jnp.square(v - mu), axis=-1, keepdims=True)
    return (v - mu) * lax.rsqrt(var + LN_EPS) * g + b


def _outproj_ln_kernel(m_ref, x_ref, w_ref, g_ref, b_ref, o32_ref, o16_ref):
    proj = jnp.dot(m_ref[...], w_ref[...], preferred_element_type=F32)
    h = _layer_norm_rows(ALPHA * x_ref[...] + proj, g_ref[...], b_ref[...])
    o32_ref[...] = h
    o16_ref[...] = h.astype(BF16)


def _outproj_ln(merged, x2d, w_out, g, b, tm):
    M, D = x2d.shape
    return pl.pallas_call(
        _outproj_ln_kernel,
        grid=(M // tm,),
        in_specs=[pl.BlockSpec((tm, D), lambda i: (i, 0)),
                  pl.BlockSpec((tm, D), lambda i: (i, 0)),
                  pl.BlockSpec((D, D), lambda i: (0, 0)),
                  pl.BlockSpec((1, D), lambda i: (0, 0)),
                  pl.BlockSpec((1, D), lambda i: (0, 0))],
        out_specs=[pl.BlockSpec((tm, D), lambda i: (i, 0)),
                   pl.BlockSpec((tm, D), lambda i: (i, 0))],
        out_shape=[jax.ShapeDtypeStruct((M, D), F32),
                   jax.ShapeDtypeStruct((M, D), BF16)],
        compiler_params=_params(1),
        name="outproj_ln1",
    )(merged, x2d, w_out, g.reshape(1, D), b.reshape(1, D))


def _ffn_up_kernel(halo_ref, h_ref, wg_ref, wv_ref, cwg_ref, cwv_ref, cbg_ref, cbv_ref, o_ref,
                   wg16, wv16, *, tm, conv_w, tiles_per_seq):
    i = pl.program_id(1)

    @pl.when(i == 0)
    def _():
        wg16[...] = wg_ref[...].astype(BF16)
        wv16[...] = wv_ref[...].astype(BF16)

    keep = jnp.where(i % tiles_per_seq > 0, 1.0, 0.0).astype(BF16)
    lhs = jnp.concatenate([halo_ref[...] * keep, h_ref[...]], axis=0)

    def conv_half(lhs_rows, n_rows, w_ref, cw_ref, cb_ref):
        up = jnp.dot(lhs_rows, w_ref[...], preferred_element_type=F32)
        y = cb_ref[...] + cw_ref[conv_w - 1:conv_w, :] * up[SUBLANES:SUBLANES + n_rows]
        for j in range(conv_w - 1):
            back = pltpu.roll(up, conv_w - 1 - j, axis=0)
            y = y + cw_ref[j:j + 1, :] * back[SUBLANES:SUBLANES + n_rows]
        return y

    yg = conv_half(lhs, tm, wg16, cwg_ref, cbg_ref)
    yv = conv_half(lhs, tm, wv16, cwv_ref, cbv_ref)
    o_ref[...] = (_gelu_tanh(yg) * yv).astype(o_ref.dtype)


def _ffn_up(h16, w_up, conv_w, conv_b, S, tm, tn):
    M, D = h16.shape
    d_ff = w_up.shape[1] // 2
    width = conv_w.shape[0]
    assert width - 1 <= SUBLANES and S % tm == 0
    nj = d_ff // tn
    kern = functools.partial(_ffn_up_kernel, tm=tm, conv_w=width, tiles_per_seq=S // tm)
    halo_blocks = tm // SUBLANES
    cb2 = conv_b.reshape(1, 2 * d_ff)
    return pl.pallas_call(
        kern,
        grid=(nj, M // tm),
        in_specs=[pl.BlockSpec((SUBLANES, D), lambda j, i: (jnp.maximum(i * halo_blocks - 1, 0), 0)),
                  pl.BlockSpec((tm, D), lambda j, i: (i, 0)),
                  pl.BlockSpec((D, tn), lambda j, i: (0, j)),
                  pl.BlockSpec((D, tn), lambda j, i: (0, nj + j)),
                  pl.BlockSpec((width, tn), lambda j, i: (0, j)),
                  pl.BlockSpec((width, tn), lambda j, i: (0, nj + j)),
                  pl.BlockSpec((1, tn), lambda j, i: (0, j)),
                  pl.BlockSpec((1, tn), lambda j, i: (0, nj + j))],
        out_specs=pl.BlockSpec((tm, tn), lambda j, i: (i, j)),
        out_shape=jax.ShapeDtypeStruct((M, d_ff), BF16),
        scratch_shapes=[pltpu.VMEM((D, tn), BF16), pltpu.VMEM((D, tn), BF16)],
        compiler_params=_params(2),
        name="ffn_up_conv_geglu",
    )(h16, h16, w_up, w_up, conv_w, conv_w, cb2, cb2)


def _ffn_down_kernel(a_ref, h_ref, w_ref, g_ref, b_ref, o_ref):
    k = pl.program_id(1)

    @pl.when(k == 0)
    def _():
        o_ref[...] = jnp.dot(a_ref[...], w_ref[...], preferred_element_type=F32)

    @pl.when(k > 0)
    def _():
        o_ref[...] += jnp.dot(a_ref[...], w_ref[...], preferred_element_type=F32)

    @pl.when(k == pl.num_programs(1) - 1)
    def _():
        o_ref[...] = _layer_norm_rows(ALPHA * h_ref[...] + o_ref[...], g_ref[...], b_ref[...])


def _ffn_down_ln(act, h32, w_down, g, b, tm, tk):
    d_ff, D = w_down.shape
    M = act.shape[0]
    return pl.pallas_call(
        _ffn_down_kernel,
        grid=(M // tm, d_ff // tk),
        in_specs=[pl.BlockSpec((tm, tk), lambda i, k: (i, k)),
                  pl.BlockSpec((tm, D), lambda i, k: (i, 0), pipeline_mode=pl.Buffered(1)),
                  pl.BlockSpec((tk, D), lambda i, k: (k, 0)),
                  pl.BlockSpec((1, D), lambda i, k: (0, 0)),
                  pl.BlockSpec((1, D), lambda i, k: (0, 0))],
        out_specs=pl.BlockSpec((tm, D), lambda i, k: (i, 0)),
        out_shape=jax.ShapeDtypeStruct((M, D), F32),
        compiler_params=_params(2),
        name="ffn_down_ln2",
    )(act, h32, w_down, g.reshape(1, D), b.reshape(1, D))


def kernel(x, w_in, lru_conv_w, lru_conv_b, lru_gate_a_w, lru_gate_a_b, lru_gate_x_w, lru_gate_x_b,
           lru_lambda, idx_knorm_g, idx_knorm_b, rel_bias, w_proj_lru, w_proj_attn, w_out,
           ln1_g, ln1_b, ffn_w_up, ffn_conv_w, ffn_conv_b, ffn_w_down, ln2_g, ln2_b):
    B, S, D = x.shape
    assert B == SUBLANES and S % CK == 0 and w_in.shape[0] == DEPTH
    top_k = min(TOPK_MAX, S // 4)
    d_rnn = lru_conv_w.shape[-1]
    q_w = N_HEADS * HEAD_DIM
    qi_w = IDX_HEADS * IDX_DIM
    splits = (d_rnn, d_rnn, q_w, HEAD_DIM, HEAD_DIM, qi_w, IDX_DIM, IDX_HEADS, D, D)
    offs = [0]
    for s_ in splits:
        offs.append(offs[-1] + s_)
    M = B * S

    l = 0
    w = w_in[l].astype(BF16)
    x2d = x.reshape(M, D)
    w_lru = w[:, offs[0]:offs[2]]
    w_q = w[:, offs[2]:offs[3]]
    pad = LANES - IDX_DIM - IDX_HEADS
    w_kvw = jnp.concatenate([w[:, offs[3]:offs[5]], w[:, offs[6]:offs[8]],
                             jnp.zeros((D, pad), BF16)], axis=1)
    w_qi = w[:, offs[5]:offs[6]]
    w_g = w[:, offs[8]:offs[10]]

    tm_in = min(M, 2048)
    lxg, x16 = _project_and_cast(x2d, w_lru, BF16, min(M, 1024), 1024)
    lxg = lxg.reshape(B, S, 2 * d_rnn)
    q = _project(x16, w_q, BF16, tm_in, 1024,
                 out_scale=(HEAD_DIM ** -0.5) * LOG2E).reshape(B, S, q_w)
    kvw = _project(x16, w_kvw, F32, tm_in, w_kvw.shape[1]).reshape(B, S, w_kvw.shape[1])
    qi = _project(x16, w_qi, BF16, tm_in, 1024).reshape(B, S, qi_w)
    gates = _project(x16, w_g, BF16, tm_in, 1024)

    y_lru = _rg_lru(lxg, lru_conv_w[l], lru_conv_b[l], lru_gate_a_w[l], lru_gate_a_b[l],
                    lru_gate_x_w[l], lru_gate_x_b[l], lru_lambda[l], S, B, tc=min(S, 512), cbw=256)

    y_att = _sparse_attention(qi, kvw, q, rel_bias, idx_knorm_g[l], idx_knorm_b[l], top_k)

    merged = _gated_merge(y_lru.reshape(M, d_rnn), y_att.reshape(M, q_w),
                          w_proj_lru[l].astype(BF16), w_proj_attn[l].astype(BF16),
                          gates, tm=min(M, 1024), tn=512)
    h32, h16 = _outproj_ln(merged, x2d, w_out[l].astype(BF16), ln1_g[l], ln1_b[l], tm=512)
    act = _ffn_up(h16, ffn_w_up[l], ffn_conv_w[l], ffn_conv_b[l], S,
                  tm=min(S, 1024), tn=768)
    out = _ffn_down_ln(act, h32, ffn_w_down[l].astype(BF16), ln2_g[l], ln2_b[l],
                       tm=min(M, 512), tk=3072)
    return out.reshape(B, S, D)
```
